```python
import math
import jax, jax.numpy as jnp
from jax import lax
import numpy as np


D_MODEL = 2048
BATCH = 16
SEQ = 256
DEPTH = 2
DEC_BATCH = 2
DEC_SEQ = 4096
PAST_LEN = 512

GRID_W = 64
D_MIX = D_MODEL
SSD_HEAD_DIM = 64
D_SSD = D_MIX // 2
SSD_HEADS = D_SSD // SSD_HEAD_DIM
SSD_GROUPS = 2
SSD_STATE = 128
SSD_BC = SSD_GROUPS * SSD_STATE
SSD_CONV_CH = D_SSD + 2 * SSD_BC
SSD_CHUNK = 64
GLA_HEADS = 4
D_GLA = D_MIX // 4
GLA_DV = D_GLA // GLA_HEADS
GLA_DK = GLA_DV // 2
GLA_QK = GLA_HEADS * GLA_DK
GLA_LR = 16
GLA_GATE_NORM = 16.0
GLA_CHUNK = 16
RWKV_N = 64
D_RWKV = D_MIX - D_SSD - D_GLA
RWKV_HEADS = D_RWKV // RWKV_N
RWKV_LW = 64
RWKV_LA = 64
RWKV_LG = 128
P_RWKV = 3 * D_RWKV + 2 * RWKV_LW + 2 * RWKV_LA + RWKV_LG
IN_SPLITS = (D_SSD, SSD_CONV_CH, 2 * SSD_HEADS, GLA_QK, GLA_QK, D_GLA, D_GLA, 2 * GLA_LR, P_RWKV)
P_IN = sum(IN_SPLITS)
N_EXPERTS = 16
MOE_GROUPS = 4
EXPERTS_PER_GROUP = N_EXPERTS // MOE_GROUPS
MOE_TOP_K = 2
D_EXPERT = D_MODEL // 2
RMS_EPS = 1e-6
RWKV_LN_EPS = 64e-5

kernel_name = 'hybrid_ssd_gla_rwkv7_moe_diffusion_step'


def rmsnorm(x, g):
    xf = x.astype(jnp.float32)
    y = xf * lax.rsqrt(jnp.mean(xf * xf, axis=-1, keepdims=True) + RMS_EPS)
    return (y * g.astype(jnp.float32)).astype(x.dtype)


def rms_unit(x):
    return x * lax.rsqrt(jnp.mean(x * x, axis=-1, keepdims=True) + RMS_EPS)


def modulate(h, shift, scale):
    return h * (1 + scale[:, None]) + shift[:, None]


def split_cols(u, sizes):
    out, start = [], 0
    for s in sizes:
        out.append(u[..., start:start + s])
        start += s
    return out


def to_col_major(t, rows):
    b, n = t.shape[:2]
    return t.reshape(b, rows, GRID_W, -1).swapaxes(1, 2).reshape(b, n, -1)


def from_col_major(t, rows):
    b, n = t.shape[:2]
    return t.reshape(b, GRID_W, rows, -1).swapaxes(1, 2).reshape(b, n, -1)


def dwconv2d(x, w, bias):
    ch = x.shape[-1]
    y = lax.conv_general_dilated(x, w.astype(x.dtype)[:, :, None, :], window_strides=(1, 1), padding='SAME',
                                 dimension_numbers=('NHWC', 'HWIO', 'NHWC'), feature_group_count=ch)
    return y + bias.astype(x.dtype)


def ssd_chunked(x, dt, a, bm, cm, s0):
    b, n, nh, hp = x.shape
    ng, ns = bm.shape[2], bm.shape[3]
    L = SSD_CHUNK
    nc = n // L
    hpg = nh // ng
    xc = (x * dt[..., None]).reshape(b, nc, L, ng, hpg, hp)
    bc = bm.reshape(b, nc, L, ng, ns)
    cc = cm.reshape(b, nc, L, ng, ns)
    cum = jnp.cumsum((dt * a).reshape(b, nc, L, nh), axis=2)
    mask = jnp.tril(jnp.ones((L, L), bool))[None, None, :, :, None]
    seg = cum[:, :, :, None, :] - cum[:, :, None, :, :]
    decay = jnp.exp(jnp.where(mask, seg, -jnp.inf)).reshape(b, nc, L, L, ng, hpg)
    scores = jnp.einsum('bclgn,bcsgn->bclsg', cc, bc)
    y_diag = jnp.einsum('bclsgh,bcsghp->bclghp', scores[..., None] * decay, xc)
    to_end = jnp.exp(cum[:, :, -1:, :] - cum).reshape(b, nc, L, ng, hpg)
    states = jnp.einsum('bclgn,bclghp->bcghpn', bc, xc * to_end[..., None]).reshape(b, nc, nh, hp, ns)
    chunk_decay = jnp.exp(cum[:, :, -1, :])

    def step(s, inp):
        st, dec = inp
        return s * dec[:, :, None, None] + st, s

    s_fin, s_prev = lax.scan(step, s0, (jnp.moveaxis(states, 1, 0), jnp.moveaxis(chunk_decay, 1, 0)))
    s_prev = jnp.moveaxis(s_prev, 0, 1).reshape(b, nc, ng, hpg, hp, ns)
    y_off = jnp.einsum('bclgn,bcghpn->bclghp', cc, s_prev) * jnp.exp(cum).reshape(b, nc, L, ng, hpg)[..., None]
    return (y_diag + y_off).reshape(b, n, nh, hp), s_fin


def gla_chunked(q, k, v, lg, s0):
    b, n, nh, dk = q.shape
    dv = v.shape[-1]
    L = GLA_CHUNK
    nc = n // L
    q = q.reshape(b, nc, L, nh, dk)
    k = k.reshape(b, nc, L, nh, dk)
    v = v.reshape(b, nc, L, nh, dv)
    cum = jnp.cumsum(lg.reshape(b, nc, L, nh, dk), axis=2)
    mask = jnp.tril(jnp.ones((L, L), bool))[None, None, :, :, None, None]
    seg = jnp.exp(jnp.where(mask, cum[:, :, :, None] - cum[:, :, None], -jnp.inf))
    attn = jnp.einsum('bclhk,bcshk,bclshk->bclsh', q, k, seg)
    o_intra = jnp.einsum('bclsh,bcshv->bclhv', attn, v)
    states = jnp.einsum('bcshk,bcshv->bchkv', k * jnp.exp(cum[:, :, -1:] - cum), v)
    chunk_decay = jnp.exp(cum[:, :, -1])

    def step(s, inp):
        st, dec = inp
        return s * dec[..., None] + st, s

    s_fin, s_prev = lax.scan(step, s0, (jnp.moveaxis(states, 1, 0), jnp.moveaxis(chunk_decay, 1, 0)))
    s_prev = jnp.moveaxis(s_prev, 0, 1)
    o_inter = jnp.einsum('bclhk,bchkv->bclhv', q * jnp.exp(cum), s_prev)
    return (o_intra + o_inter).reshape(b, n, nh, dv), s_fin


def rwkv_scan(r, w, k, v, kk, a, s0, reverse):
    def step(s, inp):
        r_t, w_t, k_t, v_t, kk_t, a_t = inp
        sa = jnp.einsum('bhvk,bhk->bhv', s, -kk_t)
        s = s * w_t[:, :, None, :] + sa[..., None] * (kk_t * a_t)[:, :, None, :] + v_t[..., None] * k_t[:, :, None, :]
        return s, jnp.einsum('bhvk,bhk->bhv', s, r_t)

    xs = tuple(jnp.moveaxis(t, 1, 0) for t in (r, w, k, v, kk, a))
    s_fin, out = lax.scan(step, s0, xs, reverse=reverse)
    return jnp.moveaxis(out, 0, 1), s_fin


def token_mixers(h, s0, rows, cols, col_major, p, l):
    b, n, _ = h.shape
    f32 = jnp.float32
    s_ssd, s_gla, s_rwkv = s0
    flip = lambda t: jnp.flip(t, axis=1)
    u = h @ p['w_in'][l]
    z, xbc, dt_raw, q, k, v, g, glr, rw = split_cols(u, IN_SPLITS)
    xbc = jax.nn.silu(dwconv2d(xbc.reshape(b, rows, cols, SSD_CONV_CH), p['ssd_conv_w'][l],
                               p['ssd_conv_b'][l])).reshape(b, n, SSD_CONV_CH)
    parts = [z, xbc, dt_raw, q, k, v, g, glr, rw]
    if col_major:
        parts = [to_col_major(t, rows) for t in parts]
    z, xbc, dt_raw, q, k, v, g, glr, rw = [t.astype(f32) for t in parts]

    xs, bm, cm = split_cols(xbc, (D_SSD, SSD_BC, SSD_BC))
    xs = xs.reshape(b, n, SSD_HEADS, SSD_HEAD_DIM)
    bm = bm.reshape(b, n, SSD_GROUPS, SSD_STATE)
    cm = cm.reshape(b, n, SSD_GROUPS, SSD_STATE)
    dt = jax.nn.softplus(dt_raw.reshape(b, n, 2, SSD_HEADS) + p['ssd_dt_bias'][l].astype(f32))
    a = -jnp.exp(p['ssd_a_log'][l].astype(f32))
    y_f, sf = ssd_chunked(xs, dt[:, :, 0], a[0], bm, cm, s_ssd[:, 0])
    y_b, sb = ssd_chunked(flip(xs), flip(dt[:, :, 1]), a[1], flip(bm), flip(cm), s_ssd[:, 1])
    y = (y_f + flip(y_b) + xs * p['ssd_d'][l].astype(f32)[:, None]).reshape(b, n, D_SSD) * jax.nn.silu(z)
    y = rms_unit(y.reshape(b, n, SSD_GROUPS, D_SSD // SSD_GROUPS)).reshape(b, n, D_SSD) * p['ssd_norm_g'][l].astype(f32)

    qh = q.reshape(b, n, GLA_HEADS, GLA_DK) * GLA_DK ** -0.5
    kh = k.reshape(b, n, GLA_HEADS, GLA_DK)
    vh = v.reshape(b, n, GLA_HEADS, GLA_DV)
    gl = jnp.einsum('btdr,drk->btdk', glr.reshape(b, n, 2, GLA_LR), p['gla_gate_up'][l].astype(f32)) + p['gla_gate_b'][l].astype(f32)
    lg = (jax.nn.log_sigmoid(gl) / GLA_GATE_NORM).reshape(b, n, 2, GLA_HEADS, GLA_DK)
    o_f, gf = gla_chunked(qh, kh, vh, lg[:, :, 0], s_gla[:, 0])
    o_b, gb = gla_chunked(flip(qh), flip(kh), flip(vh), flip(lg[:, :, 1]), s_gla[:, 1])
    o_gla = (rms_unit(o_f + flip(o_b)) * p['gla_norm_g'][l].astype(f32)
             * jax.nn.silu(g.reshape(b, n, GLA_HEADS, GLA_DV))).reshape(b, n, D_GLA)

    mu = p['rwkv_mu'][l].astype(f32)
    prev = jnp.pad(rw, ((0, 0), (1, 0), (0, 0)))[:, :-1]
    nxt = jnp.pad(rw, ((0, 0), (0, 1), (0, 0)))[:, 1:]
    rw = rw + mu[0] * (prev - rw) + mu[1] * (nxt - rw)
    r, kr, vr, wd, ad, gd = split_cols(rw, (D_RWKV, D_RWKV, D_RWKV, 2 * RWKV_LW, 2 * RWKV_LA, RWKV_LG))
    w_log = jax.nn.log_sigmoid(p['rwkv_w0'][l].astype(f32) + jnp.einsum(
        'btdr,drc->btdc', jnp.tanh(wd.reshape(b, n, 2, RWKV_LW)), p['rwkv_w2'][l].astype(f32))) - 0.5
    decay = jnp.exp(-jnp.exp(w_log))
    a_icl = jax.nn.sigmoid(p['rwkv_a0'][l].astype(f32) + jnp.einsum(
        'btdr,drc->btdc', ad.reshape(b, n, 2, RWKV_LA), p['rwkv_a2'][l].astype(f32)))
    gate = jax.nn.sigmoid(gd) @ p['rwkv_g2'][l].astype(f32)
    kk = (kr * p['rwkv_k_k'][l].astype(f32)).reshape(b, n, RWKV_HEADS, RWKV_N)
    kk = kk * lax.rsqrt(jnp.sum(kk * kk, axis=-1, keepdims=True) + 1e-12)
    k_eff = kr[:, :, None] * (1.0 + (a_icl - 1.0) * p['rwkv_k_a'][l].astype(f32))
    hd = lambda t: t.reshape(b, n, RWKV_HEADS, RWKV_N)
    rh, vrh = hd(r), hd(vr)
    o_rf, rf = rwkv_scan(rh, hd(decay[:, :, 0]), hd(k_eff[:, :, 0]), vrh, kk, hd(a_icl[:, :, 0]), s_rwkv[:, 0], False)
    o_rb, rb = rwkv_scan(rh, hd(decay[:, :, 1]), hd(k_eff[:, :, 1]), vrh, kk, hd(a_icl[:, :, 1]), s_rwkv[:, 1], True)
    o = o_rf + o_rb
    mean = jnp.mean(o, axis=-1, keepdims=True)
    var = jnp.mean((o - mean) ** 2, axis=-1, keepdims=True)
    o = ((o - mean) * lax.rsqrt(var + RWKV_LN_EPS)).reshape(b, n, D_RWKV) * p['rwkv_ln_g'][l].astype(f32) + p['rwkv_ln_b'][l].astype(f32)
    bonus = jnp.sum(rh[:, :, None] * k_eff.reshape(b, n, 2, RWKV_HEADS, RWKV_N), axis=2) * p['rwkv_r_k'][l].astype(f32)
    bonus = jnp.sum(bonus, axis=-1, keepdims=True) * vrh
    o_rwkv = (o + bonus.reshape(b, n, D_RWKV)) * gate

    out = jnp.concatenate([y, o_gla, o_rwkv], axis=-1)
    if col_major:
        out = from_col_major(out, rows)
    out = out.astype(h.dtype) @ p['w_out'][l]
    return out, (jnp.stack([sf, sb], axis=1), jnp.stack([gf, gb], axis=1), jnp.stack([rf, rb], axis=1))


def moe(h, router_w, router_b, w_gate, w_up, w_down):
    b, n, d = h.shape
    ht = h.reshape(b * n, d)
    aff = jax.nn.sigmoid((ht @ router_w).astype(jnp.float32))
    sel = (aff + router_b.astype(jnp.float32)).reshape(-1, MOE_GROUPS, EXPERTS_PER_GROUP)
    group_score = jnp.sum(lax.top_k(sel, 2)[0], axis=-1)
    grp = jnp.argmax(group_score, axis=-1)
    in_grp = jnp.take_along_axis(sel, grp[:, None, None], axis=1)[:, 0]
    _, loc = lax.top_k(in_grp, MOE_TOP_K)
    idx = grp[:, None] * EXPERTS_PER_GROUP + loc
    wts = jnp.take_along_axis(aff, idx, axis=1)
    wts = wts / jnp.sum(wts, axis=-1, keepdims=True)
    gates = jnp.sum(jax.nn.one_hot(idx, N_EXPERTS, dtype=jnp.float32) * wts[..., None], axis=1)
    out = jnp.zeros((b * n, d), jnp.float32)
    for e in range(N_EXPERTS):
        he = jax.nn.silu(ht @ w_gate[e]) * (ht @ w_up[e])
        out = out + gates[:, e:e + 1] * (he @ w_down[e])
    return out.reshape(b, n, d).astype(h.dtype)


def run_trunk(x, cvec, init_states, latent, p):
    b, n, _ = x.shape
    rows, cols = (n // GRID_W, GRID_W) if latent else (1, n)
    ctx_states = []
    for l in range(DEPTH):
        mod = jax.nn.silu(cvec) @ p['ada_w'][l] + p['ada_b'][l]
        sh1, sc1, g1, sh2, sc2, g2 = jnp.split(mod, 6, axis=-1)
        h = modulate(rmsnorm(x, p['norm_mix_g'][l]), sh1, sc1)
        if latent:
            s0 = tuple(s[:, l].astype(jnp.float32) for s in init_states)
        else:
            s0 = (jnp.zeros((b, 2, SSD_HEADS, SSD_HEAD_DIM, SSD_STATE), jnp.float32),
                  jnp.zeros((b, 2, GLA_HEADS, GLA_DK, GLA_DV), jnp.float32),
                  jnp.zeros((b, 2, RWKV_HEADS, RWKV_N, RWKV_N), jnp.float32))
        mix, new = token_mixers(h, s0, rows, cols, latent and (l % 2 == 1), p, l)
        x = x + g1[:, None] * mix
        h = modulate(rmsnorm(x, p['norm_ffn_g'][l]), sh2, sc2)
        x = x + g2[:, None] * moe(h, p['router_w'], p['router_b'], p['moe_w_gate'][l], p['moe_w_up'][l], p['moe_w_down'][l])
        ctx_states.append(new)
    return rmsnorm(x, p['final_norm_g']), ctx_states


def setup_inputs(seed: int = 0) -> dict:
    key = jax.random.key(seed)
    keys = jax.random.split(key, 48)
    counter = [0]

    def nk():
        counter[0] += 1
        return keys[counter[0] - 1]

    def nrm(shape, scale):
        return jax.random.normal(nk(), shape, jnp.float32) * scale

    def unif(shape, lo, hi):
        return jax.random.uniform(nk(), shape, jnp.float32, lo, hi)

    D = D_MODEL
    inp = {}
    inp['x_prompt'] = nrm((BATCH, SEQ, D), 1.0)
    inp['x_sample'] = nrm((DEC_BATCH, DEC_SEQ, D), 1.0)
    inp['state_ssd'] = nrm((DEC_BATCH, DEPTH, 2, SSD_HEADS, SSD_HEAD_DIM, SSD_STATE), 0.1)
    inp['state_gla'] = nrm((DEC_BATCH, DEPTH, 2, GLA_HEADS, GLA_DK, GLA_DV), 0.3)
    inp['state_rwkv'] = nrm((DEC_BATCH, DEPTH, 2, RWKV_HEADS, RWKV_N, RWKV_N), 0.3)
    inp['c'] = nrm((DEC_BATCH, D), 1.0)
    inp['c_ctx'] = nrm((D,), 1.0)
    inp['ada_w'] = nrm((DEPTH, D, 6 * D), 0.5 * D ** -0.5)
    inp['ada_b'] = nrm((DEPTH, 6 * D), 0.02)
    inp['norm_mix_g'] = 1.0 + nrm((DEPTH, D), 0.02)
    inp['norm_ffn_g'] = 1.0 + nrm((DEPTH, D), 0.02)
    inp['w_in'] = nrm((DEPTH, D, P_IN), D ** -0.5)
    inp['ssd_conv_w'] = nrm((DEPTH, 3, 3, SSD_CONV_CH), 1.0 / 3.0)
    inp['ssd_conv_b'] = nrm((DEPTH, SSD_CONV_CH), 0.02)
    dt0 = jnp.exp(unif((DEPTH, 2, SSD_HEADS), math.log(1e-3), math.log(1e-1)))
    inp['ssd_dt_bias'] = dt0 + jnp.log(-jnp.expm1(-dt0))
    inp['ssd_a_log'] = jnp.log(unif((DEPTH, 2, SSD_HEADS), 1.0, 16.0))
    inp['ssd_d'] = 1.0 + nrm((DEPTH, SSD_HEADS), 0.1)
    inp['ssd_norm_g'] = 1.0 + nrm((DEPTH, D_SSD), 0.02)
    inp['gla_gate_up'] = nrm((DEPTH, 2, GLA_LR, GLA_QK), GLA_LR ** -0.5)
    inp['gla_gate_b'] = nrm((DEPTH, 2, GLA_QK), 0.1)
    inp['gla_norm_g'] = 1.0 + nrm((DEPTH, GLA_DV), 0.02)
    inp['rwkv_mu'] = unif((DEPTH, 2, P_RWKV), 0.0, 0.5)
    inp['rwkv_w0'] = unif((DEPTH, 2, D_RWKV), -6.0, 0.0)
    inp['rwkv_w2'] = nrm((DEPTH, 2, RWKV_LW, D_RWKV), 0.1 * RWKV_LW ** -0.5)
    inp['rwkv_a0'] = nrm((DEPTH, 2, D_RWKV), 0.5)
    inp['rwkv_a2'] = nrm((DEPTH, 2, RWKV_LA, D_RWKV), 0.5 * RWKV_LA ** -0.5)
    inp['rwkv_g2'] = nrm((DEPTH, RWKV_LG, D_RWKV), RWKV_LG ** -0.5)
    inp['rwkv_k_k'] = 0.85 + nrm((DEPTH, D_RWKV), 0.05)
    inp['rwkv_k_a'] = 1.0 + nrm((DEPTH, D_RWKV), 0.05)
    inp['rwkv_r_k'] = nrm((DEPTH, RWKV_HEADS, RWKV_N), 0.1)
    inp['rwkv_ln_g'] = 1.0 + nrm((DEPTH, D_RWKV), 0.02)
    inp['rwkv_ln_b'] = nrm((DEPTH, D_RWKV), 0.02)
    inp['w_out'] = nrm((DEPTH, D_MIX, D), D_MIX ** -0.5)
    inp['router_w'] = nrm((D, N_EXPERTS), D ** -0.5)
    inp['router_b'] = nrm((N_EXPERTS,), 0.01)
    inp['moe_w_gate'] = nrm((DEPTH, N_EXPERTS, D, D_EXPERT), D ** -0.5)
    inp['moe_w_up'] = nrm((DEPTH, N_EXPERTS, D, D_EXPERT), D ** -0.5)
    inp['moe_w_down'] = nrm((DEPTH, N_EXPERTS, D_EXPERT, D), D_EXPERT ** -0.5)
    inp['final_norm_g'] = 1.0 + nrm((D,), 0.02)
    return inp


def reference(x_prompt, x_sample, state_ssd, state_gla, state_rwkv, c, c_ctx, ada_w, ada_b, norm_mix_g, norm_ffn_g,
              w_in, ssd_conv_w, ssd_conv_b, ssd_dt_bias, ssd_a_log, ssd_d, ssd_norm_g, gla_gate_up, gla_gate_b,
              gla_norm_g, rwkv_mu, rwkv_w0, rwkv_w2, rwkv_a0, rwkv_a2, rwkv_g2, rwkv_k_k, rwkv_k_a, rwkv_r_k,
              rwkv_ln_g, rwkv_ln_b, w_out, router_w, router_b, moe_w_gate, moe_w_up, moe_w_down, final_norm_g):
    p = dict(ada_w=ada_w, ada_b=ada_b, norm_mix_g=norm_mix_g, norm_ffn_g=norm_ffn_g, w_in=w_in,
             ssd_conv_w=ssd_conv_w, ssd_conv_b=ssd_conv_b, ssd_dt_bias=ssd_dt_bias, ssd_a_log=ssd_a_log,
             ssd_d=ssd_d, ssd_norm_g=ssd_norm_g, gla_gate_up=gla_gate_up, gla_gate_b=gla_gate_b,
             gla_norm_g=gla_norm_g, rwkv_mu=rwkv_mu, rwkv_w0=rwkv_w0, rwkv_w2=rwkv_w2, rwkv_a0=rwkv_a0,
             rwkv_a2=rwkv_a2, rwkv_g2=rwkv_g2, rwkv_k_k=rwkv_k_k, rwkv_k_a=rwkv_k_a, rwkv_r_k=rwkv_r_k,
             rwkv_ln_g=rwkv_ln_g, rwkv_ln_b=rwkv_ln_b, w_out=w_out, router_w=router_w, router_b=router_b,
             moe_w_gate=moe_w_gate, moe_w_up=moe_w_up, moe_w_down=moe_w_down, final_norm_g=final_norm_g)
    y_prompt, ctx_states = run_trunk(x_prompt, c_ctx[None, :], None, False, p)
    y_sample, _ = run_trunk(x_sample, c, (state_ssd, state_gla, state_rwkv), True, p)
    new_state_ssd = jnp.stack([s[0] for s in ctx_states], axis=1).astype(state_ssd.dtype)
    new_state_gla = jnp.stack([s[1] for s in ctx_states], axis=1).astype(state_gla.dtype)
    new_state_rwkv = jnp.stack([s[2] for s in ctx_states], axis=1).astype(state_rwkv.dtype)
    return (y_prompt, y_sample, new_state_ssd, new_state_gla, new_state_rwkv)
```

```python
import functools

import jax
import jax.numpy as jnp
from jax import lax
from jax.experimental import pallas as pl
from jax.experimental.pallas import tpu as pltpu

F32 = jnp.float32
BF16 = jnp.bfloat16

D_MODEL = 2048
DEPTH = 2
GRID_W = 64
SSD_HEAD_DIM = 64
D_SSD = 1024
SSD_HEADS = 16
SSD_GROUPS = 2
SSD_STATE = 128
SSD_BC = SSD_GROUPS * SSD_STATE
SSD_CONV_CH = D_SSD + 2 * SSD_BC
SSD_CHUNK = 64
GLA_HEADS = 4
D_GLA = 512
GLA_DV = 128
GLA_DK = 64
GLA_QK = GLA_HEADS * GLA_DK
GLA_LR = 16
GLA_GATE_NORM = 16.0
GLA_CHUNK = 16
RWKV_N = 64
D_RWKV = 512
RWKV_HEADS = 8
RWKV_LW = 64
RWKV_LA = 64
RWKV_LG = 128
P_RWKV = 3 * D_RWKV + 2 * RWKV_LW + 2 * RWKV_LA + RWKV_LG
IN_SPLITS = (D_SSD, SSD_CONV_CH, 2 * SSD_HEADS, GLA_QK, GLA_QK, D_GLA, D_GLA, 2 * GLA_LR, P_RWKV)
N_EXPERTS = 16
MOE_GROUPS = 4
EXPERTS_PER_GROUP = 4
MOE_TOP_K = 2
D_EXPERT = 1024
RMS_EPS = 1e-6
RWKV_LN_EPS = 64e-5

RWKV_CHUNK = 64
LANE = 128
P_IN_PAD = 6144
MOE_TM = 256
VMEM_LIMIT = 56 * 1024 * 1024


def _dot(a, b, contract, passes):
    dn = (contract, ((), ()))
    if passes == 6:
        return lax.dot_general(a, b, dn, precision=lax.Precision.HIGHEST, preferred_element_type=F32)
    ah = a.astype(BF16)
    bh = b.astype(BF16)
    out = lax.dot_general(ah, bh, dn, preferred_element_type=F32)
    if passes == 3:
        al = (a - ah.astype(F32)).astype(BF16)
        bl = (b - bh.astype(F32)).astype(BF16)
        out = out + lax.dot_general(ah, bl, dn, preferred_element_type=F32)
        out = out + lax.dot_general(al, bh, dn, preferred_element_type=F32)
    return out


_NN = ((1,), (0,))
_NT = ((1,), (1,))
_TN = ((0,), (0,))


def _params(*sem):
    return pltpu.CompilerParams(dimension_semantics=sem, vmem_limit_bytes=VMEM_LIMIT)


def _rwkv_kernel(r_ref, v_ref, kk_ref, lw_ref, k_ref, a_ref, s0_ref, o_ref, sfin_ref, s_scr, *, p_aa, p_inv, p_st):
    d = pl.program_id(1)
    c = pl.program_id(2)
    C = RWKV_CHUNK
    N = RWKV_N

    @pl.when(c == 0)
    def _():
        s_scr[...] = s0_ref[0, 0]

    r = r_ref[0]
    v = v_ref[0]
    kk = kk_ref[0]
    lw = lw_ref[0, 0]
    k = k_ref[0, 0]
    a = a_ref[0, 0]

    row = lax.broadcasted_iota(jnp.int32, (C, C), 0)
    col = lax.broadcasted_iota(jnp.int32, (C, C), 1)
    ahead = (row - col) * (1 - 2 * d)
    incl = ahead >= 0
    strict = ahead > 0
    eye = (col == row).astype(F32)

    cum = _dot(incl.astype(F32), lw, _NN, 6)
    tot = jnp.sum(lw, axis=0, keepdims=True)
    b = a * kk
    kp = kk * jnp.exp(cum - lw)
    rp = r * jnp.exp(cum)
    pinv = jnp.exp(-cum)
    kinv = k * pinv
    binv = b * pinv
    pend = jnp.exp(tot - cum)
    kd = k * pend
    bd = b * pend
    ptot = jnp.exp(tot)

    for h in range(RWKV_HEADS):
        sl = slice(h * N, (h + 1) * N)
        s0 = s_scr[h]
        vh = v[:, sl]
        x = jnp.concatenate([kp[:, sl], rp[:, sl]], axis=0)
        y = jnp.concatenate([kinv[:, sl], binv[:, sl]], axis=0)
        aa = _dot(x, y, _NT, p_aa)
        a_kk = jnp.where(strict, aa[:C, :C], 0.0)
        a_kb = jnp.where(strict, aa[:C, C:], 0.0)
        a_rk = jnp.where(incl, aa[C:, :C], 0.0)
        a_rb = jnp.where(incl, aa[C:, C:], 0.0)
        t = eye - a_kb
        pw = a_kb
        n = 1
        while 2 * n < C + 1:
            pw = _dot(pw, pw, _NN, p_inv)
            t = t + _dot(t, pw, _NN, p_inv)
            n *= 2
        xs = _dot(x, s0, _NT, p_st)
        g = xs[:C] + _dot(a_kk, vh, _NN, p_st)
        u = _dot(t, g, _NN, p_inv)
        o = xs[C:] + _dot(a_rk, vh, _NN, p_st) - _dot(a_rb, u, _NN, p_st)
        o_ref[0, 0, :, sl] = o
        s_scr[h] = s0 * ptot[:, sl] + _dot(vh, kd[:, sl], _TN, p_st) - _dot(u, bd[:, sl], _TN, p_st)

    @pl.when(c == pl.num_programs(2) - 1)
    def _():
        sfin_ref[0, 0] = s_scr[...]


def _rwkv_scan(r, v, kk, lw, keff, a, s0, *, interpret=False, p_aa=1, p_inv=1, p_st=1):
    B, T, HN = r.shape
    C = RWKV_CHUNK
    nc = T // C

    def chunk(d, c):
        return c + d * (nc - 1 - 2 * c)

    tok = pl.BlockSpec((1, C, HN), lambda b, d, c: (b, chunk(d, c), 0))
    tokd = pl.BlockSpec((1, 1, C, HN), lambda b, d, c: (b, d, chunk(d, c), 0))
    st = pl.BlockSpec((1, 1, RWKV_HEADS, RWKV_N, RWKV_N), lambda b, d, c: (b, d, 0, 0, 0))
    return pl.pallas_call(
        functools.partial(_rwkv_kernel, p_aa=p_aa, p_inv=p_inv, p_st=p_st),
        grid=(B, 2, nc),
        in_specs=[tok, tok, tok, tokd, tokd, tokd, st],
        out_specs=[tokd, st],
        out_shape=[jax.ShapeDtypeStruct((B, 2, T, HN), F32),
                   jax.ShapeDtypeStruct((B, 2, RWKV_HEADS, RWKV_N, RWKV_N), F32)],
        scratch_shapes=[pltpu.VMEM((RWKV_HEADS, RWKV_N, RWKV_N), F32)],
        compiler_params=_params("arbitrary", "arbitrary", "arbitrary"),
        interpret=interpret,
    )(r, v, kk, lw, keff, a, s0)


def _ada_kernel(c_ref, w_ref, b_ref, o_ref):
    cv = c_ref[...]
    cv = cv * jax.nn.sigmoid(cv)
    o_ref[0] = _dot(cv, w_ref[0], _NN, 1) + b_ref[0]


def _ada(cvec8, ada_w, ada_b):
    tn = 1536
    n = ada_w.shape[-1]
    return pl.pallas_call(
        _ada_kernel,
        grid=(DEPTH, n // tn),
        in_specs=[pl.BlockSpec((8, D_MODEL), lambda l, j: (0, 0)),
                  pl.BlockSpec((1, D_MODEL, tn), lambda l, j: (l, 0, j)),
                  pl.BlockSpec((1, 1, tn), lambda l, j: (l, 0, j))],
        out_specs=pl.BlockSpec((1, 8, tn), lambda l, j: (l, 0, j)),
        out_shape=jax.ShapeDtypeStruct((DEPTH, 8, n), F32),
        compiler_params=_params("arbitrary", "arbitrary"),
    )(cvec8, ada_w, ada_b.reshape(DEPTH, 1, n))


def _norm_mod(x, g, shift, scale):
    y = x * lax.rsqrt(jnp.mean(x * x, axis=-1, keepdims=True) + RMS_EPS)
    return (y * g) * (1.0 + scale) + shift


def _inproj_kernel(x_ref, g_ref, sh_ref, sc_ref, w_ref, o_ref, h_scr):
    @pl.when(pl.program_id(2) == 0)
    def _():
        h_scr[...] = _norm_mod(x_ref[0], g_ref[...], sh_ref[0], sc_ref[0]).astype(BF16)

    o_ref[0] = jnp.dot(h_scr[...], w_ref[...], preferred_element_type=F32)


def _inproj(x, g, shift, scale, w):
    G, R, D = x.shape
    N = w.shape[1]
    tm, tn = 1024, 1024
    vec = pl.BlockSpec((1, 1, D), lambda gi, i, j: (gi, 0, 0))
    return pl.pallas_call(
        _inproj_kernel,
        grid=(G, R // tm, N // tn),
        in_specs=[pl.BlockSpec((1, tm, D), lambda gi, i, j: (gi, i, 0)),
                  pl.BlockSpec((1, D), lambda gi, i, j: (0, 0)),
                  vec, vec,
                  pl.BlockSpec((D, tn), lambda gi, i, j: (0, j))],
        out_specs=pl.BlockSpec((1, tm, tn), lambda gi, i, j: (gi, i, j)),
        out_shape=jax.ShapeDtypeStruct((G, R, N), F32),
        scratch_shapes=[pltpu.VMEM((tm, D), BF16)],
        compiler_params=_params("arbitrary", "arbitrary", "arbitrary"),
    )(x, g.reshape(1, D), shift.reshape(G, 1, D), scale.reshape(G, 1, D), w)


def _outproj_kernel(a_ref, w_ref, x_ref, gate_ref, o_ref):
    o_ref[0] = x_ref[0] + gate_ref[0] * jnp.dot(a_ref[0], w_ref[...], preferred_element_type=F32)


def _outproj(a, w, xres, gate):
    G, R, Kd = a.shape
    N = w.shape[1]
    tm, tn = 1024, 1024
    return pl.pallas_call(
        _outproj_kernel,
        grid=(G, R // tm, N // tn),
        in_specs=[pl.BlockSpec((1, tm, Kd), lambda gi, i, j: (gi, i, 0)),
                  pl.BlockSpec((Kd, tn), lambda gi, i, j: (0, j)),
                  pl.BlockSpec((1, tm, tn), lambda gi, i, j: (gi, i, j)),
                  pl.BlockSpec((1, 1, tn), lambda gi, i, j: (gi, 0, j))],
        out_specs=pl.BlockSpec((1, tm, tn), lambda gi, i, j: (gi, i, j)),
        out_shape=jax.ShapeDtypeStruct((G, R, N), F32),
        compiler_params=_params("arbitrary", "arbitrary", "arbitrary"),
    )(a, w, xres, gate.reshape(G, 1, N))


def _ffn_in_kernel(x_ref, g_ref, sh_ref, sc_ref, rw_ref, h_ref, lg_ref):
    h = _norm_mod(x_ref[0], g_ref[...], sh_ref[0], sc_ref[0])
    h_ref[0] = h.astype(BF16)
    lg_ref[0] = _dot(h, rw_ref[...], _NN, 6)


def _ffn_in(x, g, shift, scale, router_w_pad):
    G, R, D = x.shape
    tm = 512
    vec = pl.BlockSpec((1, 1, D), lambda gi, i: (gi, 0, 0))
    return pl.pallas_call(
        _ffn_in_kernel,
        grid=(G, R // tm),
        in_specs=[pl.BlockSpec((1, tm, D), lambda gi, i: (gi, i, 0)),
                  pl.BlockSpec((1, D), lambda gi, i: (0, 0)),
                  vec, vec,
                  pl.BlockSpec((D, LANE), lambda gi, i: (0, 0))],
        out_specs=[pl.BlockSpec((1, tm, D), lambda gi, i: (gi, i, 0)),
                   pl.BlockSpec((1, tm, LANE), lambda gi, i: (gi, i, 0))],
        out_shape=[jax.ShapeDtypeStruct((G, R, D), BF16), jax.ShapeDtypeStruct((G, R, LANE), F32)],
        compiler_params=_params("arbitrary", "arbitrary"),
    )(x, g.reshape(1, D), shift.reshape(G, 1, D), scale.reshape(G, 1, D), router_w_pad)


def _moe_kernel(te_ref, na_ref, x_ref, gw_ref, wg_ref, wu_ref, wd_ref, o_ref):
    i = pl.program_id(0)

    @pl.when(i < na_ref[0])
    def _():
        x = x_ref[...]
        hg = jnp.dot(x, wg_ref[0], preferred_element_type=F32)
        hu = jnp.dot(x, wu_ref[0], preferred_element_type=F32)
        he = (hg * jax.nn.sigmoid(hg) * hu).astype(BF16)
        o_ref[...] = gw_ref[...] * jnp.dot(he, wd_ref[0], preferred_element_type=F32)

    @pl.when(i >= na_ref[0])
    def _():
        o_ref[...] = jnp.zeros_like(o_ref)


def _moe_experts(tile_expert, n_active, xs, gate_w, w_gate, w_up, w_down):
    P, D = xs.shape
    tm = MOE_TM
    grid_spec = pltpu.PrefetchScalarGridSpec(
        num_scalar_prefetch=2,
        grid=(P // tm,),
        in_specs=[pl.BlockSpec((tm, D), lambda i, te, na: (i, 0)),
                  pl.BlockSpec((tm, 1), lambda i, te, na: (i, 0)),
                  pl.BlockSpec((1, D, D_EXPERT), lambda i, te, na: (te[i], 0, 0)),
                  pl.BlockSpec((1, D, D_EXPERT), lambda i, te, na: (te[i], 0, 0)),
                  pl.BlockSpec((1, D_EXPERT, D), lambda i, te, na: (te[i], 0, 0))],
        out_specs=pl.BlockSpec((tm, D), lambda i, te, na: (i, 0)),
    )
    return pl.pallas_call(
        _moe_kernel,
        grid_spec=grid_spec,
        out_shape=jax.ShapeDtypeStruct((P, D), F32),
        compiler_params=_params("arbitrary"),
    )(tile_expert, n_active, xs, gate_w, w_gate, w_up, w_down)


def _route(logits, router_b):
    aff = jax.nn.sigmoid(logits)
    sel = (aff + router_b.astype(F32)).reshape(-1, MOE_GROUPS, EXPERTS_PER_GROUP)
    group_score = jnp.sum(lax.top_k(sel, 2)[0], axis=-1)
    grp = jnp.argmax(group_score, axis=-1)
    in_grp = jnp.take_along_axis(sel, grp[:, None, None], axis=1)[:, 0]
    _, loc = lax.top_k(in_grp, MOE_TOP_K)
    idx = grp[:, None] * EXPERTS_PER_GROUP + loc
    wts = jnp.take_along_axis(aff, idx, axis=1)
    wts = wts / jnp.sum(wts, axis=-1, keepdims=True)
    return idx.astype(jnp.int32), wts


def _moe(h, logits, router_b, w_gate, w_up, w_down):
    M, D = h.shape
    tm = MOE_TM
    idx, wts = _route(logits, router_b)
    n_asg = M * MOE_TOP_K
    P = n_asg + N_EXPERTS * tm
    e_flat = idx.reshape(-1)
    order = jnp.argsort(e_flat, stable=True)
    e_sorted = e_flat[order]
    counts = jnp.sum(jax.nn.one_hot(e_flat, N_EXPERTS, dtype=jnp.int32), axis=0)
    starts = jnp.cumsum(counts) - counts
    pcounts = ((counts + tm - 1) // tm) * tm
    pends = jnp.cumsum(pcounts)
    pstarts = pends - pcounts
    dest = pstarts[e_sorted] + (jnp.arange(n_asg, dtype=jnp.int32) - starts[e_sorted])
    rows_tok = jnp.zeros((P,), jnp.int32).at[dest].set((order // MOE_TOP_K).astype(jnp.int32))
    gate_w = jnp.zeros((P,), F32).at[dest].set(wts.reshape(-1)[order])
    pos = jnp.zeros((n_asg,), jnp.int32).at[order].set(dest).reshape(M, MOE_TOP_K)
    tile_start = jnp.arange(P // tm, dtype=jnp.int32) * tm
    tile_expert = jnp.minimum(jnp.searchsorted(pends, tile_start, side='right'), N_EXPERTS - 1).astype(jnp.int32)
    n_active = (pends[-1] // tm).astype(jnp.int32).reshape(1)
    xs = jnp.take(h, rows_tok, axis=0)
    y = _moe_experts(tile_expert, n_active, xs, gate_w.reshape(P, 1), w_gate, w_up, w_down)
    return jnp.take(y, pos[:, 0], axis=0) + jnp.take(y, pos[:, 1], axis=0)


def _final_norm_kernel(x_ref, g_ref, o_ref):
    x = x_ref[...]
    o_ref[...] = x * lax.rsqrt(jnp.mean(x * x, axis=-1, keepdims=True) + RMS_EPS) * g_ref[...]


def _final_norm(x, g):
    M, D = x.shape
    tm = 512
    return pl.pallas_call(
        _final_norm_kernel,
        grid=(M // tm,),
        in_specs=[pl.BlockSpec((tm, D), lambda i: (i, 0)), pl.BlockSpec((1, D), lambda i: (0, 0))],
        out_specs=pl.BlockSpec((tm, D), lambda i: (i, 0)),
        out_shape=jax.ShapeDtypeStruct((M, D), F32),
        compiler_params=_params("arbitrary"),
    )(x, g.reshape(1, D))


def _rms_unit(x):
    return x * lax.rsqrt(jnp.mean(x * x, axis=-1, keepdims=True) + RMS_EPS)


def _to_col_major(t, rows):
    b, n = t.shape[:2]
    return t.reshape(b, rows, GRID_W, -1).swapaxes(1, 2).reshape(b, n, -1)


def _from_col_major(t, rows):
    b, n = t.shape[:2]
    return t.reshape(b, GRID_W, rows, -1).swapaxes(1, 2).reshape(b, n, -1)


def _dwconv2d(x, w, bias):
    ch = x.shape[-1]
    y = lax.conv_general_dilated(x, w.astype(x.dtype)[:, :, None, :], window_strides=(1, 1), padding='SAME',
                                 dimension_numbers=('NHWC', 'HWIO', 'NHWC'), feature_group_count=ch)
    return y + bias.astype(x.dtype)


def _ssd_chunked(x, dt, a, bm, cm, s0):
    b, n, nh, hp = x.shape
    ng, ns = bm.shape[2], bm.shape[3]
    L = SSD_CHUNK
    nc = n // L
    hpg = nh // ng
    xc = (x * dt[..., None]).reshape(b, nc, L, ng, hpg, hp)
    bc = bm.reshape(b, nc, L, ng, ns)
    cc = cm.reshape(b, nc, L, ng, ns)
    cum = jnp.cumsum((dt * a).reshape(b, nc, L, nh), axis=2)
    mask = jnp.tril(jnp.ones((L, L), bool))[None, None, :, :, None]
    seg = cum[:, :, :, None, :] - cum[:, :, None, :, :]
    decay = jnp.exp(jnp.where(mask, seg, -jnp.inf)).reshape(b, nc, L, L, ng, hpg)
    scores = jnp.einsum('bclgn,bcsgn->bclsg', cc, bc)
    y_diag = jnp.einsum('bclsgh,bcsghp->bclghp', scores[..., None] * decay, xc)
    to_end = jnp.exp(cum[:, :, -1:, :] - cum).reshape(b, nc, L, ng, hpg)
    states = jnp.einsum('bclgn,bclghp->bcghpn', bc, xc * to_end[..., None]).reshape(b, nc, nh, hp, ns)
    chunk_decay = jnp.exp(cum[:, :, -1, :])

    def step(s, inp):
        st, dec = inp
        return s * dec[:, :, None, None] + st, s

    s_fin, s_prev = lax.scan(step, s0, (jnp.moveaxis(states, 1, 0), jnp.moveaxis(chunk_decay, 1, 0)))
    s_prev = jnp.moveaxis(s_prev, 0, 1).reshape(b, nc, ng, hpg, hp, ns)
    y_off = jnp.einsum('bclgn,bcghpn->bclghp', cc, s_prev) * jnp.exp(cum).reshape(b, nc, L, ng, hpg)[..., None]
    return (y_diag + y_off).reshape(b, n, nh, hp), s_fin


def _gla_chunked(q, k, v, lg, s0):
    b, n, nh, dk = q.shape
    dv = v.shape[-1]
    L = GLA_CHUNK
    nc = n // L
    q = q.reshape(b, nc, L, nh, dk)
    k = k.reshape(b, nc, L, nh, dk)
    v = v.reshape(b, nc, L, nh, dv)
    cum = jnp.cumsum(lg.reshape(b, nc, L, nh, dk), axis=2)
    mask = jnp.tril(jnp.ones((L, L), bool))[None, None, :, :, None, None]
    seg = jnp.exp(jnp.where(mask, cum[:, :, :, None] - cum[:, :, None], -jnp.inf))
    attn = jnp.einsum('bclhk,bcshk,bclshk->bclsh', q, k, seg)
    o_intra = jnp.einsum('bclsh,bcshv->bclhv', attn, v)
    states = jnp.einsum('bcshk,bcshv->bchkv', k * jnp.exp(cum[:, :, -1:] - cum), v)
    chunk_decay = jnp.exp(cum[:, :, -1])

    def step(s, inp):
        st, dec = inp
        return s * dec[..., None] + st, s

    s_fin, s_prev = lax.scan(step, s0, (jnp.moveaxis(states, 1, 0), jnp.moveaxis(chunk_decay, 1, 0)))
    s_prev = jnp.moveaxis(s_prev, 0, 1)
    o_inter = jnp.einsum('bclhk,bchkv->bclhv', q * jnp.exp(cum), s_prev)
    return (o_intra + o_inter).reshape(b, n, nh, dv), s_fin


_U_SPLITS = (D_SSD, SSD_CONV_CH, GLA_QK, GLA_QK, D_GLA, D_GLA, P_RWKV, 2 * SSD_HEADS, 2 * GLA_LR)


def _split_cols(u, sizes):
    out, start = [], 0
    for s in sizes:
        out.append(u[..., start:start + s])
        start += s
    return out


def _token_mixers(u, s0, rows, cols, col_major, p, l):
    b, n, _ = u.shape
    s_ssd, s_gla, s_rwkv = s0
    flip = lambda t: jnp.flip(t, axis=1)
    z, xbc, q, k, v, g, rw, dt_raw, glr = _split_cols(u, _U_SPLITS)
    xbc = jax.nn.silu(_dwconv2d(xbc.reshape(b, rows, cols, SSD_CONV_CH), p['ssd_conv_w'][l],
                                p['ssd_conv_b'][l])).reshape(b, n, SSD_CONV_CH)
    parts = [z, xbc, dt_raw, q, k, v, g, glr, rw]
    if col_major:
        parts = [_to_col_major(t, rows) for t in parts]
    z, xbc, dt_raw, q, k, v, g, glr, rw = parts

    xs, bm, cm = _split_cols(xbc, (D_SSD, SSD_BC, SSD_BC))
    xs = xs.reshape(b, n, SSD_HEADS, SSD_HEAD_DIM)
    bm = bm.reshape(b, n, SSD_GROUPS, SSD_STATE)
    cm = cm.reshape(b, n, SSD_GROUPS, SSD_STATE)
    dt = jax.nn.softplus(dt_raw.reshape(b, n, 2, SSD_HEADS) + p['ssd_dt_bias'][l])
    a = -jnp.exp(p['ssd_a_log'][l])
    y_f, sf = _ssd_chunked(xs, dt[:, :, 0], a[0], bm, cm, s_ssd[:, 0])
    y_b, sb = _ssd_chunked(flip(xs), flip(dt[:, :, 1]), a[1], flip(bm), flip(cm), s_ssd[:, 1])
    y = (y_f + flip(y_b) + xs * p['ssd_d'][l][:, None]).reshape(b, n, D_SSD) * jax.nn.silu(z)
    y = _rms_unit(y.reshape(b, n, SSD_GROUPS, D_SSD // SSD_GROUPS)).reshape(b, n, D_SSD) * p['ssd_norm_g'][l]

    qh = q.reshape(b, n, GLA_HEADS, GLA_DK) * GLA_DK ** -0.5
    kh = k.reshape(b, n, GLA_HEADS, GLA_DK)
    vh = v.reshape(b, n, GLA_HEADS, GLA_DV)
    gl = jnp.einsum('btdr,drk->btdk', glr.reshape(b, n, 2, GLA_LR), p['gla_gate_up'][l]) + p['gla_gate_b'][l]
    lg = (jax.nn.log_sigmoid(gl) / GLA_GATE_NORM).reshape(b, n, 2, GLA_HEADS, GLA_DK)
    o_f, gf = _gla_chunked(qh, kh, vh, lg[:, :, 0], s_gla[:, 0])
    o_b, gb = _gla_chunked(flip(qh), flip(kh), flip(vh), flip(lg[:, :, 1]), s_gla[:, 1])
    o_gla = (_rms_unit(o_f + flip(o_b)) * p['gla_norm_g'][l]
             * jax.nn.silu(g.reshape(b, n, GLA_HEADS, GLA_DV))).reshape(b, n, D_GLA)

    mu = p['rwkv_mu'][l]
    prev = jnp.pad(rw, ((0, 0), (1, 0), (0, 0)))[:, :-1]
    nxt = jnp.pad(rw, ((0, 0), (0, 1), (0, 0)))[:, 1:]
    rw = rw + mu[0] * (prev - rw) + mu[1] * (nxt - rw)
    r, kr, vr, wd, ad, gd = _split_cols(rw, (D_RWKV, D_RWKV, D_RWKV, 2 * RWKV_LW, 2 * RWKV_LA, RWKV_LG))
    w_log = jax.nn.log_sigmoid(p['rwkv_w0'][l] + jnp.einsum(
        'btdr,drc->btdc', jnp.tanh(wd.reshape(b, n, 2, RWKV_LW)), p['rwkv_w2'][l])) - 0.5
    lw = -jnp.exp(w_log)
    a_icl = jax.nn.sigmoid(p['rwkv_a0'][l] + jnp.einsum(
        'btdr,drc->btdc', ad.reshape(b, n, 2, RWKV_LA), p['rwkv_a2'][l]))
    gate = jax.nn.sigmoid(gd) @ p['rwkv_g2'][l]
    kk = (kr * p['rwkv_k_k'][l]).reshape(b, n, RWKV_HEADS, RWKV_N)
    kk = (kk * lax.rsqrt(jnp.sum(kk * kk, axis=-1, keepdims=True) + 1e-12)).reshape(b, n, D_RWKV)
    k_eff = kr[:, :, None] * (1.0 + (a_icl - 1.0) * p['rwkv_k_a'][l])
    dirs = lambda t: jnp.swapaxes(t, 1, 2)
    o2, s_rw = _rwkv_scan(r, vr, kk, dirs(lw), dirs(k_eff), dirs(a_icl), s_rwkv)
    o = (o2[:, 0] + o2[:, 1]).reshape(b, n, RWKV_HEADS, RWKV_N)
    mean = jnp.mean(o, axis=-1, keepdims=True)
    var = jnp.mean((o - mean) ** 2, axis=-1, keepdims=True)
    o = ((o - mean) * lax.rsqrt(var + RWKV_LN_EPS)).reshape(b, n, D_RWKV) * p['rwkv_ln_g'][l] + p['rwkv_ln_b'][l]
    rh = r.reshape(b, n, RWKV_HEADS, RWKV_N)
    bonus = jnp.sum(rh[:, :, None] * k_eff.reshape(b, n, 2, RWKV_HEADS, RWKV_N), axis=2) * p['rwkv_r_k'][l]
    bonus = jnp.sum(bonus, axis=-1, keepdims=True) * vr.reshape(b, n, RWKV_HEADS, RWKV_N)
    o_rwkv = (o + bonus.reshape(b, n, D_RWKV)) * gate

    out = jnp.concatenate([y, o_gla, o_rwkv], axis=-1)
    if col_major:
        out = _from_col_major(out, rows)
    return out.astype(BF16), (jnp.stack([sf, sb], axis=1), jnp.stack([gf, gb], axis=1), s_rw)


def _regroup_w_in(w):
    z, xbc, dt, q, k, v, g, glr, rw = _split_cols(w, IN_SPLITS)
    pad = jnp.zeros((w.shape[0], P_IN_PAD - sum(IN_SPLITS)), w.dtype)
    return jnp.concatenate([z, xbc, q, k, v, g, rw, dt, glr, pad], axis=1).astype(BF16)


def kernel(x_prompt, x_sample, state_ssd, state_gla, state_rwkv, c, c_ctx, ada_w, ada_b, norm_mix_g, norm_ffn_g,
           w_in, ssd_conv_w, ssd_conv_b, ssd_dt_bias, ssd_a_log, ssd_d, ssd_norm_g, gla_gate_up, gla_gate_b,
           gla_norm_g, rwkv_mu, rwkv_w0, rwkv_w2, rwkv_a0, rwkv_a2, rwkv_g2, rwkv_k_k, rwkv_k_a, rwkv_r_k,
           rwkv_ln_g, rwkv_ln_b, w_out, router_w, router_b, moe_w_gate, moe_w_up, moe_w_down, final_norm_g):
    p = dict(ssd_conv_w=ssd_conv_w, ssd_conv_b=ssd_conv_b, ssd_dt_bias=ssd_dt_bias, ssd_a_log=ssd_a_log,
             ssd_d=ssd_d, ssd_norm_g=ssd_norm_g, gla_gate_up=gla_gate_up, gla_gate_b=gla_gate_b,
             gla_norm_g=gla_norm_g, rwkv_mu=rwkv_mu, rwkv_w0=rwkv_w0, rwkv_w2=rwkv_w2, rwkv_a0=rwkv_a0,
             rwkv_a2=rwkv_a2, rwkv_g2=rwkv_g2, rwkv_k_k=rwkv_k_k, rwkv_k_a=rwkv_k_a, rwkv_r_k=rwkv_r_k,
             rwkv_ln_g=rwkv_ln_g, rwkv_ln_b=rwkv_ln_b)
    D = D_MODEL
    nb, seq = x_prompt.shape[:2]
    db, dseq = x_sample.shape[:2]
    grp = nb * seq
    assert dseq == grp
    G = 1 + db
    x = jnp.concatenate([x_prompt.reshape(1, grp, D), x_sample], axis=0)
    cvec = jnp.concatenate([c_ctx[None, :], c, jnp.zeros((8 - G, D), F32)], axis=0)
    mods = _ada(cvec, ada_w, ada_b)[:, :G]
    router_w_pad = jnp.pad(router_w, ((0, 0), (0, LANE - N_EXPERTS)))
    zero_states = (jnp.zeros((nb, 2, SSD_HEADS, SSD_HEAD_DIM, SSD_STATE), F32),
                   jnp.zeros((nb, 2, GLA_HEADS, GLA_DK, GLA_DV), F32),
                   jnp.zeros((nb, 2, RWKV_HEADS, RWKV_N, RWKV_N), F32))
    ctx_states = []
    for l in range(DEPTH):
        sh1, sc1, g1, sh2, sc2, g2 = jnp.split(mods[l], 6, axis=-1)
        u = _inproj(x, norm_mix_g[l], sh1, sc1, _regroup_w_in(w_in[l]))
        mix_p, new = _token_mixers(u[0].reshape(nb, seq, P_IN_PAD), zero_states, 1, seq, False, p, l)
        s0 = (state_ssd[:, l], state_gla[:, l], state_rwkv[:, l])
        mix_s, _ = _token_mixers(u[1:], s0, dseq // GRID_W, GRID_W, l % 2 == 1, p, l)
        ctx_states.append(new)
        mix = jnp.concatenate([mix_p.reshape(1, grp, D), mix_s], axis=0)
        x = _outproj(mix, w_out[l].astype(BF16), x, g1)
        h, logits = _ffn_in(x, norm_ffn_g[l], sh2, sc2, router_w_pad)
        moe = _moe(h.reshape(G * grp, D), logits.reshape(G * grp, LANE)[:, :N_EXPERTS], router_b,
                   moe_w_gate[l].astype(BF16), moe_w_up[l].astype(BF16), moe_w_down[l].astype(BF16))
        x = x + g2[:, None, :] * moe.reshape(G, grp, D)
    y = _final_norm(x.reshape(G * grp, D), final_norm_g).reshape(G, grp, D)
    y_prompt = y[0].reshape(nb, seq, D)
    y_sample = y[1:]
    new_ssd = jnp.stack([s[0] for s in ctx_states], axis=1)
    new_gla = jnp.stack([s[1] for s in ctx_states], axis=1)
    new_rwkv = jnp.stack([s[2] for s in ctx_states], axis=1)
    return (y_prompt, y_sample, new_ssd, new_gla, new_rwkv)
```

```python
import functools

import jax
import jax.numpy as jnp
from jax import lax
from jax.experimental import pallas as pl
from jax.experimental.pallas import tpu as pltpu

F32 = jnp.float32
BF16 = jnp.bfloat16

D_MODEL = 2048
DEPTH = 2
GRID_W = 64
SSD_HEAD_DIM = 64
D_SSD = 1024
SSD_HEADS = 16
SSD_GROUPS = 2
SSD_STATE = 128
SSD_BC = SSD_GROUPS * SSD_STATE
SSD_CONV_CH = D_SSD + 2 * SSD_BC
SSD_CHUNK = 64
GLA_HEADS = 4
D_GLA = 512
GLA_DV = 128
GLA_DK = 64
GLA_QK = GLA_HEADS * GLA_DK
GLA_LR = 16
GLA_GATE_NORM = 16.0
GLA_CHUNK = 16
RWKV_N = 64
D_RWKV = 512
RWKV_HEADS = 8
RWKV_LW = 64
RWKV_LA = 64
RWKV_LG = 128
P_RWKV = 3 * D_RWKV + 2 * RWKV_LW + 2 * RWKV_LA + RWKV_LG
IN_SPLITS = (D_SSD, SSD_CONV_CH, 2 * SSD_HEADS, GLA_QK, GLA_QK, D_GLA, D_GLA, 2 * GLA_LR, P_RWKV)
N_EXPERTS = 16
MOE_GROUPS = 4
EXPERTS_PER_GROUP = 4
MOE_TOP_K = 2
D_EXPERT = 1024
RMS_EPS = 1e-6
RWKV_LN_EPS = 64e-5

RWKV_CHUNK = 64
LANE = 128
P_IN_PAD = 6144
MOE_TM = 256
VMEM_LIMIT = 56 * 1024 * 1024


def _dot(a, b, contract, passes):
    dn = (contract, ((), ()))
    if passes == 6:
        return lax.dot_general(a, b, dn, precision=lax.Precision.HIGHEST, preferred_element_type=F32)
    ah = a.astype(BF16)
    bh = b.astype(BF16)
    out = lax.dot_general(ah, bh, dn, preferred_element_type=F32)
    if passes == 3:
        al = (a - ah.astype(F32)).astype(BF16)
        bl = (b - bh.astype(F32)).astype(BF16)
        out = out + lax.dot_general(ah, bl, dn, preferred_element_type=F32)
        out = out + lax.dot_general(al, bh, dn, preferred_element_type=F32)
    return out


_NN = ((1,), (0,))
_NT = ((1,), (1,))
_TN = ((0,), (0,))


def _params(*sem):
    return pltpu.CompilerParams(dimension_semantics=sem, vmem_limit_bytes=VMEM_LIMIT)


def _sel_dot(sel, x, contract=_NN):
    dn = (contract, ((), ()))
    s = sel.astype(BF16)
    hi = x.astype(BF16)
    r1 = x - hi.astype(F32)
    mid = r1.astype(BF16)
    lo = (r1 - mid.astype(F32)).astype(BF16)
    out = lax.dot_general(s, hi, dn, preferred_element_type=F32)
    out = out + lax.dot_general(s, mid, dn, preferred_element_type=F32)
    return out + lax.dot_general(s, lo, dn, preferred_element_type=F32)


def _bdot(a, b, contract=_NN):
    return lax.dot_general(a.astype(BF16), b.astype(BF16), (contract, ((), ())), preferred_element_type=F32)


def _rwkv_kernel(rf_ref, vf_ref, kkf_ref, lwf_ref, kf_ref, af_ref, rb_ref, vb_ref, kkb_ref, lwb_ref, kb_ref, ab_ref,
                 s0_ref, of_ref, ob_ref, sfin_ref, s_scr):
    c = pl.program_id(1)
    C = RWKV_CHUNK
    N = RWKV_N

    @pl.when(c == 0)
    def _():
        s_scr[...] = s0_ref[0]

    row = lax.broadcasted_iota(jnp.int32, (C, C), 0)
    col = lax.broadcasted_iota(jnp.int32, (C, C), 1)
    eye = (col == row).astype(F32)
    dir_refs = ((rf_ref, vf_ref, kkf_ref, lwf_ref, kf_ref, af_ref, of_ref),
                (rb_ref, vb_ref, kkb_ref, lwb_ref, kb_ref, ab_ref, ob_ref))

    chains = []
    for d, (r_ref, v_ref, kk_ref, lw_ref, k_ref, a_ref, o_ref) in enumerate(dir_refs):
        incl = (col <= row) if d == 0 else (col >= row)
        strict = (col < row) if d == 0 else (col > row)
        r = r_ref[0]
        v = v_ref[0]
        kk = kk_ref[0]
        lw = lw_ref[0, 0]
        k = k_ref[0, 0]
        a = a_ref[0, 0]
        cum = _sel_dot(incl, lw)
        tot = jnp.sum(lw, axis=0, keepdims=True)
        b = a * kk
        kp = kk * jnp.exp(cum - lw)
        rp = r * jnp.exp(cum)
        pinv = jnp.exp(-cum)
        kinv = k * pinv
        binv = b * pinv
        pend = jnp.exp(tot - cum)
        kd = k * pend
        bd = b * pend
        ptot = jnp.exp(tot)
        for h in range(RWKV_HEADS):
            sl = slice(h * N, (h + 1) * N)
            chains.append(dict(
                d=d, h=h, sl=sl, incl=incl, strict=strict, o_ref=o_ref, v=v[:, sl], kd=kd[:, sl], bd=bd[:, sl],
                ptot=ptot[:, sl],
                x=jnp.concatenate([kp[:, sl], rp[:, sl]], axis=0).astype(BF16),
                y=jnp.concatenate([kinv[:, sl], binv[:, sl]], axis=0).astype(BF16)))

    for ch in chains:
        aa = _bdot(ch['x'], ch['y'], _NT)
        ch['a_kk'] = jnp.where(ch['strict'], aa[:C, :C], 0.0)
        ch['pw'] = jnp.where(ch['strict'], aa[:C, C:], 0.0)
        ch['a_rk'] = jnp.where(ch['incl'], aa[C:, :C], 0.0)
        ch['a_rb'] = jnp.where(ch['incl'], aa[C:, C:], 0.0)
        ch['t'] = eye - ch['pw']
    n = 1
    while 2 * n < C + 1:
        for ch in chains:
            ch['pw'] = _bdot(ch['pw'], ch['pw'])
        for ch in chains:
            ch['t'] = ch['t'] + _bdot(ch['t'], ch['pw'])
        n *= 2
    for ch in chains:
        ch['s0'] = s_scr[ch['d'], ch['h']]
        ch['xs'] = _bdot(ch['x'], ch['s0'], _NT)
        ch['g'] = ch['xs'][:C] + _bdot(ch['a_kk'], ch['v'])
    for ch in chains:
        ch['u'] = _bdot(ch['t'], ch['g'])
    for ch in chains:
        o = ch['xs'][C:] + _bdot(ch['a_rk'], ch['v']) - _bdot(ch['a_rb'], ch['u'])
        ch['o_ref'][0, :, ch['sl']] = o
        s_scr[ch['d'], ch['h']] = (ch['s0'] * ch['ptot'] + _bdot(ch['v'], ch['kd'], _TN)
                                   - _bdot(ch['u'], ch['bd'], _TN))

    @pl.when(c == pl.num_programs(1) - 1)
    def _():
        sfin_ref[0] = s_scr[...]


def _rwkv_scan(r, v, kk, lw, keff, a, s0, *, interpret=False):
    B, T, HN = r.shape
    C = RWKV_CHUNK
    nc = T // C
    tok = [pl.BlockSpec((1, C, HN), lambda b, c: (b, c, 0)),
           pl.BlockSpec((1, C, HN), lambda b, c: (b, nc - 1 - c, 0))]
    tokd = [pl.BlockSpec((1, 1, C, HN), lambda b, c: (b, 0, c, 0)),
            pl.BlockSpec((1, 1, C, HN), lambda b, c: (b, 1, nc - 1 - c, 0))]
    st = pl.BlockSpec((1, 2, RWKV_HEADS, RWKV_N, RWKV_N), lambda b, c: (b, 0, 0, 0, 0))
    in_specs, args = [], []
    for d in range(2):
        in_specs += [tok[d], tok[d], tok[d], tokd[d], tokd[d], tokd[d]]
        args += [r, v, kk, lw, keff, a]
    return pl.pallas_call(
        _rwkv_kernel,
        grid=(B, nc),
        in_specs=in_specs + [st],
        out_specs=[tok[0], tok[1], st],
        out_shape=[jax.ShapeDtypeStruct((B, T, HN), F32),
                   jax.ShapeDtypeStruct((B, T, HN), F32),
                   jax.ShapeDtypeStruct((B, 2, RWKV_HEADS, RWKV_N, RWKV_N), F32)],
        scratch_shapes=[pltpu.VMEM((2, RWKV_HEADS, RWKV_N, RWKV_N), F32)],
        compiler_params=_params("arbitrary", "arbitrary"),
        interpret=interpret,
    )(*args, s0)


def _ada_kernel(c_ref, w_ref, b_ref, o_ref):
    cv = c_ref[...]
    cv = cv * jax.nn.sigmoid(cv)
    o_ref[0] = _dot(cv, w_ref[0], _NN, 1) + b_ref[0]


def _ada(cvec8, ada_w, ada_b):
    tn = 1536
    n = ada_w.shape[-1]
    return pl.pallas_call(
        _ada_kernel,
        grid=(DEPTH, n // tn),
        in_specs=[pl.BlockSpec((8, D_MODEL), lambda l, j: (0, 0)),
                  pl.BlockSpec((1, D_MODEL, tn), lambda l, j: (l, 0, j)),
                  pl.BlockSpec((1, 1, tn), lambda l, j: (l, 0, j))],
        out_specs=pl.BlockSpec((1, 8, tn), lambda l, j: (l, 0, j)),
        out_shape=jax.ShapeDtypeStruct((DEPTH, 8, n), F32),
        compiler_params=_params("arbitrary", "arbitrary"),
    )(cvec8, ada_w, ada_b.reshape(DEPTH, 1, n))


def _norm_mod(x, g, shift, scale):
    y = x * lax.rsqrt(jnp.mean(x * x, axis=-1, keepdims=True) + RMS_EPS)
    return (y * g) * (1.0 + scale) + shift


def _inproj_kernel(x_ref, g_ref, sh_ref, sc_ref, w_ref, o_ref, h_scr):
    @pl.when(pl.program_id(2) == 0)
    def _():
        h_scr[...] = _norm_mod(x_ref[0], g_ref[...], sh_ref[0], sc_ref[0]).astype(BF16)

    o_ref[0] = jnp.dot(h_scr[...], w_ref[...], preferred_element_type=F32)


def _inproj(x, g, shift, scale, w):
    G, R, D = x.shape
    N = w.shape[1]
    tm, tn = 1024, 1024
    vec = pl.BlockSpec((1, 1, D), lambda gi, i, j: (gi, 0, 0))
    return pl.pallas_call(
        _inproj_kernel,
        grid=(G, R // tm, N // tn),
        in_specs=[pl.BlockSpec((1, tm, D), lambda gi, i, j: (gi, i, 0)),
                  pl.BlockSpec((1, D), lambda gi, i, j: (0, 0)),
                  vec, vec,
                  pl.BlockSpec((D, tn), lambda gi, i, j: (0, j))],
        out_specs=pl.BlockSpec((1, tm, tn), lambda gi, i, j: (gi, i, j)),
        out_shape=jax.ShapeDtypeStruct((G, R, N), F32),
        scratch_shapes=[pltpu.VMEM((tm, D), BF16)],
        compiler_params=_params("arbitrary", "arbitrary", "arbitrary"),
    )(x, g.reshape(1, D), shift.reshape(G, 1, D), scale.reshape(G, 1, D), w)


def _outproj_kernel(a_ref, w_ref, x_ref, gate_ref, o_ref):
    o_ref[0] = x_ref[0] + gate_ref[0] * jnp.dot(a_ref[0], w_ref[...], preferred_element_type=F32)


def _outproj(a, w, xres, gate):
    G, R, Kd = a.shape
    N = w.shape[1]
    tm, tn = 1024, 1024
    return pl.pallas_call(
        _outproj_kernel,
        grid=(G, R // tm, N // tn),
        in_specs=[pl.BlockSpec((1, tm, Kd), lambda gi, i, j: (gi, i, 0)),
                  pl.BlockSpec((Kd, tn), lambda gi, i, j: (0, j)),
                  pl.BlockSpec((1, tm, tn), lambda gi, i, j: (gi, i, j)),
                  pl.BlockSpec((1, 1, tn), lambda gi, i, j: (gi, 0, j))],
        out_specs=pl.BlockSpec((1, tm, tn), lambda gi, i, j: (gi, i, j)),
        out_shape=jax.ShapeDtypeStruct((G, R, N), F32),
        compiler_params=_params("arbitrary", "arbitrary", "arbitrary"),
    )(a, w, xres, gate.reshape(G, 1, N))


def _ffn_in_kernel(x_ref, g_ref, sh_ref, sc_ref, rw_ref, h_ref, lg_ref):
    h = _norm_mod(x_ref[0], g_ref[...], sh_ref[0], sc_ref[0])
    h_ref[0] = h.astype(BF16)
    lg_ref[0] = _dot(h, rw_ref[...], _NN, 6)


def _ffn_in(x, g, shift, scale, router_w_pad):
    G, R, D = x.shape
    tm = 512
    vec = pl.BlockSpec((1, 1, D), lambda gi, i: (gi, 0, 0))
    return pl.pallas_call(
        _ffn_in_kernel,
        grid=(G, R // tm),
        in_specs=[pl.BlockSpec((1, tm, D), lambda gi, i: (gi, i, 0)),
                  pl.BlockSpec((1, D), lambda gi, i: (0, 0)),
                  vec, vec,
                  pl.BlockSpec((D, LANE), lambda gi, i: (0, 0))],
        out_specs=[pl.BlockSpec((1, tm, D), lambda gi, i: (gi, i, 0)),
                   pl.BlockSpec((1, tm, LANE), lambda gi, i: (gi, i, 0))],
        out_shape=[jax.ShapeDtypeStruct((G, R, D), BF16), jax.ShapeDtypeStruct((G, R, LANE), F32)],
        compiler_params=_params("arbitrary", "arbitrary"),
    )(x, g.reshape(1, D), shift.reshape(G, 1, D), scale.reshape(G, 1, D), router_w_pad)


def _moe_kernel(te_ref, na_ref, x_ref, gw_ref, wg_ref, wu_ref, wd_ref, o_ref):
    i = pl.program_id(0)

    @pl.when(i < na_ref[0])
    def _():
        x = x_ref[...]
        hg = jnp.dot(x, wg_ref[0], preferred_element_type=F32)
        hu = jnp.dot(x, wu_ref[0], preferred_element_type=F32)
        he = (hg * jax.nn.sigmoid(hg) * hu).astype(BF16)
        o_ref[...] = gw_ref[...] * jnp.dot(he, wd_ref[0], preferred_element_type=F32)

    @pl.when(i >= na_ref[0])
    def _():
        o_ref[...] = jnp.zeros_like(o_ref)


def _moe_experts(tile_expert, n_active, xs, gate_w, w_gate, w_up, w_down):
    P, D = xs.shape
    tm = MOE_TM
    grid_spec = pltpu.PrefetchScalarGridSpec(
        num_scalar_prefetch=2,
        grid=(P // tm,),
        in_specs=[pl.BlockSpec((tm, D), lambda i, te, na: (i, 0)),
                  pl.BlockSpec((tm, 1), lambda i, te, na: (i, 0)),
                  pl.BlockSpec((1, D, D_EXPERT), lambda i, te, na: (te[i], 0, 0)),
                  pl.BlockSpec((1, D, D_EXPERT), lambda i, te, na: (te[i], 0, 0)),
                  pl.BlockSpec((1, D_EXPERT, D), lambda i, te, na: (te[i], 0, 0))],
        out_specs=pl.BlockSpec((tm, D), lambda i, te, na: (i, 0)),
    )
    return pl.pallas_call(
        _moe_kernel,
        grid_spec=grid_spec,
        out_shape=jax.ShapeDtypeStruct((P, D), F32),
        compiler_params=_params("arbitrary"),
    )(tile_expert, n_active, xs, gate_w, w_gate, w_up, w_down)


def _route(logits, router_b):
    aff = jax.nn.sigmoid(logits)
    sel = (aff + router_b.astype(F32)).reshape(-1, MOE_GROUPS, EXPERTS_PER_GROUP)
    group_score = jnp.sum(lax.top_k(sel, 2)[0], axis=-1)
    grp = jnp.argmax(group_score, axis=-1)
    in_grp = jnp.take_along_axis(sel, grp[:, None, None], axis=1)[:, 0]
    _, loc = lax.top_k(in_grp, MOE_TOP_K)
    idx = grp[:, None] * EXPERTS_PER_GROUP + loc
    wts = jnp.take_along_axis(aff, idx, axis=1)
    wts = wts / jnp.sum(wts, axis=-1, keepdims=True)
    return idx.astype(jnp.int32), wts


def _moe(h, logits, router_b, w_gate, w_up, w_down):
    M, D = h.shape
    tm = MOE_TM
    idx, wts = _route(logits, router_b)
    n_asg = M * MOE_TOP_K
    P = n_asg + N_EXPERTS * tm
    onehot = jax.nn.one_hot(idx.reshape(-1), N_EXPERTS, dtype=jnp.int32)
    csum = jnp.cumsum(onehot, axis=0)
    counts = csum[-1]
    pcounts = ((counts + tm - 1) // tm) * tm
    pends = jnp.cumsum(pcounts)
    pstarts = pends - pcounts
    dest = jnp.sum(onehot * (pstarts[None, :] + csum - 1), axis=1)
    rows_tok = jnp.zeros((P,), jnp.int32).at[dest].set(jnp.arange(n_asg, dtype=jnp.int32) // MOE_TOP_K)
    gate_w = jnp.zeros((P,), F32).at[dest].set(wts.reshape(-1))
    pos = dest.reshape(M, MOE_TOP_K)
    tile_start = jnp.arange(P // tm, dtype=jnp.int32) * tm
    tile_expert = jnp.minimum(jnp.searchsorted(pends, tile_start, side='right'), N_EXPERTS - 1).astype(jnp.int32)
    n_active = (pends[-1] // tm).astype(jnp.int32).reshape(1)
    xs = jnp.take(h, rows_tok, axis=0)
    y = _moe_experts(tile_expert, n_active, xs, gate_w.reshape(P, 1), w_gate, w_up, w_down)
    return jnp.take(y, pos[:, 0], axis=0) + jnp.take(y, pos[:, 1], axis=0)


def _final_norm_kernel(x_ref, g_ref, o_ref):
    x = x_ref[...]
    o_ref[...] = x * lax.rsqrt(jnp.mean(x * x, axis=-1, keepdims=True) + RMS_EPS) * g_ref[...]


def _final_norm(x, g):
    M, D = x.shape
    tm = 512
    return pl.pallas_call(
        _final_norm_kernel,
        grid=(M // tm,),
        in_specs=[pl.BlockSpec((tm, D), lambda i: (i, 0)), pl.BlockSpec((1, D), lambda i: (0, 0))],
        out_specs=pl.BlockSpec((tm, D), lambda i: (i, 0)),
        out_shape=jax.ShapeDtypeStruct((M, D), F32),
        compiler_params=_params("arbitrary"),
    )(x, g.reshape(1, D))


def _rms_unit(x):
    return x * lax.rsqrt(jnp.mean(x * x, axis=-1, keepdims=True) + RMS_EPS)


def _to_col_major(t, rows):
    b, n = t.shape[:2]
    return t.reshape(b, rows, GRID_W, -1).swapaxes(1, 2).reshape(b, n, -1)


def _from_col_major(t, rows):
    b, n = t.shape[:2]
    return t.reshape(b, GRID_W, rows, -1).swapaxes(1, 2).reshape(b, n, -1)


def _dwconv2d(x, w, bias):
    ch = x.shape[-1]
    y = lax.conv_general_dilated(x, w.astype(x.dtype)[:, :, None, :], window_strides=(1, 1), padding='SAME',
                                 dimension_numbers=('NHWC', 'HWIO', 'NHWC'), feature_group_count=ch)
    return y + bias.astype(x.dtype)


def _ssd_chunked(x, dt, a, bm, cm, s0):
    b, n, nh, hp = x.shape
    ng, ns = bm.shape[2], bm.shape[3]
    L = SSD_CHUNK
    nc = n // L
    hpg = nh // ng
    xc = (x * dt[..., None]).reshape(b, nc, L, ng, hpg, hp)
    bc = bm.reshape(b, nc, L, ng, ns)
    cc = cm.reshape(b, nc, L, ng, ns)
    cum = jnp.cumsum((dt * a).reshape(b, nc, L, nh), axis=2)
    mask = jnp.tril(jnp.ones((L, L), bool))[None, None, :, :, None]
    seg = cum[:, :, :, None, :] - cum[:, :, None, :, :]
    decay = jnp.exp(jnp.where(mask, seg, -jnp.inf)).reshape(b, nc, L, L, ng, hpg)
    scores = jnp.einsum('bclgn,bcsgn->bclsg', cc, bc)
    y_diag = jnp.einsum('bclsgh,bcsghp->bclghp', scores[..., None] * decay, xc)
    to_end = jnp.exp(cum[:, :, -1:, :] - cum).reshape(b, nc, L, ng, hpg)
    states = jnp.einsum('bclgn,bclghp->bcghpn', bc, xc * to_end[..., None]).reshape(b, nc, nh, hp, ns)
    chunk_decay = jnp.exp(cum[:, :, -1, :])

    def step(s, inp):
        st, dec = inp
        return s * dec[:, :, None, None] + st, s

    s_fin, s_prev = lax.scan(step, s0, (jnp.moveaxis(states, 1, 0), jnp.moveaxis(chunk_decay, 1, 0)))
    s_prev = jnp.moveaxis(s_prev, 0, 1).reshape(b, nc, ng, hpg, hp, ns)
    y_off = jnp.einsum('bclgn,bcghpn->bclghp', cc, s_prev) * jnp.exp(cum).reshape(b, nc, L, ng, hpg)[..., None]
    return (y_diag + y_off).reshape(b, n, nh, hp), s_fin


def _gla_chunked(q, k, v, lg, s0):
    b, n, nh, dk = q.shape
    dv = v.shape[-1]
    L = GLA_CHUNK
    nc = n // L
    q = q.reshape(b, nc, L, nh, dk)
    k = k.reshape(b, nc, L, nh, dk)
    v = v.reshape(b, nc, L, nh, dv)
    cum = jnp.cumsum(lg.reshape(b, nc, L, nh, dk), axis=2)
    mask = jnp.tril(jnp.ones((L, L), bool))[None, None, :, :, None, None]
    seg = jnp.exp(jnp.where(mask, cum[:, :, :, None] - cum[:, :, None], -jnp.inf))
    attn = jnp.einsum('bclhk,bcshk,bclshk->bclsh', q, k, seg)
    o_intra = jnp.einsum('bclsh,bcshv->bclhv', attn, v)
    states = jnp.einsum('bcshk,bcshv->bchkv', k * jnp.exp(cum[:, :, -1:] - cum), v)
    chunk_decay = jnp.exp(cum[:, :, -1])

    def step(s, inp):
        st, dec = inp
        return s * dec[..., None] + st, s

    s_fin, s_prev = lax.scan(step, s0, (jnp.moveaxis(states, 1, 0), jnp.moveaxis(chunk_decay, 1, 0)))
    s_prev = jnp.moveaxis(s_prev, 0, 1)
    o_inter = jnp.einsum('bclhk,bchkv->bclhv', q * jnp.exp(cum), s_prev)
    return (o_intra + o_inter).reshape(b, n, nh, dv), s_fin


_U_SPLITS = (D_SSD, SSD_CONV_CH, GLA_QK, GLA_QK, D_GLA, D_GLA, P_RWKV, 2 * SSD_HEADS, 2 * GLA_LR)


def _split_cols(u, sizes):
    out, start = [], 0
    for s in sizes:
        out.append(u[..., start:start + s])
        start += s
    return out


def _token_mixers(u, s0, rows, cols, col_major, p, l):
    b, n, _ = u.shape
    s_ssd, s_gla, s_rwkv = s0
    flip = lambda t: jnp.flip(t, axis=1)
    z, xbc, q, k, v, g, rw, dt_raw, glr = _split_cols(u, _U_SPLITS)
    xbc = jax.nn.silu(_dwconv2d(xbc.reshape(b, rows, cols, SSD_CONV_CH), p['ssd_conv_w'][l],
                                p['ssd_conv_b'][l])).reshape(b, n, SSD_CONV_CH)
    parts = [z, xbc, dt_raw, q, k, v, g, glr, rw]
    if col_major:
        parts = [_to_col_major(t, rows) for t in parts]
    z, xbc, dt_raw, q, k, v, g, glr, rw = parts

    xs, bm, cm = _split_cols(xbc, (D_SSD, SSD_BC, SSD_BC))
    xs = xs.reshape(b, n, SSD_HEADS, SSD_HEAD_DIM)
    bm = bm.reshape(b, n, SSD_GROUPS, SSD_STATE)
    cm = cm.reshape(b, n, SSD_GROUPS, SSD_STATE)
    dt = jax.nn.softplus(dt_raw.reshape(b, n, 2, SSD_HEADS) + p['ssd_dt_bias'][l])
    a = -jnp.exp(p['ssd_a_log'][l])
    y_f, sf = _ssd_chunked(xs, dt[:, :, 0], a[0], bm, cm, s_ssd[:, 0])
    y_b, sb = _ssd_chunked(flip(xs), flip(dt[:, :, 1]), a[1], flip(bm), flip(cm), s_ssd[:, 1])
    y = (y_f + flip(y_b) + xs * p['ssd_d'][l][:, None]).reshape(b, n, D_SSD) * jax.nn.silu(z)
    y = _rms_unit(y.reshape(b, n, SSD_GROUPS, D_SSD // SSD_GROUPS)).reshape(b, n, D_SSD) * p['ssd_norm_g'][l]

    qh = q.reshape(b, n, GLA_HEADS, GLA_DK) * GLA_DK ** -0.5
    kh = k.reshape(b, n, GLA_HEADS, GLA_DK)
    vh = v.reshape(b, n, GLA_HEADS, GLA_DV)
    gl = jnp.einsum('btdr,drk->btdk', glr.reshape(b, n, 2, GLA_LR), p['gla_gate_up'][l]) + p['gla_gate_b'][l]
    lg = (jax.nn.log_sigmoid(gl) / GLA_GATE_NORM).reshape(b, n, 2, GLA_HEADS, GLA_DK)
    o_f, gf = _gla_chunked(qh, kh, vh, lg[:, :, 0], s_gla[:, 0])
    o_b, gb = _gla_chunked(flip(qh), flip(kh), flip(vh), flip(lg[:, :, 1]), s_gla[:, 1])
    o_gla = (_rms_unit(o_f + flip(o_b)) * p['gla_norm_g'][l]
             * jax.nn.silu(g.reshape(b, n, GLA_HEADS, GLA_DV))).reshape(b, n, D_GLA)

    mu = p['rwkv_mu'][l]
    prev = jnp.pad(rw, ((0, 0), (1, 0), (0, 0)))[:, :-1]
    nxt = jnp.pad(rw, ((0, 0), (0, 1), (0, 0)))[:, 1:]
    rw = rw + mu[0] * (prev - rw) + mu[1] * (nxt - rw)
    r, kr, vr, wd, ad, gd = _split_cols(rw, (D_RWKV, D_RWKV, D_RWKV, 2 * RWKV_LW, 2 * RWKV_LA, RWKV_LG))
    w_log = jax.nn.log_sigmoid(p['rwkv_w0'][l] + jnp.einsum(
        'btdr,drc->btdc', jnp.tanh(wd.reshape(b, n, 2, RWKV_LW)), p['rwkv_w2'][l])) - 0.5
    lw = -jnp.exp(w_log)
    a_icl = jax.nn.sigmoid(p['rwkv_a0'][l] + jnp.einsum(
        'btdr,drc->btdc', ad.reshape(b, n, 2, RWKV_LA), p['rwkv_a2'][l]))
    gate = jax.nn.sigmoid(gd) @ p['rwkv_g2'][l]
    kk = (kr * p['rwkv_k_k'][l]).reshape(b, n, RWKV_HEADS, RWKV_N)
    kk = (kk * lax.rsqrt(jnp.sum(kk * kk, axis=-1, keepdims=True) + 1e-12)).reshape(b, n, D_RWKV)
    k_eff = kr[:, :, None] * (1.0 + (a_icl - 1.0) * p['rwkv_k_a'][l])
    dirs = lambda t: jnp.swapaxes(t, 1, 2)
    o_f, o_b, s_rw = _rwkv_scan(r, vr, kk, dirs(lw), dirs(k_eff), dirs(a_icl), s_rwkv)
    o = (o_f + o_b).reshape(b, n, RWKV_HEADS, RWKV_N)
    mean = jnp.mean(o, axis=-1, keepdims=True)
    var = jnp.mean((o - mean) ** 2, axis=-1, keepdims=True)
    o = ((o - mean) * lax.rsqrt(var + RWKV_LN_EPS)).reshape(b, n, D_RWKV) * p['rwkv_ln_g'][l] + p['rwkv_ln_b'][l]
    rh = r.reshape(b, n, RWKV_HEADS, RWKV_N)
    bonus = jnp.sum(rh[:, :, None] * k_eff.reshape(b, n, 2, RWKV_HEADS, RWKV_N), axis=2) * p['rwkv_r_k'][l]
    bonus = jnp.sum(bonus, axis=-1, keepdims=True) * vr.reshape(b, n, RWKV_HEADS, RWKV_N)
    o_rwkv = (o + bonus.reshape(b, n, D_RWKV)) * gate

    out = jnp.concatenate([y, o_gla, o_rwkv], axis=-1)
    if col_major:
        out = _from_col_major(out, rows)
    return out.astype(BF16), (jnp.stack([sf, sb], axis=1), jnp.stack([gf, gb], axis=1), s_rw)


def _regroup_w_in(w):
    z, xbc, dt, q, k, v, g, glr, rw = _split_cols(w, IN_SPLITS)
    pad = jnp.zeros((w.shape[0], P_IN_PAD - sum(IN_SPLITS)), w.dtype)
    return jnp.concatenate([z, xbc, q, k, v, g, rw, dt, glr, pad], axis=1).astype(BF16)


def kernel(x_prompt, x_sample, state_ssd, state_gla, state_rwkv, c, c_ctx, ada_w, ada_b, norm_mix_g, norm_ffn_g,
           w_in, ssd_conv_w, ssd_conv_b, ssd_dt_bias, ssd_a_log, ssd_d, ssd_norm_g, gla_gate_up, gla_gate_b,
           gla_norm_g, rwkv_mu, rwkv_w0, rwkv_w2, rwkv_a0, rwkv_a2, rwkv_g2, rwkv_k_k, rwkv_k_a, rwkv_r_k,
           rwkv_ln_g, rwkv_ln_b, w_out, router_w, router_b, moe_w_gate, moe_w_up, moe_w_down, final_norm_g):
    p = dict(ssd_conv_w=ssd_conv_w, ssd_conv_b=ssd_conv_b, ssd_dt_bias=ssd_dt_bias, ssd_a_log=ssd_a_log,
             ssd_d=ssd_d, ssd_norm_g=ssd_norm_g, gla_gate_up=gla_gate_up, gla_gate_b=gla_gate_b,
             gla_norm_g=gla_norm_g, rwkv_mu=rwkv_mu, rwkv_w0=rwkv_w0, rwkv_w2=rwkv_w2, rwkv_a0=rwkv_a0,
             rwkv_a2=rwkv_a2, rwkv_g2=rwkv_g2, rwkv_k_k=rwkv_k_k, rwkv_k_a=rwkv_k_a, rwkv_r_k=rwkv_r_k,
             rwkv_ln_g=rwkv_ln_g, rwkv_ln_b=rwkv_ln_b)
    D = D_MODEL
    nb, seq = x_prompt.shape[:2]
    db, dseq = x_sample.shape[:2]
    grp = nb * seq
    assert dseq == grp
    G = 1 + db
    x = jnp.concatenate([x_prompt.reshape(1, grp, D), x_sample], axis=0)
    cvec = jnp.concatenate([c_ctx[None, :], c, jnp.zeros((8 - G, D), F32)], axis=0)
    mods = _ada(cvec, ada_w, ada_b)[:, :G]
    router_w_pad = jnp.pad(router_w, ((0, 0), (0, LANE - N_EXPERTS)))
    zero_states = (jnp.zeros((nb, 2, SSD_HEADS, SSD_HEAD_DIM, SSD_STATE), F32),
                   jnp.zeros((nb, 2, GLA_HEADS, GLA_DK, GLA_DV), F32),
                   jnp.zeros((nb, 2, RWKV_HEADS, RWKV_N, RWKV_N), F32))
    ctx_states = []
    for l in range(DEPTH):
        sh1, sc1, g1, sh2, sc2, g2 = jnp.split(mods[l], 6, axis=-1)
        u = _inproj(x, norm_mix_g[l], sh1, sc1, _regroup_w_in(w_in[l]))
        mix_p, new = _token_mixers(u[0].reshape(nb, seq, P_IN_PAD), zero_states, 1, seq, False, p, l)
        s0 = (state_ssd[:, l], state_gla[:, l], state_rwkv[:, l])
        mix_s, _ = _token_mixers(u[1:], s0, dseq // GRID_W, GRID_W, l % 2 == 1, p, l)
        ctx_states.append(new)
        mix = jnp.concatenate([mix_p.reshape(1, grp, D), mix_s], axis=0)
        x = _outproj(mix, w_out[l].astype(BF16), x, g1)
        h, logits = _ffn_in(x, norm_ffn_g[l], sh2, sc2, router_w_pad)
        moe = _moe(h.reshape(G * grp, D), logits.reshape(G * grp, LANE)[:, :N_EXPERTS], router_b,
                   moe_w_gate[l].astype(BF16), moe_w_up[l].astype(BF16), moe_w_down[l].astype(BF16))
        x = x + g2[:, None, :] * moe.reshape(G, grp, D)
    y = _final_norm(x.reshape(G * grp, D), final_norm_g).reshape(G, grp, D)
    y_prompt = y[0].reshape(nb, seq, D)
    y_sample = y[1:]
    new_ssd = jnp.stack([s[0] for s in ctx_states], axis=1)
    new_gla = jnp.stack([s[1] for s in ctx_states], axis=1)
    new_rwkv = jnp.stack([s[2] for s in ctx_states], axis=1)
    return (y_prompt, y_sample, new_ssd, new_gla, new_rwkv)
```

```python
import functools

import jax
import jax.numpy as jnp
from jax import lax
from jax.experimental import pallas as pl
from jax.experimental.pallas import tpu as pltpu

F32 = jnp.float32
BF16 = jnp.bfloat16

D_MODEL = 2048
DEPTH = 2
GRID_W = 64
SSD_HEAD_DIM = 64
D_SSD = 1024
SSD_HEADS = 16
SSD_GROUPS = 2
SSD_STATE = 128
SSD_BC = SSD_GROUPS * SSD_STATE
SSD_CONV_CH = D_SSD + 2 * SSD_BC
SSD_CHUNK = 64
GLA_HEADS = 4
D_GLA = 512
GLA_DV = 128
GLA_DK = 64
GLA_QK = GLA_HEADS * GLA_DK
GLA_LR = 16
GLA_GATE_NORM = 16.0
GLA_CHUNK = 16
RWKV_N = 64
D_RWKV = 512
RWKV_HEADS = 8
RWKV_LW = 64
RWKV_LA = 64
RWKV_LG = 128
P_RWKV = 3 * D_RWKV + 2 * RWKV_LW + 2 * RWKV_LA + RWKV_LG
IN_SPLITS = (D_SSD, SSD_CONV_CH, 2 * SSD_HEADS, GLA_QK, GLA_QK, D_GLA, D_GLA, 2 * GLA_LR, P_RWKV)
N_EXPERTS = 16
MOE_GROUPS = 4
EXPERTS_PER_GROUP = 4
MOE_TOP_K = 2
D_EXPERT = 1024
RMS_EPS = 1e-6
RWKV_LN_EPS = 64e-5

RWKV_CHUNK = 64
LANE = 128
P_IN_PAD = 6144
MOE_TM = 256
VMEM_LIMIT = 56 * 1024 * 1024


def _dot(a, b, contract, passes):
    dn = (contract, ((), ()))
    if passes == 6:
        return lax.dot_general(a, b, dn, precision=lax.Precision.HIGHEST, preferred_element_type=F32)
    ah = a.astype(BF16)
    bh = b.astype(BF16)
    out = lax.dot_general(ah, bh, dn, preferred_element_type=F32)
    if passes == 3:
        al = (a - ah.astype(F32)).astype(BF16)
        bl = (b - bh.astype(F32)).astype(BF16)
        out = out + lax.dot_general(ah, bl, dn, preferred_element_type=F32)
        out = out + lax.dot_general(al, bh, dn, preferred_element_type=F32)
    return out


_NN = ((1,), (0,))
_NT = ((1,), (1,))
_TN = ((0,), (0,))


def _params(*sem):
    return pltpu.CompilerParams(dimension_semantics=sem, vmem_limit_bytes=VMEM_LIMIT)


def _sel_dot(sel, x, contract=_NN):
    dn = (contract, ((), ()))
    s = sel.astype(BF16)
    hi = x.astype(BF16)
    r1 = x - hi.astype(F32)
    mid = r1.astype(BF16)
    lo = (r1 - mid.astype(F32)).astype(BF16)
    out = lax.dot_general(s, hi, dn, preferred_element_type=F32)
    out = out + lax.dot_general(s, mid, dn, preferred_element_type=F32)
    return out + lax.dot_general(s, lo, dn, preferred_element_type=F32)


def _bdot(a, b, contract=_NN):
    return lax.dot_general(a.astype(BF16), b.astype(BF16), (contract, ((), ())), preferred_element_type=F32)


def _rwkv_kernel(rf_ref, vf_ref, kkf_ref, lwf_ref, kf_ref, af_ref, rb_ref, vb_ref, kkb_ref, lwb_ref, kb_ref, ab_ref,
                 s0_ref, of_ref, ob_ref, sfin_ref, s_scr):
    c = pl.program_id(1)
    C = RWKV_CHUNK
    N = RWKV_N

    @pl.when(c == 0)
    def _():
        s_scr[...] = s0_ref[0]

    row = lax.broadcasted_iota(jnp.int32, (C, C), 0)
    col = lax.broadcasted_iota(jnp.int32, (C, C), 1)
    eye = (col == row).astype(F32)
    dir_refs = ((rf_ref, vf_ref, kkf_ref, lwf_ref, kf_ref, af_ref, of_ref),
                (rb_ref, vb_ref, kkb_ref, lwb_ref, kb_ref, ab_ref, ob_ref))

    chains = []
    for d, (r_ref, v_ref, kk_ref, lw_ref, k_ref, a_ref, o_ref) in enumerate(dir_refs):
        incl = (col <= row) if d == 0 else (col >= row)
        strict = (col < row) if d == 0 else (col > row)
        r = r_ref[0]
        v = v_ref[0]
        kk = kk_ref[0]
        lw = lw_ref[0, 0]
        k = k_ref[0, 0]
        a = a_ref[0, 0]
        cum = _sel_dot(incl, lw)
        tot = jnp.sum(lw, axis=0, keepdims=True)
        b = a * kk
        kp = kk * jnp.exp(cum - lw)
        rp = r * jnp.exp(cum)
        pinv = jnp.exp(-cum)
        kinv = k * pinv
        binv = b * pinv
        pend = jnp.exp(tot - cum)
        kd = k * pend
        bd = b * pend
        ptot = jnp.exp(tot)
        for h in range(RWKV_HEADS):
            sl = slice(h * N, (h + 1) * N)
            chains.append(dict(
                d=d, h=h, sl=sl, incl=incl, strict=strict, o_ref=o_ref, v=v[:, sl], kd=kd[:, sl], bd=bd[:, sl],
                ptot=ptot[:, sl],
                x=jnp.concatenate([kp[:, sl], rp[:, sl]], axis=0).astype(BF16),
                y=jnp.concatenate([kinv[:, sl], binv[:, sl]], axis=0).astype(BF16)))

    for ch in chains:
        aa = _bdot(ch['x'], ch['y'], _NT)
        ch['a_kk'] = jnp.where(ch['strict'], aa[:C, :C], 0.0)
        ch['pw'] = jnp.where(ch['strict'], aa[:C, C:], 0.0)
        ch['a_rk'] = jnp.where(ch['incl'], aa[C:, :C], 0.0)
        ch['a_rb'] = jnp.where(ch['incl'], aa[C:, C:], 0.0)
        ch['t'] = eye - ch['pw']
    n = 1
    while 2 * n < C + 1:
        for ch in chains:
            ch['pw'] = _bdot(ch['pw'], ch['pw'])
        for ch in chains:
            ch['t'] = ch['t'] + _bdot(ch['t'], ch['pw'])
        n *= 2
    for ch in chains:
        ch['s0'] = s_scr[ch['d'], ch['h']]
        ch['xs'] = _bdot(ch['x'], ch['s0'], _NT)
        ch['g'] = ch['xs'][:C] + _bdot(ch['a_kk'], ch['v'])
    for ch in chains:
        ch['u'] = _bdot(ch['t'], ch['g'])
    for ch in chains:
        o = ch['xs'][C:] + _bdot(ch['a_rk'], ch['v']) - _bdot(ch['a_rb'], ch['u'])
        ch['o_ref'][0, :, ch['sl']] = o
        s_scr[ch['d'], ch['h']] = (ch['s0'] * ch['ptot'] + _bdot(ch['v'], ch['kd'], _TN)
                                   - _bdot(ch['u'], ch['bd'], _TN))

    @pl.when(c == pl.num_programs(1) - 1)
    def _():
        sfin_ref[0] = s_scr[...]


def _rwkv_scan(r, v, kk, lw, keff, a, s0, *, interpret=False):
    B, T, HN = r.shape
    C = RWKV_CHUNK
    nc = T // C
    tok = [pl.BlockSpec((1, C, HN), lambda b, c: (b, c, 0)),
           pl.BlockSpec((1, C, HN), lambda b, c: (b, nc - 1 - c, 0))]
    tokd = [pl.BlockSpec((1, 1, C, HN), lambda b, c: (b, 0, c, 0)),
            pl.BlockSpec((1, 1, C, HN), lambda b, c: (b, 1, nc - 1 - c, 0))]
    st = pl.BlockSpec((1, 2, RWKV_HEADS, RWKV_N, RWKV_N), lambda b, c: (b, 0, 0, 0, 0))
    in_specs, args = [], []
    for d in range(2):
        in_specs += [tok[d], tok[d], tok[d], tokd[d], tokd[d], tokd[d]]
        args += [r, v, kk, lw, keff, a]
    return pl.pallas_call(
        _rwkv_kernel,
        grid=(B, nc),
        in_specs=in_specs + [st],
        out_specs=[tok[0], tok[1], st],
        out_shape=[jax.ShapeDtypeStruct((B, T, HN), F32),
                   jax.ShapeDtypeStruct((B, T, HN), F32),
                   jax.ShapeDtypeStruct((B, 2, RWKV_HEADS, RWKV_N, RWKV_N), F32)],
        scratch_shapes=[pltpu.VMEM((2, RWKV_HEADS, RWKV_N, RWKV_N), F32)],
        compiler_params=_params("arbitrary", "arbitrary"),
        interpret=interpret,
    )(*args, s0)


def _ada_kernel(c_ref, w_ref, b_ref, o_ref):
    cv = c_ref[...]
    cv = cv * jax.nn.sigmoid(cv)
    o_ref[0] = _dot(cv, w_ref[0], _NN, 1) + b_ref[0]


def _ada(cvec8, ada_w, ada_b):
    tn = 1536
    n = ada_w.shape[-1]
    return pl.pallas_call(
        _ada_kernel,
        grid=(DEPTH, n // tn),
        in_specs=[pl.BlockSpec((8, D_MODEL), lambda l, j: (0, 0)),
                  pl.BlockSpec((1, D_MODEL, tn), lambda l, j: (l, 0, j)),
                  pl.BlockSpec((1, 1, tn), lambda l, j: (l, 0, j))],
        out_specs=pl.BlockSpec((1, 8, tn), lambda l, j: (l, 0, j)),
        out_shape=jax.ShapeDtypeStruct((DEPTH, 8, n), F32),
        compiler_params=_params("arbitrary", "arbitrary"),
    )(cvec8, ada_w, ada_b.reshape(DEPTH, 1, n))


def _norm_mod(x, g, shift, scale):
    y = x * lax.rsqrt(jnp.mean(x * x, axis=-1, keepdims=True) + RMS_EPS)
    return (y * g) * (1.0 + scale) + shift


def _inproj_kernel(x_ref, g_ref, sh_ref, sc_ref, w_ref, o_ref, h_scr):
    @pl.when(pl.program_id(2) == 0)
    def _():
        h_scr[...] = _norm_mod(x_ref[0], g_ref[...], sh_ref[0], sc_ref[0]).astype(BF16)

    o_ref[0] = jnp.dot(h_scr[...], w_ref[...], preferred_element_type=F32)


def _inproj(x, g, shift, scale, w):
    G, R, D = x.shape
    N = w.shape[1]
    tm, tn = 1024, 1024
    vec = pl.BlockSpec((1, 1, D), lambda gi, i, j: (gi, 0, 0))
    return pl.pallas_call(
        _inproj_kernel,
        grid=(G, R // tm, N // tn),
        in_specs=[pl.BlockSpec((1, tm, D), lambda gi, i, j: (gi, i, 0)),
                  pl.BlockSpec((1, D), lambda gi, i, j: (0, 0)),
                  vec, vec,
                  pl.BlockSpec((D, tn), lambda gi, i, j: (0, j))],
        out_specs=pl.BlockSpec((1, tm, tn), lambda gi, i, j: (gi, i, j)),
        out_shape=jax.ShapeDtypeStruct((G, R, N), F32),
        scratch_shapes=[pltpu.VMEM((tm, D), BF16)],
        compiler_params=_params("arbitrary", "arbitrary", "arbitrary"),
    )(x, g.reshape(1, D), shift.reshape(G, 1, D), scale.reshape(G, 1, D), w)


def _outproj_kernel(a_ref, w_ref, x_ref, gate_ref, o_ref):
    o_ref[0] = x_ref[0] + gate_ref[0] * jnp.dot(a_ref[0], w_ref[...], preferred_element_type=F32)


def _outproj(a, w, xres, gate):
    G, R, Kd = a.shape
    N = w.shape[1]
    tm, tn = 1024, 1024
    return pl.pallas_call(
        _outproj_kernel,
        grid=(G, R // tm, N // tn),
        in_specs=[pl.BlockSpec((1, tm, Kd), lambda gi, i, j: (gi, i, 0)),
                  pl.BlockSpec((Kd, tn), lambda gi, i, j: (0, j)),
                  pl.BlockSpec((1, tm, tn), lambda gi, i, j: (gi, i, j)),
                  pl.BlockSpec((1, 1, tn), lambda gi, i, j: (gi, 0, j))],
        out_specs=pl.BlockSpec((1, tm, tn), lambda gi, i, j: (gi, i, j)),
        out_shape=jax.ShapeDtypeStruct((G, R, N), F32),
        compiler_params=_params("arbitrary", "arbitrary", "arbitrary"),
    )(a, w, xres, gate.reshape(G, 1, N))


def _ffn_in_kernel(x_ref, g_ref, sh_ref, sc_ref, rw_ref, h_ref, lg_ref):
    h = _norm_mod(x_ref[0], g_ref[...], sh_ref[0], sc_ref[0])
    h_ref[0] = h.astype(BF16)
    lg_ref[0] = _dot(h, rw_ref[...], _NN, 6)


def _ffn_in(x, g, shift, scale, router_w_pad):
    G, R, D = x.shape
    tm = 512
    vec = pl.BlockSpec((1, 1, D), lambda gi, i: (gi, 0, 0))
    return pl.pallas_call(
        _ffn_in_kernel,
        grid=(G, R // tm),
        in_specs=[pl.BlockSpec((1, tm, D), lambda gi, i: (gi, i, 0)),
                  pl.BlockSpec((1, D), lambda gi, i: (0, 0)),
                  vec, vec,
                  pl.BlockSpec((D, LANE), lambda gi, i: (0, 0))],
        out_specs=[pl.BlockSpec((1, tm, D), lambda gi, i: (gi, i, 0)),
                   pl.BlockSpec((1, tm, LANE), lambda gi, i: (gi, i, 0))],
        out_shape=[jax.ShapeDtypeStruct((G, R, D), BF16), jax.ShapeDtypeStruct((G, R, LANE), F32)],
        compiler_params=_params("arbitrary", "arbitrary"),
    )(x, g.reshape(1, D), shift.reshape(G, 1, D), scale.reshape(G, 1, D), router_w_pad)


def _moe_kernel(te_ref, na_ref, x_ref, gw_ref, wg_ref, wu_ref, wd_ref, o_ref):
    i = pl.program_id(0)

    @pl.when(i < na_ref[0])
    def _():
        x = x_ref[...]
        hg = jnp.dot(x, wg_ref[0], preferred_element_type=F32)
        hu = jnp.dot(x, wu_ref[0], preferred_element_type=F32)
        he = (hg * jax.nn.sigmoid(hg) * hu).astype(BF16)
        o_ref[...] = gw_ref[...] * jnp.dot(he, wd_ref[0], preferred_element_type=F32)

    @pl.when(i >= na_ref[0])
    def _():
        o_ref[...] = jnp.zeros_like(o_ref)


def _moe_experts(tile_expert, n_active, xs, gate_w, w_gate, w_up, w_down):
    P, D = xs.shape
    tm = MOE_TM
    grid_spec = pltpu.PrefetchScalarGridSpec(
        num_scalar_prefetch=2,
        grid=(P // tm,),
        in_specs=[pl.BlockSpec((tm, D), lambda i, te, na: (i, 0)),
                  pl.BlockSpec((tm, 1), lambda i, te, na: (i, 0)),
                  pl.BlockSpec((1, D, D_EXPERT), lambda i, te, na: (te[i], 0, 0)),
                  pl.BlockSpec((1, D, D_EXPERT), lambda i, te, na: (te[i], 0, 0)),
                  pl.BlockSpec((1, D_EXPERT, D), lambda i, te, na: (te[i], 0, 0))],
        out_specs=pl.BlockSpec((tm, D), lambda i, te, na: (i, 0)),
    )
    return pl.pallas_call(
        _moe_kernel,
        grid_spec=grid_spec,
        out_shape=jax.ShapeDtypeStruct((P, D), F32),
        compiler_params=_params("arbitrary"),
    )(tile_expert, n_active, xs, gate_w, w_gate, w_up, w_down)


def _route(logits, router_b):
    aff = jax.nn.sigmoid(logits)
    sel = (aff + router_b.astype(F32)).reshape(-1, MOE_GROUPS, EXPERTS_PER_GROUP)
    lane = lax.broadcasted_iota(jnp.int32, sel.shape, 2)

    def top2(x, ids):
        i1 = jnp.argmax(x, axis=-1)
        x2 = jnp.where(ids == i1[..., None], -jnp.inf, x)
        i2 = jnp.argmax(x2, axis=-1)
        return jnp.max(x, axis=-1), jnp.max(x2, axis=-1), i1, i2

    m1, m2, _, _ = top2(sel, lane)
    grp = jnp.argmax(m1 + m2, axis=-1)
    in_grp = jnp.take_along_axis(sel, grp[:, None, None], axis=1)[:, 0]
    _, _, l1, l2 = top2(in_grp, lane[:, 0])
    idx = grp[:, None] * EXPERTS_PER_GROUP + jnp.stack([l1, l2], axis=-1)
    wts = jnp.take_along_axis(aff, idx, axis=1)
    wts = wts / jnp.sum(wts, axis=-1, keepdims=True)
    return idx.astype(jnp.int32), wts


def _moe(h, logits, router_b, w_gate, w_up, w_down):
    M, D = h.shape
    tm = MOE_TM
    idx, wts = _route(logits, router_b)
    n_asg = M * MOE_TOP_K
    P = n_asg + N_EXPERTS * tm
    onehot = jax.nn.one_hot(idx.reshape(-1), N_EXPERTS, dtype=jnp.int32)
    csum = jnp.cumsum(onehot, axis=0)
    counts = csum[-1]
    pcounts = ((counts + tm - 1) // tm) * tm
    pends = jnp.cumsum(pcounts)
    pstarts = pends - pcounts
    dest = jnp.sum(onehot * (pstarts[None, :] + csum - 1), axis=1)
    rows_tok = jnp.zeros((P,), jnp.int32).at[dest].set(jnp.arange(n_asg, dtype=jnp.int32) // MOE_TOP_K)
    gate_w = jnp.zeros((P,), F32).at[dest].set(wts.reshape(-1))
    pos = dest.reshape(M, MOE_TOP_K)
    tile_start = jnp.arange(P // tm, dtype=jnp.int32) * tm
    tile_expert = jnp.minimum(jnp.searchsorted(pends, tile_start, side='right'), N_EXPERTS - 1).astype(jnp.int32)
    n_active = (pends[-1] // tm).astype(jnp.int32).reshape(1)
    xs = jnp.take(h, rows_tok, axis=0)
    y = _moe_experts(tile_expert, n_active, xs, gate_w.reshape(P, 1), w_gate, w_up, w_down)
    return jnp.take(y, pos[:, 0], axis=0) + jnp.take(y, pos[:, 1], axis=0)


def _final_norm_kernel(x_ref, g_ref, o_ref):
    x = x_ref[...]
    o_ref[...] = x * lax.rsqrt(jnp.mean(x * x, axis=-1, keepdims=True) + RMS_EPS) * g_ref[...]


def _final_norm(x, g):
    M, D = x.shape
    tm = 512
    return pl.pallas_call(
        _final_norm_kernel,
        grid=(M // tm,),
        in_specs=[pl.BlockSpec((tm, D), lambda i: (i, 0)), pl.BlockSpec((1, D), lambda i: (0, 0))],
        out_specs=pl.BlockSpec((tm, D), lambda i: (i, 0)),
        out_shape=jax.ShapeDtypeStruct((M, D), F32),
        compiler_params=_params("arbitrary"),
    )(x, g.reshape(1, D))


def _rms_unit(x):
    return x * lax.rsqrt(jnp.mean(x * x, axis=-1, keepdims=True) + RMS_EPS)


def _to_col_major(t, rows):
    b, n = t.shape[:2]
    return t.reshape(b, rows, GRID_W, -1).swapaxes(1, 2).reshape(b, n, -1)


def _from_col_major(t, rows):
    b, n = t.shape[:2]
    return t.reshape(b, GRID_W, rows, -1).swapaxes(1, 2).reshape(b, n, -1)


def _dwconv2d(x, w, bias):
    ch = x.shape[-1]
    y = lax.conv_general_dilated(x, w.astype(x.dtype)[:, :, None, :], window_strides=(1, 1), padding='SAME',
                                 dimension_numbers=('NHWC', 'HWIO', 'NHWC'), feature_group_count=ch)
    return y + bias.astype(x.dtype)


def _ssd_chunked(x, dt, a, bm, cm, s0):
    b, n, nh, hp = x.shape
    ng, ns = bm.shape[2], bm.shape[3]
    L = SSD_CHUNK
    nc = n // L
    hpg = nh // ng
    xc = (x * dt[..., None]).reshape(b, nc, L, ng, hpg, hp)
    bc = bm.reshape(b, nc, L, ng, ns)
    cc = cm.reshape(b, nc, L, ng, ns)
    cum = jnp.cumsum((dt * a).reshape(b, nc, L, nh), axis=2)
    mask = jnp.tril(jnp.ones((L, L), bool))[None, None, :, :, None]
    seg = cum[:, :, :, None, :] - cum[:, :, None, :, :]
    decay = jnp.exp(jnp.where(mask, seg, -jnp.inf)).reshape(b, nc, L, L, ng, hpg)
    scores = jnp.einsum('bclgn,bcsgn->bclsg', cc, bc)
    y_diag = jnp.einsum('bclsgh,bcsghp->bclghp', scores[..., None] * decay, xc)
    to_end = jnp.exp(cum[:, :, -1:, :] - cum).reshape(b, nc, L, ng, hpg)
    states = jnp.einsum('bclgn,bclghp->bcghpn', bc, xc * to_end[..., None]).reshape(b, nc, nh, hp, ns)
    chunk_decay = jnp.exp(cum[:, :, -1, :])

    def step(s, inp):
        st, dec = inp
        return s * dec[:, :, None, None] + st, s

    s_fin, s_prev = lax.scan(step, s0, (jnp.moveaxis(states, 1, 0), jnp.moveaxis(chunk_decay, 1, 0)))
    s_prev = jnp.moveaxis(s_prev, 0, 1).reshape(b, nc, ng, hpg, hp, ns)
    y_off = jnp.einsum('bclgn,bcghpn->bclghp', cc, s_prev) * jnp.exp(cum).reshape(b, nc, L, ng, hpg)[..., None]
    return (y_diag + y_off).reshape(b, n, nh, hp), s_fin


def _gla_chunked(q, k, v, lg, s0):
    b, n, nh, dk = q.shape
    dv = v.shape[-1]
    L = GLA_CHUNK
    nc = n // L
    q = q.reshape(b, nc, L, nh, dk)
    k = k.reshape(b, nc, L, nh, dk)
    v = v.reshape(b, nc, L, nh, dv)
    cum = jnp.cumsum(lg.reshape(b, nc, L, nh, dk), axis=2)
    mask = jnp.tril(jnp.ones((L, L), bool))[None, None, :, :, None, None]
    seg = jnp.exp(jnp.where(mask, cum[:, :, :, None] - cum[:, :, None], -jnp.inf))
    attn = jnp.einsum('bclhk,bcshk,bclshk->bclsh', q, k, seg)
    o_intra = jnp.einsum('bclsh,bcshv->bclhv', attn, v)
    states = jnp.einsum('bcshk,bcshv->bchkv', k * jnp.exp(cum[:, :, -1:] - cum), v)
    chunk_decay = jnp.exp(cum[:, :, -1])

    def step(s, inp):
        st, dec = inp
        return s * dec[..., None] + st, s

    s_fin, s_prev = lax.scan(step, s0, (jnp.moveaxis(states, 1, 0), jnp.moveaxis(chunk_decay, 1, 0)))
    s_prev = jnp.moveaxis(s_prev, 0, 1)
    o_inter = jnp.einsum('bclhk,bchkv->bclhv', q * jnp.exp(cum), s_prev)
    return (o_intra + o_inter).reshape(b, n, nh, dv), s_fin


_U_SPLITS = (D_SSD, SSD_CONV_CH, GLA_QK, GLA_QK, D_GLA, D_GLA, P_RWKV, 2 * SSD_HEADS, 2 * GLA_LR)


def _split_cols(u, sizes):
    out, start = [], 0
    for s in sizes:
        out.append(u[..., start:start + s])
        start += s
    return out


def _token_mixers(u, s0, rows, cols, col_major, p, l):
    b, n, _ = u.shape
    s_ssd, s_gla, s_rwkv = s0
    flip = lambda t: jnp.flip(t, axis=1)
    z, xbc, q, k, v, g, rw, dt_raw, glr = _split_cols(u, _U_SPLITS)
    xbc = jax.nn.silu(_dwconv2d(xbc.reshape(b, rows, cols, SSD_CONV_CH), p['ssd_conv_w'][l],
                                p['ssd_conv_b'][l])).reshape(b, n, SSD_CONV_CH)
    parts = [z, xbc, dt_raw, q, k, v, g, glr, rw]
    if col_major:
        parts = [_to_col_major(t, rows) for t in parts]
    z, xbc, dt_raw, q, k, v, g, glr, rw = parts

    xs, bm, cm = _split_cols(xbc, (D_SSD, SSD_BC, SSD_BC))
    xs = xs.reshape(b, n, SSD_HEADS, SSD_HEAD_DIM)
    bm = bm.reshape(b, n, SSD_GROUPS, SSD_STATE)
    cm = cm.reshape(b, n, SSD_GROUPS, SSD_STATE)
    dt = jax.nn.softplus(dt_raw.reshape(b, n, 2, SSD_HEADS) + p['ssd_dt_bias'][l])
    a = -jnp.exp(p['ssd_a_log'][l])
    y_f, sf = _ssd_chunked(xs, dt[:, :, 0], a[0], bm, cm, s_ssd[:, 0])
    y_b, sb = _ssd_chunked(flip(xs), flip(dt[:, :, 1]), a[1], flip(bm), flip(cm), s_ssd[:, 1])
    y = (y_f + flip(y_b) + xs * p['ssd_d'][l][:, None]).reshape(b, n, D_SSD) * jax.nn.silu(z)
    y = _rms_unit(y.reshape(b, n, SSD_GROUPS, D_SSD // SSD_GROUPS)).reshape(b, n, D_SSD) * p['ssd_norm_g'][l]

    qh = q.reshape(b, n, GLA_HEADS, GLA_DK) * GLA_DK ** -0.5
    kh = k.reshape(b, n, GLA_HEADS, GLA_DK)
    vh = v.reshape(b, n, GLA_HEADS, GLA_DV)
    gl = jnp.einsum('btdr,drk->btdk', glr.reshape(b, n, 2, GLA_LR), p['gla_gate_up'][l]) + p['gla_gate_b'][l]
    lg = (jax.nn.log_sigmoid(gl) / GLA_GATE_NORM).reshape(b, n, 2, GLA_HEADS, GLA_DK)
    o_f, gf = _gla_chunked(qh, kh, vh, lg[:, :, 0], s_gla[:, 0])
    o_b, gb = _gla_chunked(flip(qh), flip(kh), flip(vh), flip(lg[:, :, 1]), s_gla[:, 1])
    o_gla = (_rms_unit(o_f + flip(o_b)) * p['gla_norm_g'][l]
             * jax.nn.silu(g.reshape(b, n, GLA_HEADS, GLA_DV))).reshape(b, n, D_GLA)

    mu = p['rwkv_mu'][l]
    prev = jnp.pad(rw, ((0, 0), (1, 0), (0, 0)))[:, :-1]
    nxt = jnp.pad(rw, ((0, 0), (0, 1), (0, 0)))[:, 1:]
    rw = rw + mu[0] * (prev - rw) + mu[1] * (nxt - rw)
    r, kr, vr, wd, ad, gd = _split_cols(rw, (D_RWKV, D_RWKV, D_RWKV, 2 * RWKV_LW, 2 * RWKV_LA, RWKV_LG))
    w_log = jax.nn.log_sigmoid(p['rwkv_w0'][l] + jnp.einsum(
        'btdr,drc->btdc', jnp.tanh(wd.reshape(b, n, 2, RWKV_LW)), p['rwkv_w2'][l])) - 0.5
    lw = -jnp.exp(w_log)
    a_icl = jax.nn.sigmoid(p['rwkv_a0'][l] + jnp.einsum(
        'btdr,drc->btdc', ad.reshape(b, n, 2, RWKV_LA), p['rwkv_a2'][l]))
    gate = jax.nn.sigmoid(gd) @ p['rwkv_g2'][l]
    kk = (kr * p['rwkv_k_k'][l]).reshape(b, n, RWKV_HEADS, RWKV_N)
    kk = (kk * lax.rsqrt(jnp.sum(kk * kk, axis=-1, keepdims=True) + 1e-12)).reshape(b, n, D_RWKV)
    k_eff = kr[:, :, None] * (1.0 + (a_icl - 1.0) * p['rwkv_k_a'][l])
    dirs = lambda t: jnp.swapaxes(t, 1, 2)
    o_f, o_b, s_rw = _rwkv_scan(r, vr, kk, dirs(lw), dirs(k_eff), dirs(a_icl), s_rwkv)
    o = (o_f + o_b).reshape(b, n, RWKV_HEADS, RWKV_N)
    mean = jnp.mean(o, axis=-1, keepdims=True)
    var = jnp.mean((o - mean) ** 2, axis=-1, keepdims=True)
    o = ((o - mean) * lax.rsqrt(var + RWKV_LN_EPS)).reshape(b, n, D_RWKV) * p['rwkv_ln_g'][l] + p['rwkv_ln_b'][l]
    rh = r.reshape(b, n, RWKV_HEADS, RWKV_N)
    bonus = jnp.sum(rh[:, :, None] * k_eff.reshape(b, n, 2, RWKV_HEADS, RWKV_N), axis=2) * p['rwkv_r_k'][l]
    bonus = jnp.sum(bonus, axis=-1, keepdims=True) * vr.reshape(b, n, RWKV_HEADS, RWKV_N)
    o_rwkv = (o + bonus.reshape(b, n, D_RWKV)) * gate

    out = jnp.concatenate([y, o_gla, o_rwkv], axis=-1)
    if col_major:
        out = _from_col_major(out, rows)
    return out.astype(BF16), (jnp.stack([sf, sb], axis=1), jnp.stack([gf, gb], axis=1), s_rw)


def _regroup_w_in(w):
    z, xbc, dt, q, k, v, g, glr, rw = _split_cols(w, IN_SPLITS)
    pad = jnp.zeros((w.shape[0], P_IN_PAD - sum(IN_SPLITS)), w.dtype)
    return jnp.concatenate([z, xbc, q, k, v, g, rw, dt, glr, pad], axis=1).astype(BF16)


def kernel(x_prompt, x_sample, state_ssd, state_gla, state_rwkv, c, c_ctx, ada_w, ada_b, norm_mix_g, norm_ffn_g,
           w_in, ssd_conv_w, ssd_conv_b, ssd_dt_bias, ssd_a_log, ssd_d, ssd_norm_g, gla_gate_up, gla_gate_b,
           gla_norm_g, rwkv_mu, rwkv_w0, rwkv_w2, rwkv_a0, rwkv_a2, rwkv_g2, rwkv_k_k, rwkv_k_a, rwkv_r_k,
           rwkv_ln_g, rwkv_ln_b, w_out, router_w, router_b, moe_w_gate, moe_w_up, moe_w_down, final_norm_g):
    p = dict(ssd_conv_w=ssd_conv_w, ssd_conv_b=ssd_conv_b, ssd_dt_bias=ssd_dt_bias, ssd_a_log=ssd_a_log,
             ssd_d=ssd_d, ssd_norm_g=ssd_norm_g, gla_gate_up=gla_gate_up, gla_gate_b=gla_gate_b,
             gla_norm_g=gla_norm_g, rwkv_mu=rwkv_mu, rwkv_w0=rwkv_w0, rwkv_w2=rwkv_w2, rwkv_a0=rwkv_a0,
             rwkv_a2=rwkv_a2, rwkv_g2=rwkv_g2, rwkv_k_k=rwkv_k_k, rwkv_k_a=rwkv_k_a, rwkv_r_k=rwkv_r_k,
             rwkv_ln_g=rwkv_ln_g, rwkv_ln_b=rwkv_ln_b)
    D = D_MODEL
    nb, seq = x_prompt.shape[:2]
    db, dseq = x_sample.shape[:2]
    grp = nb * seq
    assert dseq == grp
    G = 1 + db
    x = jnp.concatenate([x_prompt.reshape(1, grp, D), x_sample], axis=0)
    cvec = jnp.concatenate([c_ctx[None, :], c, jnp.zeros((8 - G, D), F32)], axis=0)
    mods = _ada(cvec, ada_w, ada_b)[:, :G]
    router_w_pad = jnp.pad(router_w, ((0, 0), (0, LANE - N_EXPERTS)))
    zero_states = (jnp.zeros((nb, 2, SSD_HEADS, SSD_HEAD_DIM, SSD_STATE), F32),
                   jnp.zeros((nb, 2, GLA_HEADS, GLA_DK, GLA_DV), F32),
                   jnp.zeros((nb, 2, RWKV_HEADS, RWKV_N, RWKV_N), F32))
    ctx_states = []
    for l in range(DEPTH):
        sh1, sc1, g1, sh2, sc2, g2 = jnp.split(mods[l], 6, axis=-1)
        u = _inproj(x, norm_mix_g[l], sh1, sc1, _regroup_w_in(w_in[l]))
        mix_p, new = _token_mixers(u[0].reshape(nb, seq, P_IN_PAD), zero_states, 1, seq, False, p, l)
        s0 = (state_ssd[:, l], state_gla[:, l], state_rwkv[:, l])
        mix_s, _ = _token_mixers(u[1:], s0, dseq // GRID_W, GRID_W, l % 2 == 1, p, l)
        ctx_states.append(new)
        mix = jnp.concatenate([mix_p.reshape(1, grp, D), mix_s], axis=0)
        x = _outproj(mix, w_out[l].astype(BF16), x, g1)
        h, logits = _ffn_in(x, norm_ffn_g[l], sh2, sc2, router_w_pad)
        moe = _moe(h.reshape(G * grp, D), logits.reshape(G * grp, LANE)[:, :N_EXPERTS], router_b,
                   moe_w_gate[l].astype(BF16), moe_w_up[l].astype(BF16), moe_w_down[l].astype(BF16))
        x = x + g2[:, None, :] * moe.reshape(G, grp, D)
    y = _final_norm(x.reshape(G * grp, D), final_norm_g).reshape(G, grp, D)
    y_prompt = y[0].reshape(nb, seq, D)
    y_sample = y[1:]
    new_ssd = jnp.stack([s[0] for s in ctx_states], axis=1)
    new_gla = jnp.stack([s[1] for s in ctx_states], axis=1)
    new_rwkv = jnp.stack([s[2] for s in ctx_states], axis=1)
    return (y_prompt, y_sample, new_ssd, new_gla, new_rwkv)
```

```python
import functools

import jax
import jax.numpy as jnp
from jax import lax
from jax.experimental import pallas as pl
from jax.experimental.pallas import tpu as pltpu

F32 = jnp.float32
BF16 = jnp.bfloat16

D_MODEL = 2048
DEPTH = 2
GRID_W = 64
SSD_HEAD_DIM = 64
D_SSD = 1024
SSD_HEADS = 16
SSD_GROUPS = 2
SSD_STATE = 128
SSD_BC = SSD_GROUPS * SSD_STATE
SSD_CONV_CH = D_SSD + 2 * SSD_BC
GLA_HEADS = 4
D_GLA = 512
GLA_DV = 128
GLA_DK = 64
GLA_QK = GLA_HEADS * GLA_DK
GLA_LR = 16
GLA_GATE_NORM = 16.0
RWKV_N = 64
D_RWKV = 512
RWKV_HEADS = 8
RWKV_LW = 64
RWKV_LA = 64
RWKV_LG = 128
P_RWKV = 3 * D_RWKV + 2 * RWKV_LW + 2 * RWKV_LA + RWKV_LG
IN_SPLITS = (D_SSD, SSD_CONV_CH, 2 * SSD_HEADS, GLA_QK, GLA_QK, D_GLA, D_GLA, 2 * GLA_LR, P_RWKV)
N_EXPERTS = 16
MOE_GROUPS = 4
EXPERTS_PER_GROUP = 4
MOE_TOP_K = 2
D_EXPERT = 1024
RMS_EPS = 1e-6
RWKV_LN_EPS = 64e-5

LANE = 128
CHUNK = 64
MOE_TM = 256
VMEM_LIMIT = 56 * 1024 * 1024

P_IN_PAD = 6144
COL_RW = (2048, 0)
COL_DT = (LANE, 15)
COL_Z = (1024, 2)
COL_X = (1024, 3)
COL_BC = (512, 8)
COL_QK = (512, 9)
COL_V = (512, 10)
COL_G = (512, 11)
CONV_TILE = 256
CONV_FIRST_TILE = 12


def _dot(a, b, contract, passes):
    dn = (contract, ((), ()))
    if passes == 6:
        return lax.dot_general(a, b, dn, precision=lax.Precision.HIGHEST, preferred_element_type=F32)
    ah = a.astype(BF16)
    bh = b.astype(BF16)
    out = lax.dot_general(ah, bh, dn, preferred_element_type=F32)
    if passes == 3:
        al = (a - ah.astype(F32)).astype(BF16)
        bl = (b - bh.astype(F32)).astype(BF16)
        out = out + lax.dot_general(ah, bl, dn, preferred_element_type=F32)
        out = out + lax.dot_general(al, bh, dn, preferred_element_type=F32)
    return out


_NN = ((1,), (0,))
_NT = ((1,), (1,))
_TN = ((0,), (0,))


def _params(*sem):
    return pltpu.CompilerParams(dimension_semantics=sem, vmem_limit_bytes=VMEM_LIMIT)


def _split3(x):
    hi = x.astype(BF16)
    r1 = x - hi.astype(F32)
    mid = r1.astype(BF16)
    lo = (r1 - mid.astype(F32)).astype(BF16)
    return hi, mid, lo


def _sel_dot(sel, x, contract=_NN):
    dn = (contract, ((), ()))
    s = sel.astype(BF16)
    hi, mid, lo = _split3(x)
    out = lax.dot_general(s, hi, dn, preferred_element_type=F32)
    out = out + lax.dot_general(s, mid, dn, preferred_element_type=F32)
    return out + lax.dot_general(s, lo, dn, preferred_element_type=F32)


def _dot_sel(x, sel):
    dn = (_NN, ((), ()))
    s = sel.astype(BF16)
    hi, mid, lo = _split3(x)
    out = lax.dot_general(hi, s, dn, preferred_element_type=F32)
    out = out + lax.dot_general(mid, s, dn, preferred_element_type=F32)
    return out + lax.dot_general(lo, s, dn, preferred_element_type=F32)


def _bdot(a, b, contract=_NN):
    return lax.dot_general(a.astype(BF16), b.astype(BF16), (contract, ((), ())), preferred_element_type=F32)


def _silu(x):
    return x * jax.nn.sigmoid(x)


def _softplus(x):
    return jnp.maximum(x, 0.0) + jnp.log(1.0 + jnp.exp(-jnp.abs(x)))


def _log_sigmoid(x):
    return jnp.minimum(x, 0.0) - jnp.log(1.0 + jnp.exp(-jnp.abs(x)))


def _time_masks(d):
    row = lax.broadcasted_iota(jnp.int32, (CHUNK, CHUNK), 0)
    col = lax.broadcasted_iota(jnp.int32, (CHUNK, CHUNK), 1)
    if d == 0:
        return col <= row, col < row
    return col >= row, col > row


def _seq_specs(width_idx, b0, nc, rows=CHUNK):
    w, j = width_idx
    return [pl.BlockSpec((1, rows, w), lambda b, c: (b + b0, c, j)),
            pl.BlockSpec((1, rows, w), lambda b, c: (b + b0, nc - 1 - c, j))]


def _conv_kernel(x_ref, w_ref, b_ref, o_ref, *, width, rows):
    x = x_ref[0]
    T = x.shape[0]
    t = lax.broadcasted_iota(jnp.int32, (T, 1), 0)
    col = t % width
    row = (t // width) % rows
    acc = jnp.zeros_like(x) + b_ref[...]
    for dr in (-1, 0, 1):
        if rows == 1 and dr != 0:
            continue
        for dc in (-1, 0, 1):
            off = dr * width + dc
            xs = x if off == 0 else pltpu.roll(x, (-off) % T, axis=0)
            valid = (col + dc >= 0) & (col + dc < width) & (row + dr >= 0) & (row + dr < rows)
            tap = (dr + 1) * 3 + (dc + 1)
            acc = acc + jnp.where(valid, xs, 0.0) * w_ref[tap:tap + 1, :]
    o_ref[0] = _silu(acc)


def _conv_inplace(u, w9, bias, g0, ng, width, rows):
    G, T, P = u.shape
    nt = SSD_CONV_CH // CONV_TILE
    blk = pl.BlockSpec((1, T, CONV_TILE), lambda g, j: (g + g0, 0, CONV_FIRST_TILE + j))
    return pl.pallas_call(
        functools.partial(_conv_kernel, width=width, rows=rows),
        grid=(ng, nt),
        in_specs=[blk,
                  pl.BlockSpec((9, CONV_TILE), lambda g, j: (0, j)),
                  pl.BlockSpec((1, CONV_TILE), lambda g, j: (0, j))],
        out_specs=blk,
        out_shape=jax.ShapeDtypeStruct(u.shape, u.dtype),
        input_output_aliases={0: 0},
        compiler_params=_params("arbitrary", "arbitrary"),
    )(u, w9, bias)


def _ssd_kernel(xf_ref, bcf_ref, dtf_ref, xb_ref, bcb_ref, dtb_ref, bias_ref, a_ref, s0_ref,
                yf_ref, yb_ref, sfin_ref, st_scr):
    c = pl.program_id(1)
    L = CHUNK
    HP = D_SSD

    @pl.when(c == 0)
    def _():
        st_scr[...] = s0_ref[0]

    rowx = lax.broadcasted_iota(jnp.int32, (L, HP), 0)
    s_of_lane = lax.broadcasted_iota(jnp.int32, (L, HP), 1) % SSD_HEAD_DIM
    e_row = lax.broadcasted_iota(jnp.int32, (LANE, HP), 0)
    e_head = lax.broadcasted_iota(jnp.int32, (LANE, HP), 1) // SSD_HEAD_DIM
    lane2 = lax.broadcasted_iota(jnp.int32, (L, LANE), 1)
    for d, (x_ref, bc_ref, dt_ref, y_ref) in enumerate(((xf_ref, bcf_ref, dtf_ref, yf_ref),
                                                        (xb_ref, bcb_ref, dtb_ref, yb_ref))):
        incl, _ = _time_masks(d)
        strict_t = (s_of_lane < rowx) if d == 0 else (s_of_lane > rowx)
        incl_t = (s_of_lane <= rowx) if d == 0 else (s_of_lane >= rowx)
        expand = e_row == d * SSD_HEADS + e_head
        dtv = _softplus(dt_ref[0] + bias_ref[...])
        dtx = _dot_sel(dtv, expand)
        dax = _dot_sel(dtv * a_ref[...], expand)
        cumx = _sel_dot(incl, dax)
        totx = jnp.sum(dax, axis=0, keepdims=True)
        segx = _sel_dot(incl, jnp.where(strict_t, dax, 0.0))
        decx = jnp.where(incl_t, jnp.exp(jnp.minimum(segx, 0.0)), 0.0)
        x = x_ref[0]
        xdt = x * dtx
        xend = xdt * jnp.exp(totx - cumx)
        ecum = jnp.exp(cumx)
        etot = jnp.exp(totx)
        bc = bc_ref[0]
        for g in range(SSD_GROUPS):
            bm = bc[:, g * SSD_STATE:(g + 1) * SSD_STATE]
            cm = bc[:, SSD_BC + g * SSD_STATE:SSD_BC + (g + 1) * SSD_STATE]
            sc = _bdot(cm, bm, _NT)
            sc2 = jnp.concatenate([sc, sc], axis=1)
            for i in range(SSD_HEADS // SSD_GROUPS // 2):
                ls = slice((g * 4 + i) * LANE, (g * 4 + i + 1) * LANE)
                xp = xdt[:, ls]
                xbd = jnp.concatenate([jnp.where(lane2 < SSD_HEAD_DIM, xp, 0.0),
                                       jnp.where(lane2 >= SSD_HEAD_DIM, xp, 0.0)], axis=0)
                st = st_scr[d, :, ls]
                y_ref[0, :, ls] = _bdot(decx[:, ls] * sc2, xbd) + _bdot(cm, st) * ecum[:, ls]
                st_scr[d, :, ls] = st * etot[:, ls] + _bdot(bm, xend[:, ls], _TN)

    @pl.when(c == pl.num_programs(1) - 1)
    def _():
        sfin_ref[0] = st_scr[...]


def _ssd_scan(u, b0, B, T, bias128, a128, s0):
    nc = T // CHUNK
    st = pl.BlockSpec((1, 2, SSD_STATE, D_SSD), lambda b, c: (b, 0, 0, 0))
    vec = pl.BlockSpec((1, LANE), lambda b, c: (0, 0))
    xs, bcs, dts = _seq_specs(COL_X, b0, nc), _seq_specs(COL_BC, b0, nc), _seq_specs(COL_DT, b0, nc)
    ys = _seq_specs((D_SSD, 0), 0, nc)
    return pl.pallas_call(
        _ssd_kernel,
        grid=(B, nc),
        in_specs=[xs[0], bcs[0], dts[0], xs[1], bcs[1], dts[1], vec, vec, st],
        out_specs=[ys[0], ys[1], st],
        out_shape=[jax.ShapeDtypeStruct((B, T, D_SSD), F32), jax.ShapeDtypeStruct((B, T, D_SSD), F32),
                   jax.ShapeDtypeStruct((B, 2, SSD_STATE, D_SSD), F32)],
        scratch_shapes=[pltpu.VMEM((2, SSD_STATE, D_SSD), F32)],
        compiler_params=_params("arbitrary", "arbitrary"),
    )(u, u, u, u, u, u, bias128, a128, s0)


def _stack_heads(x, head_of_lane):
    return jnp.concatenate([jnp.where(head_of_lane == h, x, 0.0) for h in range(GLA_HEADS)], axis=0)


def _gla_kernel(qkf_ref, vf_ref, dtf_ref, qkb_ref, vb_ref, dtb_ref, wup_ref, gb_ref, s0_ref,
                of_ref, ob_ref, sfin_ref, st_scr):
    c = pl.program_id(1)
    L = CHUNK

    @pl.when(c == 0)
    def _():
        st_scr[...] = s0_ref[0]

    row = lax.broadcasted_iota(jnp.int32, (L, L), 0)
    col = lax.broadcasted_iota(jnp.int32, (L, L), 1)
    row1 = lax.broadcasted_iota(jnp.int32, (L, 1), 0)
    head_of_lane = lax.broadcasted_iota(jnp.int32, (L, GLA_QK), 1) // GLA_DK
    row4 = lax.broadcasted_iota(jnp.int32, (GLA_HEADS * L, L), 0) % L
    col4 = lax.broadcasted_iota(jnp.int32, (GLA_HEADS * L, L), 1)
    for d, (qk_ref, v_ref, dt_ref, o_ref) in enumerate(((qkf_ref, vf_ref, dtf_ref, of_ref),
                                                        (qkb_ref, vb_ref, dtb_ref, ob_ref))):
        incl, _ = _time_masks(d)
        flip = (lambda i: i) if d == 0 else (lambda i: L - 1 - i)
        tq, tj, tq1, tq4, tj4 = flip(row), flip(col), flip(row1), flip(row4), flip(col4)
        qk = qk_ref[0]
        q = qk[:, :GLA_QK] * GLA_DK ** -0.5
        k = qk[:, GLA_QK:]
        v = v_ref[0]
        lg = _log_sigmoid(_dot(dt_ref[0], wup_ref[d], _NN, 3) + gb_ref[d]) / GLA_GATE_NORM
        cum = _sel_dot(incl, lg)
        tot = jnp.sum(lg, axis=0, keepdims=True)
        acc = jnp.where(row4 == col4, _bdot(_stack_heads(q, head_of_lane), k, _NT), 0.0)
        m = L // 2
        while m >= 1:
            blk = 2 * m
            mid = (tq // blk) * blk + m
            tri_q = ((tq % blk) >= m) & (tj >= mid) & (tj <= tq)
            tri_k = ((tq % blk) < m) & (tj > tq) & (tj < mid)
            qs = jnp.where((tq1 % blk) >= m, q * jnp.exp(_sel_dot(tri_q, lg)), 0.0)
            ks = jnp.where((tq1 % blk) < m, k * jnp.exp(_sel_dot(tri_k, lg)), 0.0)
            a_lvl = _bdot(_stack_heads(qs, head_of_lane), ks, _NT)
            acc = acc + jnp.where((tq4 // blk) == (tj4 // blk), a_lvl, 0.0)
            m //= 2
        st = st_scr[d]
        o_inter = _bdot(_stack_heads(q * jnp.exp(cum), head_of_lane), st, _NT)
        for h in range(GLA_HEADS):
            rs = slice(h * L, (h + 1) * L)
            ls = slice(h * GLA_DV, (h + 1) * GLA_DV)
            o_ref[0, :, ls] = _bdot(acc[rs], v[:, ls]) + o_inter[rs]
        vst = jnp.concatenate([v[:, h * GLA_DV:(h + 1) * GLA_DV] for h in range(GLA_HEADS)], axis=0)
        kst = _stack_heads(k * jnp.exp(tot - cum), head_of_lane)
        st_scr[d] = st * jnp.exp(tot) + _bdot(vst, kst, _TN)

    @pl.when(c == pl.num_programs(1) - 1)
    def _():
        sfin_ref[0] = st_scr[...]


def _gla_scan(u, b0, B, T, wup, gb, s0):
    nc = T // CHUNK
    st = pl.BlockSpec((1, 2, GLA_DV, GLA_QK), lambda b, c: (b, 0, 0, 0))
    qks, vs, dts = _seq_specs(COL_QK, b0, nc), _seq_specs(COL_V, b0, nc), _seq_specs(COL_DT, b0, nc)
    os_ = _seq_specs((D_GLA, 0), 0, nc)
    return pl.pallas_call(
        _gla_kernel,
        grid=(B, nc),
        in_specs=[qks[0], vs[0], dts[0], qks[1], vs[1], dts[1],
                  pl.BlockSpec((2, LANE, GLA_QK), lambda b, c: (0, 0, 0)),
                  pl.BlockSpec((2, 1, GLA_QK), lambda b, c: (0, 0, 0)), st],
        out_specs=[os_[0], os_[1], st],
        out_shape=[jax.ShapeDtypeStruct((B, T, D_GLA), F32), jax.ShapeDtypeStruct((B, T, D_GLA), F32),
                   jax.ShapeDtypeStruct((B, 2, GLA_DV, GLA_QK), F32)],
        scratch_shapes=[pltpu.VMEM((2, GLA_DV, GLA_QK), F32)],
        compiler_params=_params("arbitrary", "arbitrary"),
    )(u, u, u, u, u, u, wup, gb, s0)


def _rwkv_kernel(xf_ref, hpf_ref, hnf_ref, xb_ref, hpb_ref, hnb_ref, mu_ref, w0_ref, w2_ref, a0_ref, a2_ref,
                 g2_ref, kkw_ref, ka_ref, rk_ref, s0_ref, of_ref, ob_ref, gate_ref, bonus_ref, sfin_ref, s_scr):
    c = pl.program_id(1)
    nc = pl.num_programs(1)
    C = CHUNK
    N = RWKV_N
    D = D_RWKV

    @pl.when(c == 0)
    def _():
        s_scr[...] = s0_ref[0]

    row = lax.broadcasted_iota(jnp.int32, (C, C), 0)
    col = lax.broadcasted_iota(jnp.int32, (C, C), 1)
    row1 = lax.broadcasted_iota(jnp.int32, (C, 1), 0)
    eye = (col == row).astype(F32)
    same_head = (lax.broadcasted_iota(jnp.int32, (D, D), 0) // N) == (lax.broadcasted_iota(jnp.int32, (D, D), 1) // N)

    def a_icl(ad, d):
        return jax.nn.sigmoid(a0_ref[d] + _dot(ad, a2_ref[d], _NN, 3))

    def k_eff(kr, a):
        return kr * (1.0 + (a - 1.0) * ka_ref[...])

    chains = []
    for d, (x_ref, hp_ref, hn_ref, o_ref) in enumerate(((xf_ref, hpf_ref, hnf_ref, of_ref),
                                                        (xb_ref, hpb_ref, hnb_ref, ob_ref))):
        cc = c if d == 0 else nc - 1 - c
        incl, strict = _time_masks(d)
        x = x_ref[0]
        prev_row = jnp.where(cc > 0, hp_ref[0][7:8, :], 0.0)
        next_row = jnp.where(cc < nc - 1, hn_ref[0][0:1, :], 0.0)
        prev = jnp.where(row1 == 0, prev_row, pltpu.roll(x, 1, axis=0))
        nxt = jnp.where(row1 == C - 1, next_row, pltpu.roll(x, C - 1, axis=0))
        xm = x + mu_ref[0:1, :] * (prev - x) + mu_ref[1:2, :] * (nxt - x)
        r = xm[:, 0:D]
        kr = xm[:, D:2 * D]
        v = xm[:, 2 * D:3 * D]
        wd = xm[:, 3 * D:3 * D + LANE]
        ad = xm[:, 3 * D + LANE:3 * D + 2 * LANE]
        w_log = _log_sigmoid(w0_ref[d] + _dot(jnp.tanh(wd), w2_ref[d], _NN, 3)) - 0.5
        lw = -jnp.exp(w_log)
        a = a_icl(ad, d)
        kk = kr * kkw_ref[...]
        kk = kk * lax.rsqrt(_dot_sel(kk * kk, same_head) + 1e-12)
        k = k_eff(kr, a)
        if d == 0:
            gd = xm[:, 3 * D + 2 * LANE:3 * D + 3 * LANE]
            gate_ref[0] = _dot(jax.nn.sigmoid(gd), g2_ref[...], _NN, 3)
            k_both = k + k_eff(kr, a_icl(ad, 1))
            bonus_ref[0] = _dot_sel(r * k_both * rk_ref[...], same_head) * v

        cum = _sel_dot(incl, lw)
        tot = jnp.sum(lw, axis=0, keepdims=True)
        b = a * kk
        kp = kk * jnp.exp(cum - lw)
        rp = r * jnp.exp(cum)
        pinv = jnp.exp(-cum)
        kinv = k * pinv
        binv = b * pinv
        pend = jnp.exp(tot - cum)
        kd = k * pend
        bd = b * pend
        ptot = jnp.exp(tot)
        for h in range(RWKV_HEADS):
            sl = slice(h * N, (h + 1) * N)
            chains.append(dict(
                d=d, h=h, sl=sl, incl=incl, strict=strict, o_ref=o_ref, v=v[:, sl], kd=kd[:, sl], bd=bd[:, sl],
                ptot=ptot[:, sl],
                x=jnp.concatenate([kp[:, sl], rp[:, sl]], axis=0).astype(BF16),
                y=jnp.concatenate([kinv[:, sl], binv[:, sl]], axis=0).astype(BF16)))

    for ch in chains:
        aa = _bdot(ch['x'], ch['y'], _NT)
        ch['a_kk'] = jnp.where(ch['strict'], aa[:C, :C], 0.0)
        ch['pw'] = jnp.where(ch['strict'], aa[:C, C:], 0.0)
        ch['a_rk'] = jnp.where(ch['incl'], aa[C:, :C], 0.0)
        ch['a_rb'] = jnp.where(ch['incl'], aa[C:, C:], 0.0)
        ch['t'] = eye - ch['pw']
    n = 1
    while 2 * n < C + 1:
        for ch in chains:
            ch['pw'] = _bdot(ch['pw'], ch['pw'])
        for ch in chains:
            ch['t'] = ch['t'] + _bdot(ch['t'], ch['pw'])
        n *= 2
    for ch in chains:
        ch['s0'] = s_scr[ch['d'], ch['h']]
        ch['xs'] = _bdot(ch['x'], ch['s0'], _NT)
        ch['g'] = ch['xs'][:C] + _bdot(ch['a_kk'], ch['v'])
    for ch in chains:
        ch['u'] = _bdot(ch['t'], ch['g'])
    for ch in chains:
        o = ch['xs'][C:] + _bdot(ch['a_rk'], ch['v']) - _bdot(ch['a_rb'], ch['u'])
        ch['o_ref'][0, :, ch['sl']] = o
        s_scr[ch['d'], ch['h']] = (ch['s0'] * ch['ptot'] + _bdot(ch['v'], ch['kd'], _TN)
                                   - _bdot(ch['u'], ch['bd'], _TN))

    @pl.when(c == nc - 1)
    def _():
        sfin_ref[0] = s_scr[...]


def _rwkv_scan(u, b0, B, T, prm, s0):
    nc = T // CHUNK
    nb8 = T // 8
    w, j = COL_RW
    xs = _seq_specs(COL_RW, b0, nc)
    halo_prev = [pl.BlockSpec((1, 8, w), lambda b, c: (b + b0, jnp.maximum(c * 8 - 1, 0), j)),
                 pl.BlockSpec((1, 8, w), lambda b, c: (b + b0, jnp.maximum((nc - 1 - c) * 8 - 1, 0), j))]
    halo_next = [pl.BlockSpec((1, 8, w), lambda b, c: (b + b0, jnp.minimum((c + 1) * 8, nb8 - 1), j)),
                 pl.BlockSpec((1, 8, w), lambda b, c: (b + b0, jnp.minimum((nc - c) * 8, nb8 - 1), j))]
    st = pl.BlockSpec((1, 2, RWKV_HEADS, RWKV_N, RWKV_N), lambda b, c: (b, 0, 0, 0, 0))
    os_ = _seq_specs((D_RWKV, 0), 0, nc)

    def full(a):
        return pl.BlockSpec(a.shape, lambda b, c, _n=a.ndim: (0,) * _n)

    names = ('mu', 'w0', 'w2', 'a0', 'a2', 'g2', 'k_k', 'k_a', 'r_k')
    tok = jax.ShapeDtypeStruct((B, T, D_RWKV), F32)
    return pl.pallas_call(
        _rwkv_kernel,
        grid=(B, nc),
        in_specs=[xs[0], halo_prev[0], halo_next[0], xs[1], halo_prev[1], halo_next[1]]
                 + [full(prm[n]) for n in names] + [st],
        out_specs=[os_[0], os_[1], os_[0], os_[0], st],
        out_shape=[tok, tok, tok, tok, jax.ShapeDtypeStruct((B, 2, RWKV_HEADS, RWKV_N, RWKV_N), F32)],
        scratch_shapes=[pltpu.VMEM((2, RWKV_HEADS, RWKV_N, RWKV_N), F32)],
        compiler_params=_params("arbitrary", "arbitrary"),
    )(u, u, u, u, u, u, *[prm[n] for n in names], s0)


def _epilogue_kernel(yf_ref, yb_ref, z_ref, x_ref, gf_ref, gb_ref, g_ref, rf_ref, rb_ref, gate_ref, bonus_ref,
                     dx_ref, sng_ref, gng_ref, lng_ref, lnb_ref, o_ref):
    y = (yf_ref[0] + yb_ref[0] + x_ref[0] * dx_ref[...]) * _silu(z_ref[0])
    half = D_SSD // SSD_GROUPS
    for g in range(SSD_GROUPS):
        ls = slice(g * half, (g + 1) * half)
        yy = y[:, ls]
        yy = yy * lax.rsqrt(jnp.mean(yy * yy, axis=-1, keepdims=True) + RMS_EPS)
        o_ref[0, :, ls] = (yy * sng_ref[:, ls]).astype(BF16)
    go = gf_ref[0] + gb_ref[0]
    gg = g_ref[0]
    for h in range(GLA_HEADS):
        ls = slice(h * GLA_DV, (h + 1) * GLA_DV)
        oo = go[:, ls]
        oo = oo * lax.rsqrt(jnp.mean(oo * oo, axis=-1, keepdims=True) + RMS_EPS)
        o_ref[0, :, D_SSD + h * GLA_DV:D_SSD + (h + 1) * GLA_DV] = (oo * gng_ref[...] * _silu(gg[:, ls])).astype(BF16)
    ro = rf_ref[0] + rb_ref[0]
    same_head = ((lax.broadcasted_iota(jnp.int32, (D_RWKV, D_RWKV), 0) // RWKV_N)
                 == (lax.broadcasted_iota(jnp.int32, (D_RWKV, D_RWKV), 1) // RWKV_N))
    mean = _dot_sel(ro, same_head) * (1.0 / RWKV_N)
    cen = ro - mean
    var = _dot_sel(cen * cen, same_head) * (1.0 / RWKV_N)
    o = cen * lax.rsqrt(var + RWKV_LN_EPS) * lng_ref[...] + lnb_ref[...]
    o_ref[0, :, D_SSD + D_GLA:] = ((o + bonus_ref[0]) * gate_ref[0]).astype(BF16)


def _epilogue(u, b0, B, T, ssd_y, gla_o, rwkv_o, prm):
    tm = min(T, 512)

    def tok(width_idx, off):
        w, j = width_idx
        return pl.BlockSpec((1, tm, w), lambda b, i: (b + off, i, j))

    def vec(a):
        return pl.BlockSpec(a.shape, lambda b, i: (0, 0))

    y1, g1, r1 = tok((D_SSD, 0), 0), tok((D_GLA, 0), 0), tok((D_RWKV, 0), 0)
    names = ('dx', 'ssd_norm_g', 'gla_norm_g', 'ln_g', 'ln_b')
    return pl.pallas_call(
        _epilogue_kernel,
        grid=(B, T // tm),
        in_specs=[y1, y1, tok(COL_Z, b0), tok(COL_X, b0), g1, g1, tok(COL_G, b0), r1, r1, r1, r1]
                 + [vec(prm[n]) for n in names],
        out_specs=tok((D_MODEL, 0), 0),
        out_shape=jax.ShapeDtypeStruct((B, T, D_MODEL), BF16),
        compiler_params=_params("arbitrary", "arbitrary"),
    )(ssd_y[0], ssd_y[1], u, u, gla_o[0], gla_o[1], u, rwkv_o[0], rwkv_o[1], rwkv_o[2], rwkv_o[3],
      *[prm[n] for n in names])


def _ada_kernel(c_ref, w_ref, b_ref, o_ref):
    o_ref[0] = _dot(_silu(c_ref[...]), w_ref[0], _NN, 1) + b_ref[0]


def _ada(cvec8, ada_w, ada_b):
    tn = 1536
    n = ada_w.shape[-1]
    return pl.pallas_call(
        _ada_kernel,
        grid=(DEPTH, n // tn),
        in_specs=[pl.BlockSpec((8, D_MODEL), lambda l, j: (0, 0)),
                  pl.BlockSpec((1, D_MODEL, tn), lambda l, j: (l, 0, j)),
                  pl.BlockSpec((1, 1, tn), lambda l, j: (l, 0, j))],
        out_specs=pl.BlockSpec((1, 8, tn), lambda l, j: (l, 0, j)),
        out_shape=jax.ShapeDtypeStruct((DEPTH, 8, n), F32),
        compiler_params=_params("arbitrary", "arbitrary"),
    )(cvec8, ada_w, ada_b.reshape(DEPTH, 1, n))


def _norm_mod(x, g, shift, scale):
    y = x * lax.rsqrt(jnp.mean(x * x, axis=-1, keepdims=True) + RMS_EPS)
    return (y * g) * (1.0 + scale) + shift


def _inproj_kernel(x_ref, g_ref, sh_ref, sc_ref, w_ref, o_ref, h_scr):
    @pl.when(pl.program_id(2) == 0)
    def _():
        h_scr[...] = _norm_mod(x_ref[0], g_ref[...], sh_ref[0], sc_ref[0]).astype(BF16)

    o_ref[0] = jnp.dot(h_scr[...], w_ref[...], preferred_element_type=F32)


def _inproj(x, g, shift, scale, w):
    G, R, D = x.shape
    N = w.shape[1]
    tm, tn = 1024, 1024
    vec = pl.BlockSpec((1, 1, D), lambda gi, i, j: (gi, 0, 0))
    return pl.pallas_call(
        _inproj_kernel,
        grid=(G, R // tm, N // tn),
        in_specs=[pl.BlockSpec((1, tm, D), lambda gi, i, j: (gi, i, 0)),
                  pl.BlockSpec((1, D), lambda gi, i, j: (0, 0)),
                  vec, vec,
                  pl.BlockSpec((D, tn), lambda gi, i, j: (0, j))],
        out_specs=pl.BlockSpec((1, tm, tn), lambda gi, i, j: (gi, i, j)),
        out_shape=jax.ShapeDtypeStruct((G, R, N), F32),
        scratch_shapes=[pltpu.VMEM((tm, D), BF16)],
        compiler_params=_params("arbitrary", "arbitrary", "arbitrary"),
    )(x, g.reshape(1, D), shift.reshape(G, 1, D), scale.reshape(G, 1, D), w)


def _outproj_kernel(a_ref, w_ref, x_ref, gate_ref, o_ref):
    o_ref[0] = x_ref[0] + gate_ref[0] * jnp.dot(a_ref[0], w_ref[...], preferred_element_type=F32)


def _outproj(a, w, xres, gate):
    G, R, Kd = a.shape
    N = w.shape[1]
    tm, tn = 1024, 1024
    return pl.pallas_call(
        _outproj_kernel,
        grid=(G, R // tm, N // tn),
        in_specs=[pl.BlockSpec((1, tm, Kd), lambda gi, i, j: (gi, i, 0)),
                  pl.BlockSpec((Kd, tn), lambda gi, i, j: (0, j)),
                  pl.BlockSpec((1, tm, tn), lambda gi, i, j: (gi, i, j)),
                  pl.BlockSpec((1, 1, tn), lambda gi, i, j: (gi, 0, j))],
        out_specs=pl.BlockSpec((1, tm, tn), lambda gi, i, j: (gi, i, j)),
        out_shape=jax.ShapeDtypeStruct((G, R, N), F32),
        compiler_params=_params("arbitrary", "arbitrary", "arbitrary"),
    )(a, w, xres, gate.reshape(G, 1, N))


def _ffn_in_kernel(x_ref, g_ref, sh_ref, sc_ref, rw_ref, h_ref, lg_ref):
    h = _norm_mod(x_ref[0], g_ref[...], sh_ref[0], sc_ref[0])
    h_ref[0] = h.astype(BF16)
    lg_ref[0] = _dot(h, rw_ref[...], _NN, 6)


def _ffn_in(x, g, shift, scale, router_w_pad):
    G, R, D = x.shape
    tm = 512
    vec = pl.BlockSpec((1, 1, D), lambda gi, i: (gi, 0, 0))
    return pl.pallas_call(
        _ffn_in_kernel,
        grid=(G, R // tm),
        in_specs=[pl.BlockSpec((1, tm, D), lambda gi, i: (gi, i, 0)),
                  pl.BlockSpec((1, D), lambda gi, i: (0, 0)),
                  vec, vec,
                  pl.BlockSpec((D, LANE), lambda gi, i: (0, 0))],
        out_specs=[pl.BlockSpec((1, tm, D), lambda gi, i: (gi, i, 0)),
                   pl.BlockSpec((1, tm, LANE), lambda gi, i: (gi, i, 0))],
        out_shape=[jax.ShapeDtypeStruct((G, R, D), BF16), jax.ShapeDtypeStruct((G, R, LANE), F32)],
        compiler_params=_params("arbitrary", "arbitrary"),
    )(x, g.reshape(1, D), shift.reshape(G, 1, D), scale.reshape(G, 1, D), router_w_pad)


def _moe_kernel(te_ref, na_ref, x_ref, gw_ref, wg_ref, wu_ref, wd_ref, o_ref):
    i = pl.program_id(0)

    @pl.when(i < na_ref[0])
    def _():
        x = x_ref[...]
        hg = jnp.dot(x, wg_ref[0], preferred_element_type=F32)
        hu = jnp.dot(x, wu_ref[0], preferred_element_type=F32)
        he = (_silu(hg) * hu).astype(BF16)
        o_ref[...] = gw_ref[...] * jnp.dot(he, wd_ref[0], preferred_element_type=F32)

    @pl.when(i >= na_ref[0])
    def _():
        o_ref[...] = jnp.zeros_like(o_ref)


def _moe_experts(tile_expert, n_active, xs, gate_w, w_gate, w_up, w_down):
    P, D = xs.shape
    tm = MOE_TM
    grid_spec = pltpu.PrefetchScalarGridSpec(
        num_scalar_prefetch=2,
        grid=(P // tm,),
        in_specs=[pl.BlockSpec((tm, D), lambda i, te, na: (i, 0)),
                  pl.BlockSpec((tm, 1), lambda i, te, na: (i, 0)),
                  pl.BlockSpec((1, D, D_EXPERT), lambda i, te, na: (te[i], 0, 0)),
                  pl.BlockSpec((1, D, D_EXPERT), lambda i, te, na: (te[i], 0, 0)),
                  pl.BlockSpec((1, D_EXPERT, D), lambda i, te, na: (te[i], 0, 0))],
        out_specs=pl.BlockSpec((tm, D), lambda i, te, na: (i, 0)),
    )
    return pl.pallas_call(
        _moe_kernel,
        grid_spec=grid_spec,
        out_shape=jax.ShapeDtypeStruct((P, D), F32),
        compiler_params=_params("arbitrary"),
    )(tile_expert, n_active, xs, gate_w, w_gate, w_up, w_down)


def _route(logits, router_b):
    aff = jax.nn.sigmoid(logits)
    sel = (aff + router_b.astype(F32)).reshape(-1, MOE_GROUPS, EXPERTS_PER_GROUP)
    lane = lax.broadcasted_iota(jnp.int32, sel.shape, 2)

    def top2(x, ids):
        i1 = jnp.argmax(x, axis=-1)
        x2 = jnp.where(ids == i1[..., None], -jnp.inf, x)
        i2 = jnp.argmax(x2, axis=-1)
        return jnp.max(x, axis=-1), jnp.max(x2, axis=-1), i1, i2

    m1, m2, _, _ = top2(sel, lane)
    grp = jnp.argmax(m1 + m2, axis=-1)
    in_grp = jnp.take_along_axis(sel, grp[:, None, None], axis=1)[:, 0]
    _, _, l1, l2 = top2(in_grp, lane[:, 0])
    idx = grp[:, None] * EXPERTS_PER_GROUP + jnp.stack([l1, l2], axis=-1)
    wts = jnp.take_along_axis(aff, idx, axis=1)
    wts = wts / jnp.sum(wts, axis=-1, keepdims=True)
    return idx.astype(jnp.int32), wts


def _moe(h, logits, router_b, w_gate, w_up, w_down):
    M, D = h.shape
    tm = MOE_TM
    idx, wts = _route(logits, router_b)
    n_asg = M * MOE_TOP_K
    P = n_asg + N_EXPERTS * tm
    onehot = jax.nn.one_hot(idx.reshape(-1), N_EXPERTS, dtype=jnp.int32)
    csum = jnp.cumsum(onehot, axis=0)
    counts = csum[-1]
    pcounts = ((counts + tm - 1) // tm) * tm
    pends = jnp.cumsum(pcounts)
    pstarts = pends - pcounts
    dest = jnp.sum(onehot * (pstarts[None, :] + csum - 1), axis=1)
    rows_tok = jnp.zeros((P,), jnp.int32).at[dest].set(jnp.arange(n_asg, dtype=jnp.int32) // MOE_TOP_K)
    gate_w = jnp.zeros((P,), F32).at[dest].set(wts.reshape(-1))
    pos = dest.reshape(M, MOE_TOP_K)
    tile_start = jnp.arange(P // tm, dtype=jnp.int32) * tm
    tile_expert = jnp.minimum(jnp.searchsorted(pends, tile_start, side='right'), N_EXPERTS - 1).astype(jnp.int32)
    n_active = (pends[-1] // tm).astype(jnp.int32).reshape(1)
    xs = jnp.take(h, rows_tok, axis=0)
    y = _moe_experts(tile_expert, n_active, xs, gate_w.reshape(P, 1), w_gate, w_up, w_down)
    return jnp.take(y, pos[:, 0], axis=0) + jnp.take(y, pos[:, 1], axis=0)


def _final_norm_kernel(x_ref, g_ref, o_ref):
    x = x_ref[...]
    o_ref[...] = x * lax.rsqrt(jnp.mean(x * x, axis=-1, keepdims=True) + RMS_EPS) * g_ref[...]


def _final_norm(x, g):
    M, D = x.shape
    tm = 512
    return pl.pallas_call(
        _final_norm_kernel,
        grid=(M // tm,),
        in_specs=[pl.BlockSpec((tm, D), lambda i: (i, 0)), pl.BlockSpec((1, D), lambda i: (0, 0))],
        out_specs=pl.BlockSpec((tm, D), lambda i: (i, 0)),
        out_shape=jax.ShapeDtypeStruct((M, D), F32),
        compiler_params=_params("arbitrary"),
    )(x, g.reshape(1, D))


def _to_col_major(t, rows):
    b, n = t.shape[:2]
    return t.reshape(b, rows, GRID_W, -1).swapaxes(1, 2).reshape(b, n, -1)


def _from_col_major(t, rows):
    b, n = t.shape[:2]
    return t.reshape(b, GRID_W, rows, -1).swapaxes(1, 2).reshape(b, n, -1)


def _split_cols(u, sizes):
    out, start = [], 0
    for s in sizes:
        out.append(u[..., start:start + s])
        start += s
    return out


def _regroup_w_in(w):
    z, xbc, dt, q, k, v, g, glr, rw = _split_cols(w, IN_SPLITS)
    pad = jnp.zeros((w.shape[0], P_IN_PAD - sum(IN_SPLITS)), w.dtype)
    return jnp.concatenate([rw, dt, glr, pad, z, xbc, q, k, v, g], axis=1).astype(BF16)


def _lane_row(v, offset=0):
    v = v.reshape(-1)
    return jnp.zeros((1, LANE), F32).at[0, offset:offset + v.shape[0]].set(v)


def _rows_at(m, offset):
    return jnp.zeros((LANE, m.shape[1]), F32).at[offset:offset + m.shape[0]].set(m)


def _mixer_params(p, l):
    two = range(2)
    rw = dict(
        mu=jnp.pad(p['rwkv_mu'][l], ((0, 0), (0, COL_RW[0] - P_RWKV))),
        w0=p['rwkv_w0'][l].reshape(2, 1, D_RWKV),
        w2=jnp.stack([_rows_at(p['rwkv_w2'][l][d], d * RWKV_LW) for d in two]),
        a0=p['rwkv_a0'][l].reshape(2, 1, D_RWKV),
        a2=jnp.stack([_rows_at(p['rwkv_a2'][l][d], d * RWKV_LA) for d in two]),
        g2=p['rwkv_g2'][l],
        k_k=p['rwkv_k_k'][l].reshape(1, D_RWKV),
        k_a=p['rwkv_k_a'][l].reshape(1, D_RWKV),
        r_k=p['rwkv_r_k'][l].reshape(1, D_RWKV))
    return dict(
        conv_w=p['ssd_conv_w'][l].reshape(9, SSD_CONV_CH),
        conv_b=p['ssd_conv_b'][l].reshape(1, SSD_CONV_CH),
        dt_bias=_lane_row(p['ssd_dt_bias'][l]),
        ssd_a=_lane_row(-jnp.exp(p['ssd_a_log'][l])),
        gla_up=jnp.stack([_rows_at(p['gla_gate_up'][l][d], 2 * SSD_HEADS + d * GLA_LR) for d in two]),
        gla_b=p['gla_gate_b'][l].reshape(2, 1, GLA_QK),
        rwkv=rw,
        epi=dict(dx=jnp.repeat(p['ssd_d'][l], SSD_HEAD_DIM).reshape(1, D_SSD),
                 ssd_norm_g=p['ssd_norm_g'][l].reshape(1, D_SSD),
                 gla_norm_g=p['gla_norm_g'][l].reshape(1, GLA_DV),
                 ln_g=p['rwkv_ln_g'][l].reshape(1, D_RWKV),
                 ln_b=p['rwkv_ln_b'][l].reshape(1, D_RWKV)))


def _token_mixers(u, b0, B, T, s0, mp):
    s_ssd, s_gla, s_rwkv = s0
    st_ssd = s_ssd.transpose(0, 1, 4, 2, 3).reshape(B, 2, SSD_STATE, D_SSD)
    st_gla = s_gla.transpose(0, 1, 4, 2, 3).reshape(B, 2, GLA_DV, GLA_QK)
    yf, yb, f_ssd = _ssd_scan(u, b0, B, T, mp['dt_bias'], mp['ssd_a'], st_ssd)
    gf, gb, f_gla = _gla_scan(u, b0, B, T, mp['gla_up'], mp['gla_b'], st_gla)
    rf, rb, gate, bonus, f_rwkv = _rwkv_scan(u, b0, B, T, mp['rwkv'], s_rwkv)
    mix = _epilogue(u, b0, B, T, (yf, yb), (gf, gb), (rf, rb, gate, bonus), mp['epi'])
    f_ssd = f_ssd.reshape(B, 2, SSD_STATE, SSD_HEADS, SSD_HEAD_DIM).transpose(0, 1, 3, 4, 2)
    f_gla = f_gla.reshape(B, 2, GLA_DV, GLA_HEADS, GLA_DK).transpose(0, 1, 3, 4, 2)
    return mix, (f_ssd, f_gla, f_rwkv)


def kernel(x_prompt, x_sample, state_ssd, state_gla, state_rwkv, c, c_ctx, ada_w, ada_b, norm_mix_g, norm_ffn_g,
           w_in, ssd_conv_w, ssd_conv_b, ssd_dt_bias, ssd_a_log, ssd_d, ssd_norm_g, gla_gate_up, gla_gate_b,
           gla_norm_g, rwkv_mu, rwkv_w0, rwkv_w2, rwkv_a0, rwkv_a2, rwkv_g2, rwkv_k_k, rwkv_k_a, rwkv_r_k,
           rwkv_ln_g, rwkv_ln_b, w_out, router_w, router_b, moe_w_gate, moe_w_up, moe_w_down, final_norm_g):
    p = dict(ssd_conv_w=ssd_conv_w, ssd_conv_b=ssd_conv_b, ssd_dt_bias=ssd_dt_bias, ssd_a_log=ssd_a_log,
             ssd_d=ssd_d, ssd_norm_g=ssd_norm_g, gla_gate_up=gla_gate_up, gla_gate_b=gla_gate_b,
             gla_norm_g=gla_norm_g, rwkv_mu=rwkv_mu, rwkv_w0=rwkv_w0, rwkv_w2=rwkv_w2, rwkv_a0=rwkv_a0,
             rwkv_a2=rwkv_a2, rwkv_g2=rwkv_g2, rwkv_k_k=rwkv_k_k, rwkv_k_a=rwkv_k_a, rwkv_r_k=rwkv_r_k,
             rwkv_ln_g=rwkv_ln_g, rwkv_ln_b=rwkv_ln_b)
    D = D_MODEL
    nb, seq = x_prompt.shape[:2]
    db, dseq = x_sample.shape[:2]
    grp = nb * seq
    assert dseq == grp and seq % CHUNK == 0 and dseq == GRID_W * GRID_W
    G = 1 + db
    x = jnp.concatenate([x_prompt.reshape(1, grp, D), x_sample], axis=0)
    cvec = jnp.concatenate([c_ctx[None, :], c, jnp.zeros((8 - G, D), F32)], axis=0)
    mods = _ada(cvec, ada_w, ada_b)[:, :G]
    router_w_pad = jnp.pad(router_w, ((0, 0), (0, LANE - N_EXPERTS)))
    zero_states = (jnp.zeros((nb, 2, SSD_HEADS, SSD_HEAD_DIM, SSD_STATE), F32),
                   jnp.zeros((nb, 2, GLA_HEADS, GLA_DK, GLA_DV), F32),
                   jnp.zeros((nb, 2, RWKV_HEADS, RWKV_N, RWKV_N), F32))
    ctx_states = []
    for l in range(DEPTH):
        sh1, sc1, g1, sh2, sc2, g2 = jnp.split(mods[l], 6, axis=-1)
        mp = _mixer_params(p, l)
        u = _inproj(x, norm_mix_g[l], sh1, sc1, _regroup_w_in(w_in[l]))
        u = _conv_inplace(u, mp['conv_w'], mp['conv_b'], 0, 1, seq, 1)
        u = _conv_inplace(u, mp['conv_w'], mp['conv_b'], 1, db, GRID_W, dseq // GRID_W)
        mix_p, new = _token_mixers(u.reshape(G * nb, seq, P_IN_PAD), 0, nb, seq, zero_states, mp)
        s0 = (state_ssd[:, l], state_gla[:, l], state_rwkv[:, l])
        if l % 2 == 1:
            mix_s, _ = _token_mixers(_to_col_major(u[1:], dseq // GRID_W), 0, db, dseq, s0, mp)
            mix_s = _from_col_major(mix_s, dseq // GRID_W)
        else:
            mix_s, _ = _token_mixers(u, 1, db, dseq, s0, mp)
        ctx_states.append(new)
        mix = jnp.concatenate([mix_p.reshape(1, grp, D), mix_s], axis=0)
        x = _outproj(mix, w_out[l].astype(BF16), x, g1)
        h, logits = _ffn_in(x, norm_ffn_g[l], sh2, sc2, router_w_pad)
        moe = _moe(h.reshape(G * grp, D), logits.reshape(G * grp, LANE)[:, :N_EXPERTS], router_b,
                   moe_w_gate[l].astype(BF16), moe_w_up[l].astype(BF16), moe_w_down[l].astype(BF16))
        x = x + g2[:, None, :] * moe.reshape(G, grp, D)
    y = _final_norm(x.reshape(G * grp, D), final_norm_g).reshape(G, grp, D)
    y_prompt = y[0].reshape(nb, seq, D)
    y_sample = y[1:]
    new_ssd = jnp.stack([s[0] for s in ctx_states], axis=1)
    new_gla = jnp.stack([s[1] for s in ctx_states], axis=1)
    new_rwkv = jnp.stack([s[2] for s in ctx_states], axis=1)
    return (y_prompt, y_sample, new_ssd, new_gla, new_rwkv)
```

```python
import functools

import jax
import jax.numpy as jnp
from jax import lax
from jax.experimental import pallas as pl
from jax.experimental.pallas import tpu as pltpu

F32 = jnp.float32
BF16 = jnp.bfloat16

D_MODEL = 2048
DEPTH = 2
GRID_W = 64
SSD_HEAD_DIM = 64
D_SSD = 1024
SSD_HEADS = 16
SSD_GROUPS = 2
SSD_STATE = 128
SSD_BC = SSD_GROUPS * SSD_STATE
SSD_CONV_CH = D_SSD + 2 * SSD_BC
GLA_HEADS = 4
D_GLA = 512
GLA_DV = 128
GLA_DK = 64
GLA_QK = GLA_HEADS * GLA_DK
GLA_LR = 16
GLA_GATE_NORM = 16.0
RWKV_N = 64
D_RWKV = 512
RWKV_HEADS = 8
RWKV_LW = 64
RWKV_LA = 64
RWKV_LG = 128
P_RWKV = 3 * D_RWKV + 2 * RWKV_LW + 2 * RWKV_LA + RWKV_LG
IN_SPLITS = (D_SSD, SSD_CONV_CH, 2 * SSD_HEADS, GLA_QK, GLA_QK, D_GLA, D_GLA, 2 * GLA_LR, P_RWKV)
N_EXPERTS = 16
MOE_GROUPS = 4
EXPERTS_PER_GROUP = 4
MOE_TOP_K = 2
D_EXPERT = 1024
RMS_EPS = 1e-6
RWKV_LN_EPS = 64e-5

LANE = 128
CHUNK = 64
RWKV_INV_BASE = 8
MOE_TM = 256
VMEM_LIMIT = 56 * 1024 * 1024

P_IN_PAD = 6144
COL_RW = (2048, 0)
COL_DT = (LANE, 15)
COL_Z = (1024, 2)
COL_X = (1024, 3)
COL_BC = (512, 8)
COL_QK = (512, 9)
COL_V = (512, 10)
COL_G = (512, 11)
CONV_TILE = 256
CONV_FIRST_TILE = 12


def _dot(a, b, contract, passes):
    dn = (contract, ((), ()))
    if passes == 6:
        return lax.dot_general(a, b, dn, precision=lax.Precision.HIGHEST, preferred_element_type=F32)
    ah = a.astype(BF16)
    bh = b.astype(BF16)
    out = lax.dot_general(ah, bh, dn, preferred_element_type=F32)
    if passes == 3:
        al = (a - ah.astype(F32)).astype(BF16)
        bl = (b - bh.astype(F32)).astype(BF16)
        out = out + lax.dot_general(ah, bl, dn, preferred_element_type=F32)
        out = out + lax.dot_general(al, bh, dn, preferred_element_type=F32)
    return out


_NN = ((1,), (0,))
_NT = ((1,), (1,))
_TN = ((0,), (0,))


def _params(*sem):
    return pltpu.CompilerParams(dimension_semantics=sem, vmem_limit_bytes=VMEM_LIMIT)


def _split3(x):
    hi = x.astype(BF16)
    r1 = x - hi.astype(F32)
    mid = r1.astype(BF16)
    lo = (r1 - mid.astype(F32)).astype(BF16)
    return hi, mid, lo


def _sel_dot(sel, x, contract=_NN):
    dn = (contract, ((), ()))
    s = sel.astype(BF16)
    hi, mid, lo = _split3(x)
    out = lax.dot_general(s, hi, dn, preferred_element_type=F32)
    out = out + lax.dot_general(s, mid, dn, preferred_element_type=F32)
    return out + lax.dot_general(s, lo, dn, preferred_element_type=F32)


def _dot_sel(x, sel):
    dn = (_NN, ((), ()))
    s = sel.astype(BF16)
    hi, mid, lo = _split3(x)
    out = lax.dot_general(hi, s, dn, preferred_element_type=F32)
    out = out + lax.dot_general(mid, s, dn, preferred_element_type=F32)
    return out + lax.dot_general(lo, s, dn, preferred_element_type=F32)


def _bdot(a, b, contract=_NN):
    return lax.dot_general(a.astype(BF16), b.astype(BF16), (contract, ((), ())), preferred_element_type=F32)


def _silu(x):
    return x * jax.nn.sigmoid(x)


def _softplus(x):
    return jnp.maximum(x, 0.0) + jnp.log(1.0 + jnp.exp(-jnp.abs(x)))


def _log_sigmoid(x):
    return jnp.minimum(x, 0.0) - jnp.log(1.0 + jnp.exp(-jnp.abs(x)))


def _time_masks(d):
    row = lax.broadcasted_iota(jnp.int32, (CHUNK, CHUNK), 0)
    col = lax.broadcasted_iota(jnp.int32, (CHUNK, CHUNK), 1)
    if d == 0:
        return col <= row, col < row
    return col >= row, col > row


def _seq_specs(width_idx, b0, nc, rows=CHUNK):
    w, j = width_idx
    return [pl.BlockSpec((1, rows, w), lambda b, c: (b + b0, c, j)),
            pl.BlockSpec((1, rows, w), lambda b, c: (b + b0, nc - 1 - c, j))]


def _conv_kernel(x_ref, w_ref, b_ref, o_ref, *, width, rows):
    x = x_ref[0]
    T = x.shape[0]
    t = lax.broadcasted_iota(jnp.int32, (T, 1), 0)
    col = t % width
    row = (t // width) % rows
    acc = jnp.zeros_like(x) + b_ref[...]
    for dr in (-1, 0, 1):
        if rows == 1 and dr != 0:
            continue
        for dc in (-1, 0, 1):
            off = dr * width + dc
            xs = x if off == 0 else pltpu.roll(x, (-off) % T, axis=0)
            valid = (col + dc >= 0) & (col + dc < width) & (row + dr >= 0) & (row + dr < rows)
            tap = (dr + 1) * 3 + (dc + 1)
            acc = acc + jnp.where(valid, xs, 0.0) * w_ref[tap:tap + 1, :]
    o_ref[0] = _silu(acc)


def _conv_inplace(u, w9, bias, g0, ng, width, rows):
    G, T, P = u.shape
    nt = SSD_CONV_CH // CONV_TILE
    blk = pl.BlockSpec((1, T, CONV_TILE), lambda g, j: (g + g0, 0, CONV_FIRST_TILE + j))
    return pl.pallas_call(
        functools.partial(_conv_kernel, width=width, rows=rows),
        grid=(ng, nt),
        in_specs=[blk,
                  pl.BlockSpec((9, CONV_TILE), lambda g, j: (0, j)),
                  pl.BlockSpec((1, CONV_TILE), lambda g, j: (0, j))],
        out_specs=blk,
        out_shape=jax.ShapeDtypeStruct(u.shape, u.dtype),
        input_output_aliases={0: 0},
        compiler_params=_params("arbitrary", "arbitrary"),
    )(u, w9, bias)


def _ssd_kernel(xf_ref, bcf_ref, dtf_ref, xb_ref, bcb_ref, dtb_ref, bias_ref, a_ref, s0_ref,
                yf_ref, yb_ref, sfin_ref, st_scr):
    c = pl.program_id(1)
    L = CHUNK
    HP = D_SSD

    @pl.when(c == 0)
    def _():
        st_scr[...] = s0_ref[0]

    rowx = lax.broadcasted_iota(jnp.int32, (L, HP), 0)
    s_of_lane = lax.broadcasted_iota(jnp.int32, (L, HP), 1) % SSD_HEAD_DIM
    e_row = lax.broadcasted_iota(jnp.int32, (LANE, HP), 0)
    e_head = lax.broadcasted_iota(jnp.int32, (LANE, HP), 1) // SSD_HEAD_DIM
    lane2 = lax.broadcasted_iota(jnp.int32, (L, LANE), 1)
    for d, (x_ref, bc_ref, dt_ref, y_ref) in enumerate(((xf_ref, bcf_ref, dtf_ref, yf_ref),
                                                        (xb_ref, bcb_ref, dtb_ref, yb_ref))):
        incl, _ = _time_masks(d)
        strict_t = (s_of_lane < rowx) if d == 0 else (s_of_lane > rowx)
        incl_t = (s_of_lane <= rowx) if d == 0 else (s_of_lane >= rowx)
        expand = e_row == d * SSD_HEADS + e_head
        dtv = _softplus(dt_ref[0] + bias_ref[...])
        dtx = _dot_sel(dtv, expand)
        dax = _dot_sel(dtv * a_ref[...], expand)
        cumx = _sel_dot(incl, dax)
        totx = jnp.sum(dax, axis=0, keepdims=True)
        segx = _sel_dot(incl, jnp.where(strict_t, dax, 0.0))
        decx = jnp.where(incl_t, jnp.exp(jnp.minimum(segx, 0.0)), 0.0)
        x = x_ref[0]
        xdt = x * dtx
        xend = xdt * jnp.exp(totx - cumx)
        ecum = jnp.exp(cumx)
        etot = jnp.exp(totx)
        bc = bc_ref[0]
        for g in range(SSD_GROUPS):
            bm = bc[:, g * SSD_STATE:(g + 1) * SSD_STATE]
            cm = bc[:, SSD_BC + g * SSD_STATE:SSD_BC + (g + 1) * SSD_STATE]
            sc = _bdot(cm, bm, _NT)
            sc2 = jnp.concatenate([sc, sc], axis=1)
            for i in range(SSD_HEADS // SSD_GROUPS // 2):
                ls = slice((g * 4 + i) * LANE, (g * 4 + i + 1) * LANE)
                xp = xdt[:, ls]
                xbd = jnp.concatenate([jnp.where(lane2 < SSD_HEAD_DIM, xp, 0.0),
                                       jnp.where(lane2 >= SSD_HEAD_DIM, xp, 0.0)], axis=0)
                st = st_scr[d, :, ls]
                y_ref[0, :, ls] = _bdot(decx[:, ls] * sc2, xbd) + _bdot(cm, st) * ecum[:, ls]
                st_scr[d, :, ls] = st * etot[:, ls] + _bdot(bm, xend[:, ls], _TN)

    @pl.when(c == pl.num_programs(1) - 1)
    def _():
        sfin_ref[0] = st_scr[...]


def _ssd_scan(u, b0, B, T, bias128, a128, s0):
    nc = T // CHUNK
    st = pl.BlockSpec((1, 2, SSD_STATE, D_SSD), lambda b, c: (b, 0, 0, 0))
    vec = pl.BlockSpec((1, LANE), lambda b, c: (0, 0))
    xs, bcs, dts = _seq_specs(COL_X, b0, nc), _seq_specs(COL_BC, b0, nc), _seq_specs(COL_DT, b0, nc)
    ys = _seq_specs((D_SSD, 0), 0, nc)
    return pl.pallas_call(
        _ssd_kernel,
        grid=(B, nc),
        in_specs=[xs[0], bcs[0], dts[0], xs[1], bcs[1], dts[1], vec, vec, st],
        out_specs=[ys[0], ys[1], st],
        out_shape=[jax.ShapeDtypeStruct((B, T, D_SSD), F32), jax.ShapeDtypeStruct((B, T, D_SSD), F32),
                   jax.ShapeDtypeStruct((B, 2, SSD_STATE, D_SSD), F32)],
        scratch_shapes=[pltpu.VMEM((2, SSD_STATE, D_SSD), F32)],
        compiler_params=_params("arbitrary", "arbitrary"),
    )(u, u, u, u, u, u, bias128, a128, s0)


def _stack_heads(x, head_of_lane):
    return jnp.concatenate([jnp.where(head_of_lane == h, x, 0.0) for h in range(GLA_HEADS)], axis=0)


def _gla_kernel(qkf_ref, vf_ref, dtf_ref, qkb_ref, vb_ref, dtb_ref, wup_ref, gb_ref, s0_ref,
                of_ref, ob_ref, sfin_ref, st_scr):
    c = pl.program_id(1)
    L = CHUNK

    @pl.when(c == 0)
    def _():
        st_scr[...] = s0_ref[0]

    row = lax.broadcasted_iota(jnp.int32, (L, L), 0)
    col = lax.broadcasted_iota(jnp.int32, (L, L), 1)
    row1 = lax.broadcasted_iota(jnp.int32, (L, 1), 0)
    head_of_lane = lax.broadcasted_iota(jnp.int32, (L, GLA_QK), 1) // GLA_DK
    row4 = lax.broadcasted_iota(jnp.int32, (GLA_HEADS * L, L), 0) % L
    col4 = lax.broadcasted_iota(jnp.int32, (GLA_HEADS * L, L), 1)
    for d, (qk_ref, v_ref, dt_ref, o_ref) in enumerate(((qkf_ref, vf_ref, dtf_ref, of_ref),
                                                        (qkb_ref, vb_ref, dtb_ref, ob_ref))):
        incl, _ = _time_masks(d)
        flip = (lambda i: i) if d == 0 else (lambda i: L - 1 - i)
        tq, tj, tq1, tq4, tj4 = flip(row), flip(col), flip(row1), flip(row4), flip(col4)
        qk = qk_ref[0]
        q = qk[:, :GLA_QK] * GLA_DK ** -0.5
        k = qk[:, GLA_QK:]
        v = v_ref[0]
        lg = _log_sigmoid(_dot(dt_ref[0], wup_ref[d], _NN, 3) + gb_ref[d]) / GLA_GATE_NORM
        cum = _sel_dot(incl, lg)
        tot = jnp.sum(lg, axis=0, keepdims=True)
        acc = jnp.where(row4 == col4, _bdot(_stack_heads(q, head_of_lane), k, _NT), 0.0)
        halves = [L >> (i + 1) for i in range(L.bit_length() - 1)]
        pick_mid = [(tj == (tq // (2 * m)) * (2 * m) + m - 1).astype(BF16) for m in halves]
        cmid_all = _sel_dot(jnp.concatenate(pick_mid, axis=0), cum)
        for i, m in enumerate(halves):
            blk = 2 * m
            cmid = cmid_all[i * L:(i + 1) * L]
            later = (tq1 % blk) >= m
            qs = jnp.where(later, q * jnp.exp(jnp.minimum(cum - cmid, 0.0)), 0.0)
            ks = jnp.where(later, 0.0, k * jnp.exp(jnp.minimum(cmid - cum, 0.0)))
            a_lvl = _bdot(_stack_heads(qs, head_of_lane), ks, _NT)
            acc = acc + jnp.where((tq4 // blk) == (tj4 // blk), a_lvl, 0.0)
        st = st_scr[d]
        o_inter = _bdot(_stack_heads(q * jnp.exp(cum), head_of_lane), st, _NT)
        for h in range(GLA_HEADS):
            rs = slice(h * L, (h + 1) * L)
            ls = slice(h * GLA_DV, (h + 1) * GLA_DV)
            o_ref[0, :, ls] = _bdot(acc[rs], v[:, ls]) + o_inter[rs]
        vst = jnp.concatenate([v[:, h * GLA_DV:(h + 1) * GLA_DV] for h in range(GLA_HEADS)], axis=0)
        kst = _stack_heads(k * jnp.exp(tot - cum), head_of_lane)
        st_scr[d] = st * jnp.exp(tot) + _bdot(vst, kst, _TN)

    @pl.when(c == pl.num_programs(1) - 1)
    def _():
        sfin_ref[0] = st_scr[...]


def _gla_scan(u, b0, B, T, wup, gb, s0):
    nc = T // CHUNK
    st = pl.BlockSpec((1, 2, GLA_DV, GLA_QK), lambda b, c: (b, 0, 0, 0))
    qks, vs, dts = _seq_specs(COL_QK, b0, nc), _seq_specs(COL_V, b0, nc), _seq_specs(COL_DT, b0, nc)
    os_ = _seq_specs((D_GLA, 0), 0, nc)
    return pl.pallas_call(
        _gla_kernel,
        grid=(B, nc),
        in_specs=[qks[0], vs[0], dts[0], qks[1], vs[1], dts[1],
                  pl.BlockSpec((2, LANE, GLA_QK), lambda b, c: (0, 0, 0)),
                  pl.BlockSpec((2, 1, GLA_QK), lambda b, c: (0, 0, 0)), st],
        out_specs=[os_[0], os_[1], st],
        out_shape=[jax.ShapeDtypeStruct((B, T, D_GLA), F32), jax.ShapeDtypeStruct((B, T, D_GLA), F32),
                   jax.ShapeDtypeStruct((B, 2, GLA_DV, GLA_QK), F32)],
        scratch_shapes=[pltpu.VMEM((2, GLA_DV, GLA_QK), F32)],
        compiler_params=_params("arbitrary", "arbitrary"),
    )(u, u, u, u, u, u, wup, gb, s0)


def _rwkv_kernel(xf_ref, hpf_ref, hnf_ref, xb_ref, hpb_ref, hnb_ref, mu_ref, w0_ref, w2_ref, a0_ref, a2_ref,
                 g2_ref, kkw_ref, ka_ref, rk_ref, s0_ref, of_ref, ob_ref, gate_ref, bonus_ref, sfin_ref, s_scr):
    c = pl.program_id(1)
    nc = pl.num_programs(1)
    C = CHUNK
    N = RWKV_N
    D = D_RWKV

    @pl.when(c == 0)
    def _():
        s_scr[...] = s0_ref[0]

    row = lax.broadcasted_iota(jnp.int32, (C, C), 0)
    col = lax.broadcasted_iota(jnp.int32, (C, C), 1)
    row1 = lax.broadcasted_iota(jnp.int32, (C, 1), 0)
    eye = (col == row).astype(F32)
    same_head = (lax.broadcasted_iota(jnp.int32, (D, D), 0) // N) == (lax.broadcasted_iota(jnp.int32, (D, D), 1) // N)

    def a_icl(ad, d):
        return jax.nn.sigmoid(a0_ref[d] + _dot(ad, a2_ref[d], _NN, 3))

    def k_eff(kr, a):
        return kr * (1.0 + (a - 1.0) * ka_ref[...])

    chains = []
    for d, (x_ref, hp_ref, hn_ref, o_ref) in enumerate(((xf_ref, hpf_ref, hnf_ref, of_ref),
                                                        (xb_ref, hpb_ref, hnb_ref, ob_ref))):
        cc = c if d == 0 else nc - 1 - c
        incl, strict = _time_masks(d)
        x = x_ref[0]
        prev_row = jnp.where(cc > 0, hp_ref[0][7:8, :], 0.0)
        next_row = jnp.where(cc < nc - 1, hn_ref[0][0:1, :], 0.0)
        prev = jnp.where(row1 == 0, prev_row, pltpu.roll(x, 1, axis=0))
        nxt = jnp.where(row1 == C - 1, next_row, pltpu.roll(x, C - 1, axis=0))
        xm = x + mu_ref[0:1, :] * (prev - x) + mu_ref[1:2, :] * (nxt - x)
        r = xm[:, 0:D]
        kr = xm[:, D:2 * D]
        v = xm[:, 2 * D:3 * D]
        wd = xm[:, 3 * D:3 * D + LANE]
        ad = xm[:, 3 * D + LANE:3 * D + 2 * LANE]
        w_log = _log_sigmoid(w0_ref[d] + _dot(jnp.tanh(wd), w2_ref[d], _NN, 3)) - 0.5
        lw = -jnp.exp(w_log)
        a = a_icl(ad, d)
        kk = kr * kkw_ref[...]
        kk = kk * lax.rsqrt(_dot_sel(kk * kk, same_head) + 1e-12)
        k = k_eff(kr, a)
        if d == 0:
            gd = xm[:, 3 * D + 2 * LANE:3 * D + 3 * LANE]
            gate_ref[0] = _dot(jax.nn.sigmoid(gd), g2_ref[...], _NN, 3)
            k_both = k + k_eff(kr, a_icl(ad, 1))
            bonus_ref[0] = _dot_sel(r * k_both * rk_ref[...], same_head) * v

        cum = _sel_dot(incl, lw)
        tot = jnp.sum(lw, axis=0, keepdims=True)
        b = a * kk
        kp = kk * jnp.exp(cum - lw)
        rp = r * jnp.exp(cum)
        pinv = jnp.exp(-cum)
        kinv = k * pinv
        binv = b * pinv
        pend = jnp.exp(tot - cum)
        kd = k * pend
        bd = b * pend
        ptot = jnp.exp(tot)
        for h in range(RWKV_HEADS):
            sl = slice(h * N, (h + 1) * N)
            chains.append(dict(
                d=d, h=h, sl=sl, incl=incl, strict=strict, o_ref=o_ref, v=v[:, sl], kd=kd[:, sl], bd=bd[:, sl],
                ptot=ptot[:, sl],
                x=jnp.concatenate([kp[:, sl], rp[:, sl]], axis=0).astype(BF16),
                y=jnp.concatenate([kinv[:, sl], binv[:, sl]], axis=0).astype(BF16)))

    for ch in chains:
        aa = _bdot(ch['x'], ch['y'], _NT)
        ch['a_kk'] = jnp.where(ch['strict'], aa[:C, :C], 0.0)
        ch['L'] = jnp.where(ch['strict'], aa[:C, C:], 0.0)
        ch['a_rk'] = jnp.where(ch['incl'], aa[C:, :C], 0.0)
        ch['a_rb'] = jnp.where(ch['incl'], aa[C:, C:], 0.0)
    def same_block(m):
        return (row // m) == (col // m)

    for ch in chains:
        ch['pw'] = jnp.where(same_block(RWKV_INV_BASE), ch['L'], 0.0)
        ch['t'] = eye - ch['pw']
    n = 2
    while n < RWKV_INV_BASE:
        for ch in chains:
            ch['pw'] = _bdot(ch['pw'], ch['pw'])
        for ch in chains:
            ch['t'] = ch['t'] + _bdot(ch['t'], ch['pw'])
        n *= 2
    m = RWKV_INV_BASE
    while m < C:
        off = same_block(2 * m) & jnp.logical_not(same_block(m))
        for ch in chains:
            ch['pw'] = _bdot(ch['t'], jnp.where(off, ch['L'], 0.0))
        for ch in chains:
            ch['t'] = ch['t'] - _bdot(ch['pw'], ch['t'])
        m *= 2
    for ch in chains:
        ch['s0'] = s_scr[ch['d'], ch['h']]
        ch['xs'] = _bdot(ch['x'], ch['s0'], _NT)
        ch['g'] = ch['xs'][:C] + _bdot(ch['a_kk'], ch['v'])
    for ch in chains:
        ch['u'] = _bdot(ch['t'], ch['g'])
    for ch in chains:
        o = ch['xs'][C:] + _bdot(ch['a_rk'], ch['v']) - _bdot(ch['a_rb'], ch['u'])
        ch['o_ref'][0, :, ch['sl']] = o
        s_scr[ch['d'], ch['h']] = (ch['s0'] * ch['ptot'] + _bdot(ch['v'], ch['kd'], _TN)
                                   - _bdot(ch['u'], ch['bd'], _TN))

    @pl.when(c == nc - 1)
    def _():
        sfin_ref[0] = s_scr[...]


def _rwkv_scan(u, b0, B, T, prm, s0):
    nc = T // CHUNK
    nb8 = T // 8
    w, j = COL_RW
    xs = _seq_specs(COL_RW, b0, nc)
    halo_prev = [pl.BlockSpec((1, 8, w), lambda b, c: (b + b0, jnp.maximum(c * 8 - 1, 0), j)),
                 pl.BlockSpec((1, 8, w), lambda b, c: (b + b0, jnp.maximum((nc - 1 - c) * 8 - 1, 0), j))]
    halo_next = [pl.BlockSpec((1, 8, w), lambda b, c: (b + b0, jnp.minimum((c + 1) * 8, nb8 - 1), j)),
                 pl.BlockSpec((1, 8, w), lambda b, c: (b + b0, jnp.minimum((nc - c) * 8, nb8 - 1), j))]
    st = pl.BlockSpec((1, 2, RWKV_HEADS, RWKV_N, RWKV_N), lambda b, c: (b, 0, 0, 0, 0))
    os_ = _seq_specs((D_RWKV, 0), 0, nc)

    def full(a):
        return pl.BlockSpec(a.shape, lambda b, c, _n=a.ndim: (0,) * _n)

    names = ('mu', 'w0', 'w2', 'a0', 'a2', 'g2', 'k_k', 'k_a', 'r_k')
    tok = jax.ShapeDtypeStruct((B, T, D_RWKV), F32)
    return pl.pallas_call(
        _rwkv_kernel,
        grid=(B, nc),
        in_specs=[xs[0], halo_prev[0], halo_next[0], xs[1], halo_prev[1], halo_next[1]]
                 + [full(prm[n]) for n in names] + [st],
        out_specs=[os_[0], os_[1], os_[0], os_[0], st],
        out_shape=[tok, tok, tok, tok, jax.ShapeDtypeStruct((B, 2, RWKV_HEADS, RWKV_N, RWKV_N), F32)],
        scratch_shapes=[pltpu.VMEM((2, RWKV_HEADS, RWKV_N, RWKV_N), F32)],
        compiler_params=_params("arbitrary", "arbitrary"),
    )(u, u, u, u, u, u, *[prm[n] for n in names], s0)


def _epilogue_kernel(yf_ref, yb_ref, z_ref, x_ref, gf_ref, gb_ref, g_ref, rf_ref, rb_ref, gate_ref, bonus_ref,
                     dx_ref, sng_ref, gng_ref, lng_ref, lnb_ref, o_ref):
    y = (yf_ref[0] + yb_ref[0] + x_ref[0] * dx_ref[...]) * _silu(z_ref[0])
    half = D_SSD // SSD_GROUPS
    for g in range(SSD_GROUPS):
        ls = slice(g * half, (g + 1) * half)
        yy = y[:, ls]
        yy = yy * lax.rsqrt(jnp.mean(yy * yy, axis=-1, keepdims=True) + RMS_EPS)
        o_ref[0, :, ls] = (yy * sng_ref[:, ls]).astype(BF16)
    go = gf_ref[0] + gb_ref[0]
    gg = g_ref[0]
    for h in range(GLA_HEADS):
        ls = slice(h * GLA_DV, (h + 1) * GLA_DV)
        oo = go[:, ls]
        oo = oo * lax.rsqrt(jnp.mean(oo * oo, axis=-1, keepdims=True) + RMS_EPS)
        o_ref[0, :, D_SSD + h * GLA_DV:D_SSD + (h + 1) * GLA_DV] = (oo * gng_ref[...] * _silu(gg[:, ls])).astype(BF16)
    ro = rf_ref[0] + rb_ref[0]
    same_head = ((lax.broadcasted_iota(jnp.int32, (D_RWKV, D_RWKV), 0) // RWKV_N)
                 == (lax.broadcasted_iota(jnp.int32, (D_RWKV, D_RWKV), 1) // RWKV_N))
    mean = _dot_sel(ro, same_head) * (1.0 / RWKV_N)
    cen = ro - mean
    var = _dot_sel(cen * cen, same_head) * (1.0 / RWKV_N)
    o = cen * lax.rsqrt(var + RWKV_LN_EPS) * lng_ref[...] + lnb_ref[...]
    o_ref[0, :, D_SSD + D_GLA:] = ((o + bonus_ref[0]) * gate_ref[0]).astype(BF16)


def _epilogue(u, b0, B, T, ssd_y, gla_o, rwkv_o, prm):
    tm = min(T, 512)

    def tok(width_idx, off):
        w, j = width_idx
        return pl.BlockSpec((1, tm, w), lambda b, i: (b + off, i, j))

    def vec(a):
        return pl.BlockSpec(a.shape, lambda b, i: (0, 0))

    y1, g1, r1 = tok((D_SSD, 0), 0), tok((D_GLA, 0), 0), tok((D_RWKV, 0), 0)
    names = ('dx', 'ssd_norm_g', 'gla_norm_g', 'ln_g', 'ln_b')
    return pl.pallas_call(
        _epilogue_kernel,
        grid=(B, T // tm),
        in_specs=[y1, y1, tok(COL_Z, b0), tok(COL_X, b0), g1, g1, tok(COL_G, b0), r1, r1, r1, r1]
                 + [vec(prm[n]) for n in names],
        out_specs=tok((D_MODEL, 0), 0),
        out_shape=jax.ShapeDtypeStruct((B, T, D_MODEL), BF16),
        compiler_params=_params("arbitrary", "arbitrary"),
    )(ssd_y[0], ssd_y[1], u, u, gla_o[0], gla_o[1], u, rwkv_o[0], rwkv_o[1], rwkv_o[2], rwkv_o[3],
      *[prm[n] for n in names])


def _ada_kernel(c_ref, w_ref, b_ref, o_ref):
    o_ref[0] = _dot(_silu(c_ref[...]), w_ref[0], _NN, 1) + b_ref[0]


def _ada(cvec8, ada_w, ada_b):
    tn = 1536
    n = ada_w.shape[-1]
    return pl.pallas_call(
        _ada_kernel,
        grid=(DEPTH, n // tn),
        in_specs=[pl.BlockSpec((8, D_MODEL), lambda l, j: (0, 0)),
                  pl.BlockSpec((1, D_MODEL, tn), lambda l, j: (l, 0, j)),
                  pl.BlockSpec((1, 1, tn), lambda l, j: (l, 0, j))],
        out_specs=pl.BlockSpec((1, 8, tn), lambda l, j: (l, 0, j)),
        out_shape=jax.ShapeDtypeStruct((DEPTH, 8, n), F32),
        compiler_params=_params("arbitrary", "arbitrary"),
    )(cvec8, ada_w, ada_b.reshape(DEPTH, 1, n))


def _norm_mod(x, g, shift, scale):
    y = x * lax.rsqrt(jnp.mean(x * x, axis=-1, keepdims=True) + RMS_EPS)
    return (y * g) * (1.0 + scale) + shift


def _inproj_kernel(x_ref, g_ref, sh_ref, sc_ref, w_ref, o_ref, h_scr):
    @pl.when(pl.program_id(2) == 0)
    def _():
        h_scr[...] = _norm_mod(x_ref[0], g_ref[...], sh_ref[0], sc_ref[0]).astype(BF16)

    o_ref[0] = jnp.dot(h_scr[...], w_ref[...], preferred_element_type=F32)


def _inproj(x, g, shift, scale, w):
    G, R, D = x.shape
    N = w.shape[1]
    tm, tn = 1024, 1024
    vec = pl.BlockSpec((1, 1, D), lambda gi, i, j: (gi, 0, 0))
    return pl.pallas_call(
        _inproj_kernel,
        grid=(G, R // tm, N // tn),
        in_specs=[pl.BlockSpec((1, tm, D), lambda gi, i, j: (gi, i, 0)),
                  pl.BlockSpec((1, D), lambda gi, i, j: (0, 0)),
                  vec, vec,
                  pl.BlockSpec((D, tn), lambda gi, i, j: (0, j))],
        out_specs=pl.BlockSpec((1, tm, tn), lambda gi, i, j: (gi, i, j)),
        out_shape=jax.ShapeDtypeStruct((G, R, N), F32),
        scratch_shapes=[pltpu.VMEM((tm, D), BF16)],
        compiler_params=_params("arbitrary", "arbitrary", "arbitrary"),
    )(x, g.reshape(1, D), shift.reshape(G, 1, D), scale.reshape(G, 1, D), w)


def _outproj_kernel(a_ref, w_ref, x_ref, gate_ref, o_ref):
    o_ref[0] = x_ref[0] + gate_ref[0] * jnp.dot(a_ref[0], w_ref[...], preferred_element_type=F32)


def _outproj(a, w, xres, gate):
    G, R, Kd = a.shape
    N = w.shape[1]
    tm, tn = 1024, 1024
    return pl.pallas_call(
        _outproj_kernel,
        grid=(G, R // tm, N // tn),
        in_specs=[pl.BlockSpec((1, tm, Kd), lambda gi, i, j: (gi, i, 0)),
                  pl.BlockSpec((Kd, tn), lambda gi, i, j: (0, j)),
                  pl.BlockSpec((1, tm, tn), lambda gi, i, j: (gi, i, j)),
                  pl.BlockSpec((1, 1, tn), lambda gi, i, j: (gi, 0, j))],
        out_specs=pl.BlockSpec((1, tm, tn), lambda gi, i, j: (gi, i, j)),
        out_shape=jax.ShapeDtypeStruct((G, R, N), F32),
        compiler_params=_params("arbitrary", "arbitrary", "arbitrary"),
    )(a, w, xres, gate.reshape(G, 1, N))


def _ffn_in_kernel(x_ref, g_ref, sh_ref, sc_ref, rw_ref, h_ref, lg_ref):
    h = _norm_mod(x_ref[0], g_ref[...], sh_ref[0], sc_ref[0])
    h_ref[0] = h
    lg_ref[0] = _dot(h, rw_ref[...], _NN, 6)


def _ffn_in(x, g, shift, scale, router_w_pad):
    G, R, D = x.shape
    tm = 512
    vec = pl.BlockSpec((1, 1, D), lambda gi, i: (gi, 0, 0))
    return pl.pallas_call(
        _ffn_in_kernel,
        grid=(G, R // tm),
        in_specs=[pl.BlockSpec((1, tm, D), lambda gi, i: (gi, i, 0)),
                  pl.BlockSpec((1, D), lambda gi, i: (0, 0)),
                  vec, vec,
                  pl.BlockSpec((D, LANE), lambda gi, i: (0, 0))],
        out_specs=[pl.BlockSpec((1, tm, D), lambda gi, i: (gi, i, 0)),
                   pl.BlockSpec((1, tm, LANE), lambda gi, i: (gi, i, 0))],
        out_shape=[jax.ShapeDtypeStruct((G, R, D), F32), jax.ShapeDtypeStruct((G, R, LANE), F32)],
        compiler_params=_params("arbitrary", "arbitrary"),
    )(x, g.reshape(1, D), shift.reshape(G, 1, D), scale.reshape(G, 1, D), router_w_pad)


def _moe_kernel(te_ref, na_ref, x_ref, gw_ref, wg_ref, wu_ref, wd_ref, o_ref):
    i = pl.program_id(0)

    @pl.when(i < na_ref[0])
    def _():
        x = x_ref[...].astype(BF16)
        hg = jnp.dot(x, wg_ref[0], preferred_element_type=F32)
        hu = jnp.dot(x, wu_ref[0], preferred_element_type=F32)
        he = (_silu(hg) * hu).astype(BF16)
        o_ref[...] = gw_ref[...] * jnp.dot(he, wd_ref[0], preferred_element_type=F32)

    @pl.when(i >= na_ref[0])
    def _():
        o_ref[...] = jnp.zeros_like(o_ref)


def _moe_experts(tile_expert, n_active, xs, gate_w, w_gate, w_up, w_down):
    P, D = xs.shape
    tm = MOE_TM
    grid_spec = pltpu.PrefetchScalarGridSpec(
        num_scalar_prefetch=2,
        grid=(P // tm,),
        in_specs=[pl.BlockSpec((tm, D), lambda i, te, na: (i, 0)),
                  pl.BlockSpec((tm, 1), lambda i, te, na: (i, 0)),
                  pl.BlockSpec((1, D, D_EXPERT), lambda i, te, na: (te[i], 0, 0)),
                  pl.BlockSpec((1, D, D_EXPERT), lambda i, te, na: (te[i], 0, 0)),
                  pl.BlockSpec((1, D_EXPERT, D), lambda i, te, na: (te[i], 0, 0))],
        out_specs=pl.BlockSpec((tm, D), lambda i, te, na: (i, 0)),
    )
    return pl.pallas_call(
        _moe_kernel,
        grid_spec=grid_spec,
        out_shape=jax.ShapeDtypeStruct((P, D), F32),
        compiler_params=_params("arbitrary"),
    )(tile_expert, n_active, xs, gate_w, w_gate, w_up, w_down)


def _route(logits, router_b):
    aff = jax.nn.sigmoid(logits)
    sel = (aff + router_b.astype(F32)).reshape(-1, MOE_GROUPS, EXPERTS_PER_GROUP)
    lane = lax.broadcasted_iota(jnp.int32, sel.shape, 2)

    def top2(x, ids):
        i1 = jnp.argmax(x, axis=-1)
        x2 = jnp.where(ids == i1[..., None], -jnp.inf, x)
        i2 = jnp.argmax(x2, axis=-1)
        return jnp.max(x, axis=-1), jnp.max(x2, axis=-1), i1, i2

    m1, m2, _, _ = top2(sel, lane)
    grp = jnp.argmax(m1 + m2, axis=-1)
    in_grp = jnp.take_along_axis(sel, grp[:, None, None], axis=1)[:, 0]
    _, _, l1, l2 = top2(in_grp, lane[:, 0])
    idx = grp[:, None] * EXPERTS_PER_GROUP + jnp.stack([l1, l2], axis=-1)
    wts = jnp.take_along_axis(aff, idx, axis=1)
    wts = wts / jnp.sum(wts, axis=-1, keepdims=True)
    return idx.astype(jnp.int32), wts


def _moe(h, logits, router_b, w_gate, w_up, w_down):
    M, D = h.shape
    tm = MOE_TM
    idx, wts = _route(logits, router_b)
    n_asg = M * MOE_TOP_K
    P = n_asg + N_EXPERTS * tm
    onehot = jax.nn.one_hot(idx.reshape(-1), N_EXPERTS, dtype=jnp.int32)
    csum = jnp.cumsum(onehot, axis=0)
    counts = csum[-1]
    pcounts = ((counts + tm - 1) // tm) * tm
    pends = jnp.cumsum(pcounts)
    pstarts = pends - pcounts
    dest = jnp.sum(onehot * (pstarts[None, :] + csum - 1), axis=1)
    fields = jnp.stack([jnp.arange(n_asg, dtype=jnp.int32) // MOE_TOP_K,
                        lax.bitcast_convert_type(wts.reshape(-1), jnp.int32)], axis=1)
    slots = jnp.zeros((P, 2), jnp.int32).at[dest].set(fields)
    rows_tok = slots[:, 0]
    gate_w = lax.bitcast_convert_type(slots[:, 1], F32)
    pos = dest.reshape(M, MOE_TOP_K)
    tile_start = jnp.arange(P // tm, dtype=jnp.int32) * tm
    tile_expert = jnp.minimum(jnp.searchsorted(pends, tile_start, side='right'), N_EXPERTS - 1).astype(jnp.int32)
    n_active = (pends[-1] // tm).astype(jnp.int32).reshape(1)
    xs = jnp.take(h, rows_tok, axis=0, mode='clip')
    y = _moe_experts(tile_expert, n_active, xs, gate_w.reshape(P, 1), w_gate, w_up, w_down)
    yg = lax.optimization_barrier(jnp.take(y, pos.T.reshape(-1), axis=0, mode='clip'))
    return yg[:M] + yg[M:]


def _final_norm_kernel(x_ref, g_ref, o_ref):
    x = x_ref[...]
    o_ref[...] = x * lax.rsqrt(jnp.mean(x * x, axis=-1, keepdims=True) + RMS_EPS) * g_ref[...]


def _final_norm(x, g):
    M, D = x.shape
    tm = 512
    return pl.pallas_call(
        _final_norm_kernel,
        grid=(M // tm,),
        in_specs=[pl.BlockSpec((tm, D), lambda i: (i, 0)), pl.BlockSpec((1, D), lambda i: (0, 0))],
        out_specs=pl.BlockSpec((tm, D), lambda i: (i, 0)),
        out_shape=jax.ShapeDtypeStruct((M, D), F32),
        compiler_params=_params("arbitrary"),
    )(x, g.reshape(1, D))


def _to_col_major(t, rows):
    b, n = t.shape[:2]
    return t.reshape(b, rows, GRID_W, -1).swapaxes(1, 2).reshape(b, n, -1)


def _from_col_major(t, rows):
    b, n = t.shape[:2]
    return t.reshape(b, GRID_W, rows, -1).swapaxes(1, 2).reshape(b, n, -1)


def _split_cols(u, sizes):
    out, start = [], 0
    for s in sizes:
        out.append(u[..., start:start + s])
        start += s
    return out


def _regroup_w_in(w):
    z, xbc, dt, q, k, v, g, glr, rw = _split_cols(w, IN_SPLITS)
    pad = jnp.zeros((w.shape[0], P_IN_PAD - sum(IN_SPLITS)), w.dtype)
    return jnp.concatenate([rw, dt, glr, pad, z, xbc, q, k, v, g], axis=1).astype(BF16)


def _lane_row(v, offset=0):
    v = v.reshape(-1)
    return jnp.zeros((1, LANE), F32).at[0, offset:offset + v.shape[0]].set(v)


def _rows_at(m, offset):
    return jnp.zeros((LANE, m.shape[1]), F32).at[offset:offset + m.shape[0]].set(m)


def _mixer_params(p, l):
    two = range(2)
    rw = dict(
        mu=jnp.pad(p['rwkv_mu'][l], ((0, 0), (0, COL_RW[0] - P_RWKV))),
        w0=p['rwkv_w0'][l].reshape(2, 1, D_RWKV),
        w2=jnp.stack([_rows_at(p['rwkv_w2'][l][d], d * RWKV_LW) for d in two]),
        a0=p['rwkv_a0'][l].reshape(2, 1, D_RWKV),
        a2=jnp.stack([_rows_at(p['rwkv_a2'][l][d], d * RWKV_LA) for d in two]),
        g2=p['rwkv_g2'][l],
        k_k=p['rwkv_k_k'][l].reshape(1, D_RWKV),
        k_a=p['rwkv_k_a'][l].reshape(1, D_RWKV),
        r_k=p['rwkv_r_k'][l].reshape(1, D_RWKV))
    return dict(
        conv_w=p['ssd_conv_w'][l].reshape(9, SSD_CONV_CH),
        conv_b=p['ssd_conv_b'][l].reshape(1, SSD_CONV_CH),
        dt_bias=_lane_row(p['ssd_dt_bias'][l]),
        ssd_a=_lane_row(-jnp.exp(p['ssd_a_log'][l])),
        gla_up=jnp.stack([_rows_at(p['gla_gate_up'][l][d], 2 * SSD_HEADS + d * GLA_LR) for d in two]),
        gla_b=p['gla_gate_b'][l].reshape(2, 1, GLA_QK),
        rwkv=rw,
        epi=dict(dx=jnp.repeat(p['ssd_d'][l], SSD_HEAD_DIM).reshape(1, D_SSD),
                 ssd_norm_g=p['ssd_norm_g'][l].reshape(1, D_SSD),
                 gla_norm_g=p['gla_norm_g'][l].reshape(1, GLA_DV),
                 ln_g=p['rwkv_ln_g'][l].reshape(1, D_RWKV),
                 ln_b=p['rwkv_ln_b'][l].reshape(1, D_RWKV)))


def _token_mixers(u, b0, B, T, s0, mp):
    s_ssd, s_gla, s_rwkv = s0
    st_ssd = s_ssd.transpose(0, 1, 4, 2, 3).reshape(B, 2, SSD_STATE, D_SSD)
    st_gla = s_gla.transpose(0, 1, 4, 2, 3).reshape(B, 2, GLA_DV, GLA_QK)
    yf, yb, f_ssd = _ssd_scan(u, b0, B, T, mp['dt_bias'], mp['ssd_a'], st_ssd)
    gf, gb, f_gla = _gla_scan(u, b0, B, T, mp['gla_up'], mp['gla_b'], st_gla)
    rf, rb, gate, bonus, f_rwkv = _rwkv_scan(u, b0, B, T, mp['rwkv'], s_rwkv)
    mix = _epilogue(u, b0, B, T, (yf, yb), (gf, gb), (rf, rb, gate, bonus), mp['epi'])
    f_ssd = f_ssd.reshape(B, 2, SSD_STATE, SSD_HEADS, SSD_HEAD_DIM).transpose(0, 1, 3, 4, 2)
    f_gla = f_gla.reshape(B, 2, GLA_DV, GLA_HEADS, GLA_DK).transpose(0, 1, 3, 4, 2)
    return mix, (f_ssd, f_gla, f_rwkv)


def kernel(x_prompt, x_sample, state_ssd, state_gla, state_rwkv, c, c_ctx, ada_w, ada_b, norm_mix_g, norm_ffn_g,
           w_in, ssd_conv_w, ssd_conv_b, ssd_dt_bias, ssd_a_log, ssd_d, ssd_norm_g, gla_gate_up, gla_gate_b,
           gla_norm_g, rwkv_mu, rwkv_w0, rwkv_w2, rwkv_a0, rwkv_a2, rwkv_g2, rwkv_k_k, rwkv_k_a, rwkv_r_k,
           rwkv_ln_g, rwkv_ln_b, w_out, router_w, router_b, moe_w_gate, moe_w_up, moe_w_down, final_norm_g):
    p = dict(ssd_conv_w=ssd_conv_w, ssd_conv_b=ssd_conv_b, ssd_dt_bias=ssd_dt_bias, ssd_a_log=ssd_a_log,
             ssd_d=ssd_d, ssd_norm_g=ssd_norm_g, gla_gate_up=gla_gate_up, gla_gate_b=gla_gate_b,
             gla_norm_g=gla_norm_g, rwkv_mu=rwkv_mu, rwkv_w0=rwkv_w0, rwkv_w2=rwkv_w2, rwkv_a0=rwkv_a0,
             rwkv_a2=rwkv_a2, rwkv_g2=rwkv_g2, rwkv_k_k=rwkv_k_k, rwkv_k_a=rwkv_k_a, rwkv_r_k=rwkv_r_k,
             rwkv_ln_g=rwkv_ln_g, rwkv_ln_b=rwkv_ln_b)
    D = D_MODEL
    nb, seq = x_prompt.shape[:2]
    db, dseq = x_sample.shape[:2]
    grp = nb * seq
    assert dseq == grp and seq % CHUNK == 0 and dseq == GRID_W * GRID_W
    G = 1 + db
    x = jnp.concatenate([x_prompt.reshape(1, grp, D), x_sample], axis=0)
    cvec = jnp.concatenate([c_ctx[None, :], c, jnp.zeros((8 - G, D), F32)], axis=0)
    mods = _ada(cvec, ada_w, ada_b)[:, :G]
    router_w_pad = jnp.pad(router_w, ((0, 0), (0, LANE - N_EXPERTS)))
    zero_states = (jnp.zeros((nb, 2, SSD_HEADS, SSD_HEAD_DIM, SSD_STATE), F32),
                   jnp.zeros((nb, 2, GLA_HEADS, GLA_DK, GLA_DV), F32),
                   jnp.zeros((nb, 2, RWKV_HEADS, RWKV_N, RWKV_N), F32))
    ctx_states = []
    for l in range(DEPTH):
        sh1, sc1, g1, sh2, sc2, g2 = jnp.split(mods[l], 6, axis=-1)
        mp = _mixer_params(p, l)
        u = _inproj(x, norm_mix_g[l], sh1, sc1, _regroup_w_in(w_in[l]))
        u = _conv_inplace(u, mp['conv_w'], mp['conv_b'], 0, 1, seq, 1)
        u = _conv_inplace(u, mp['conv_w'], mp['conv_b'], 1, db, GRID_W, dseq // GRID_W)
        mix_p, new = _token_mixers(u.reshape(G * nb, seq, P_IN_PAD), 0, nb, seq, zero_states, mp)
        s0 = (state_ssd[:, l], state_gla[:, l], state_rwkv[:, l])
        if l % 2 == 1:
            mix_s, _ = _token_mixers(_to_col_major(u[1:], dseq // GRID_W), 0, db, dseq, s0, mp)
            mix_s = _from_col_major(mix_s, dseq // GRID_W)
        else:
            mix_s, _ = _token_mixers(u, 1, db, dseq, s0, mp)
        ctx_states.append(new)
        mix = jnp.concatenate([mix_p.reshape(1, grp, D), mix_s], axis=0)
        x = _outproj(mix, w_out[l].astype(BF16), x, g1)
        h, logits = _ffn_in(x, norm_ffn_g[l], sh2, sc2, router_w_pad)
        moe = _moe(h.reshape(G * grp, D), logits.reshape(G * grp, LANE)[:, :N_EXPERTS], router_b,
                   moe_w_gate[l].astype(BF16), moe_w_up[l].astype(BF16), moe_w_down[l].astype(BF16))
        x = x + g2[:, None, :] * moe.reshape(G, grp, D)
    y = _final_norm(x.reshape(G * grp, D), final_norm_g).reshape(G, grp, D)
    y_prompt = y[0].reshape(nb, seq, D)
    y_sample = y[1:]
    new_ssd = jnp.stack([s[0] for s in ctx_states], axis=1)
    new_gla = jnp.stack([s[1] for s in ctx_states], axis=1)
    new_rwkv = jnp.stack([s[2] for s in ctx_states], axis=1)
    return (y_prompt, y_sample, new_ssd, new_gla, new_rwkv)
```

```python
import functools

import jax
import jax.numpy as jnp
from jax import lax
from jax.experimental import pallas as pl
from jax.experimental.pallas import tpu as pltpu

F32 = jnp.float32
BF16 = jnp.bfloat16

D_MODEL = 2048
DEPTH = 2
GRID_W = 64
SSD_HEAD_DIM = 64
D_SSD = 1024
SSD_HEADS = 16
SSD_GROUPS = 2
SSD_STATE = 128
SSD_BC = SSD_GROUPS * SSD_STATE
SSD_CONV_CH = D_SSD + 2 * SSD_BC
GLA_HEADS = 4
D_GLA = 512
GLA_DV = 128
GLA_DK = 64
GLA_QK = GLA_HEADS * GLA_DK
GLA_LR = 16
GLA_GATE_NORM = 16.0
RWKV_N = 64
D_RWKV = 512
RWKV_HEADS = 8
RWKV_LW = 64
RWKV_LA = 64
RWKV_LG = 128
P_RWKV = 3 * D_RWKV + 2 * RWKV_LW + 2 * RWKV_LA + RWKV_LG
IN_SPLITS = (D_SSD, SSD_CONV_CH, 2 * SSD_HEADS, GLA_QK, GLA_QK, D_GLA, D_GLA, 2 * GLA_LR, P_RWKV)
N_EXPERTS = 16
MOE_GROUPS = 4
EXPERTS_PER_GROUP = 4
MOE_TOP_K = 2
D_EXPERT = 1024
RMS_EPS = 1e-6
RWKV_LN_EPS = 64e-5

LANE = 128
CHUNK = 64
RWKV_INV_BASE = 8
MOE_TM = 256
VMEM_LIMIT = 56 * 1024 * 1024

P_IN_PAD = 6144
COL_RW = (2048, 0)
COL_DT = (LANE, 15)
COL_Z = (1024, 2)
COL_X = (1024, 3)
COL_BC = (512, 8)
COL_QK = (512, 9)
COL_V = (512, 10)
COL_G = (512, 11)
CONV_TILE = 256
CONV_FIRST_TILE = 12


def _dot(a, b, contract, passes):
    dn = (contract, ((), ()))
    if passes == 6:
        return lax.dot_general(a, b, dn, precision=lax.Precision.HIGHEST, preferred_element_type=F32)
    ah = a.astype(BF16)
    bh = b.astype(BF16)
    out = lax.dot_general(ah, bh, dn, preferred_element_type=F32)
    if passes == 3:
        al = (a - ah.astype(F32)).astype(BF16)
        bl = (b - bh.astype(F32)).astype(BF16)
        out = out + lax.dot_general(ah, bl, dn, preferred_element_type=F32)
        out = out + lax.dot_general(al, bh, dn, preferred_element_type=F32)
    return out


_NN = ((1,), (0,))
_NT = ((1,), (1,))
_TN = ((0,), (0,))


def _params(*sem):
    return pltpu.CompilerParams(dimension_semantics=sem, vmem_limit_bytes=VMEM_LIMIT)


def _split3(x):
    hi = x.astype(BF16)
    r1 = x - hi.astype(F32)
    mid = r1.astype(BF16)
    lo = (r1 - mid.astype(F32)).astype(BF16)
    return hi, mid, lo


def _sel_dot(sel, x, contract=_NN):
    dn = (contract, ((), ()))
    s = sel.astype(BF16)
    hi, mid, lo = _split3(x)
    out = lax.dot_general(s, hi, dn, preferred_element_type=F32)
    out = out + lax.dot_general(s, mid, dn, preferred_element_type=F32)
    return out + lax.dot_general(s, lo, dn, preferred_element_type=F32)


def _dot_sel(x, sel):
    dn = (_NN, ((), ()))
    s = sel.astype(BF16)
    hi, mid, lo = _split3(x)
    out = lax.dot_general(hi, s, dn, preferred_element_type=F32)
    out = out + lax.dot_general(mid, s, dn, preferred_element_type=F32)
    return out + lax.dot_general(lo, s, dn, preferred_element_type=F32)


def _bdot(a, b, contract=_NN):
    return lax.dot_general(a.astype(BF16), b.astype(BF16), (contract, ((), ())), preferred_element_type=F32)


def _silu(x):
    return x * jax.nn.sigmoid(x)


def _softplus(x):
    return jnp.maximum(x, 0.0) + jnp.log(1.0 + jnp.exp(-jnp.abs(x)))


def _log_sigmoid(x):
    return jnp.minimum(x, 0.0) - jnp.log(1.0 + jnp.exp(-jnp.abs(x)))


def _time_masks(d):
    row = lax.broadcasted_iota(jnp.int32, (CHUNK, CHUNK), 0)
    col = lax.broadcasted_iota(jnp.int32, (CHUNK, CHUNK), 1)
    if d == 0:
        return col <= row, col < row
    return col >= row, col > row


def _chunk_spec(width_idx, b0, chunk_of, col_major=False, total=P_IN_PAD):
    w, j = width_idx
    if col_major:
        per = total // w
        return pl.BlockSpec((1, CHUNK, w), lambda b, c: (b + b0, 0, chunk_of(c) * per + j))
    return pl.BlockSpec((1, CHUNK, w), lambda b, c: (b + b0, chunk_of(c), j))


def _seq_specs(width_idx, b0, nc, col_major=False):
    return [_chunk_spec(width_idx, b0, lambda c: c, col_major),
            _chunk_spec(width_idx, b0, lambda c: nc - 1 - c, col_major)]


def _conv_kernel(x_ref, w_ref, b_ref, o_ref, *, width, rows):
    x = x_ref[0]
    T = x.shape[0]
    t = lax.broadcasted_iota(jnp.int32, (T, 1), 0)
    col = t % width
    row = (t // width) % rows
    acc = jnp.zeros_like(x) + b_ref[...]
    for dr in (-1, 0, 1):
        if rows == 1 and dr != 0:
            continue
        for dc in (-1, 0, 1):
            off = dr * width + dc
            xs = x if off == 0 else pltpu.roll(x, (-off) % T, axis=0)
            valid = (col + dc >= 0) & (col + dc < width) & (row + dr >= 0) & (row + dr < rows)
            tap = (dr + 1) * 3 + (dc + 1)
            acc = acc + jnp.where(valid, xs, 0.0) * w_ref[tap:tap + 1, :]
    o_ref[0] = _silu(acc)


def _conv_inplace(u, w9, bias, g0, ng, width, rows):
    G, T, P = u.shape
    nt = SSD_CONV_CH // CONV_TILE
    blk = pl.BlockSpec((1, T, CONV_TILE), lambda g, j: (g + g0, 0, CONV_FIRST_TILE + j))
    return pl.pallas_call(
        functools.partial(_conv_kernel, width=width, rows=rows),
        grid=(ng, nt),
        in_specs=[blk,
                  pl.BlockSpec((9, CONV_TILE), lambda g, j: (0, j)),
                  pl.BlockSpec((1, CONV_TILE), lambda g, j: (0, j))],
        out_specs=blk,
        out_shape=jax.ShapeDtypeStruct(u.shape, u.dtype),
        input_output_aliases={0: 0},
        compiler_params=_params("arbitrary", "arbitrary"),
    )(u, w9, bias)


def _ssd_kernel(xf_ref, bcf_ref, dtf_ref, xb_ref, bcb_ref, dtb_ref, bias_ref, a_ref, s0_ref,
                yf_ref, yb_ref, sfin_ref, st_scr):
    c = pl.program_id(1)
    L = CHUNK
    HP = D_SSD

    @pl.when(c == 0)
    def _():
        st_scr[...] = s0_ref[0]

    rowx = lax.broadcasted_iota(jnp.int32, (L, HP), 0)
    s_of_lane = lax.broadcasted_iota(jnp.int32, (L, HP), 1) % SSD_HEAD_DIM
    e_row = lax.broadcasted_iota(jnp.int32, (LANE, HP), 0)
    e_head = lax.broadcasted_iota(jnp.int32, (LANE, HP), 1) // SSD_HEAD_DIM
    lane2 = lax.broadcasted_iota(jnp.int32, (L, LANE), 1)
    for d, (x_ref, bc_ref, dt_ref, y_ref) in enumerate(((xf_ref, bcf_ref, dtf_ref, yf_ref),
                                                        (xb_ref, bcb_ref, dtb_ref, yb_ref))):
        incl, _ = _time_masks(d)
        strict_t = (s_of_lane < rowx) if d == 0 else (s_of_lane > rowx)
        incl_t = (s_of_lane <= rowx) if d == 0 else (s_of_lane >= rowx)
        expand = e_row == d * SSD_HEADS + e_head
        dtv = _softplus(dt_ref[0] + bias_ref[...])
        dtx = _dot_sel(dtv, expand)
        dax = _dot_sel(dtv * a_ref[...], expand)
        cumx = _sel_dot(incl, dax)
        totx = jnp.sum(dax, axis=0, keepdims=True)
        segx = _sel_dot(incl, jnp.where(strict_t, dax, 0.0))
        decx = jnp.where(incl_t, jnp.exp(jnp.minimum(segx, 0.0)), 0.0)
        x = x_ref[0]
        xdt = x * dtx
        xend = xdt * jnp.exp(totx - cumx)
        ecum = jnp.exp(cumx)
        etot = jnp.exp(totx)
        bc = bc_ref[0]
        for g in range(SSD_GROUPS):
            bm = bc[:, g * SSD_STATE:(g + 1) * SSD_STATE]
            cm = bc[:, SSD_BC + g * SSD_STATE:SSD_BC + (g + 1) * SSD_STATE]
            sc = _bdot(cm, bm, _NT)
            sc2 = jnp.concatenate([sc, sc], axis=1)
            for i in range(SSD_HEADS // SSD_GROUPS // 2):
                ls = slice((g * 4 + i) * LANE, (g * 4 + i + 1) * LANE)
                xp = xdt[:, ls]
                xbd = jnp.concatenate([jnp.where(lane2 < SSD_HEAD_DIM, xp, 0.0),
                                       jnp.where(lane2 >= SSD_HEAD_DIM, xp, 0.0)], axis=0)
                st = st_scr[d, :, ls]
                y_ref[0, :, ls] = _bdot(decx[:, ls] * sc2, xbd) + _bdot(cm, st) * ecum[:, ls]
                st_scr[d, :, ls] = st * etot[:, ls] + _bdot(bm, xend[:, ls], _TN)

    @pl.when(c == pl.num_programs(1) - 1)
    def _():
        sfin_ref[0] = st_scr[...]


def _ssd_scan(u, b0, B, T, bias128, a128, s0, col_major):
    nc = T // CHUNK
    st = pl.BlockSpec((1, 2, SSD_STATE, D_SSD), lambda b, c: (b, 0, 0, 0))
    vec = pl.BlockSpec((1, LANE), lambda b, c: (0, 0))
    xs, bcs, dts = (_seq_specs(col, b0, nc, col_major) for col in (COL_X, COL_BC, COL_DT))
    ys = _seq_specs((D_SSD, 0), 0, nc)
    return pl.pallas_call(
        _ssd_kernel,
        grid=(B, nc),
        in_specs=[xs[0], bcs[0], dts[0], xs[1], bcs[1], dts[1], vec, vec, st],
        out_specs=[ys[0], ys[1], st],
        out_shape=[jax.ShapeDtypeStruct((B, T, D_SSD), F32), jax.ShapeDtypeStruct((B, T, D_SSD), F32),
                   jax.ShapeDtypeStruct((B, 2, SSD_STATE, D_SSD), F32)],
        scratch_shapes=[pltpu.VMEM((2, SSD_STATE, D_SSD), F32)],
        compiler_params=_params("arbitrary", "arbitrary"),
    )(u, u, u, u, u, u, bias128, a128, s0)


def _stack_heads(x, head_of_lane):
    return jnp.concatenate([jnp.where(head_of_lane == h, x, 0.0) for h in range(GLA_HEADS)], axis=0)


def _gla_kernel(qkf_ref, vf_ref, dtf_ref, qkb_ref, vb_ref, dtb_ref, wup_ref, gb_ref, s0_ref,
                of_ref, ob_ref, sfin_ref, st_scr):
    c = pl.program_id(1)
    L = CHUNK

    @pl.when(c == 0)
    def _():
        st_scr[...] = s0_ref[0]

    row = lax.broadcasted_iota(jnp.int32, (L, L), 0)
    col = lax.broadcasted_iota(jnp.int32, (L, L), 1)
    row1 = lax.broadcasted_iota(jnp.int32, (L, 1), 0)
    head_of_lane = lax.broadcasted_iota(jnp.int32, (L, GLA_QK), 1) // GLA_DK
    row4 = lax.broadcasted_iota(jnp.int32, (GLA_HEADS * L, L), 0) % L
    col4 = lax.broadcasted_iota(jnp.int32, (GLA_HEADS * L, L), 1)
    for d, (qk_ref, v_ref, dt_ref, o_ref) in enumerate(((qkf_ref, vf_ref, dtf_ref, of_ref),
                                                        (qkb_ref, vb_ref, dtb_ref, ob_ref))):
        incl, _ = _time_masks(d)
        flip = (lambda i: i) if d == 0 else (lambda i: L - 1 - i)
        tq, tj, tq1, tq4, tj4 = flip(row), flip(col), flip(row1), flip(row4), flip(col4)
        qk = qk_ref[0]
        q = qk[:, :GLA_QK] * GLA_DK ** -0.5
        k = qk[:, GLA_QK:]
        v = v_ref[0]
        lg = _log_sigmoid(_dot(dt_ref[0], wup_ref[d], _NN, 3) + gb_ref[d]) / GLA_GATE_NORM
        cum = _sel_dot(incl, lg)
        tot = jnp.sum(lg, axis=0, keepdims=True)
        acc = jnp.where(row4 == col4, _bdot(_stack_heads(q, head_of_lane), k, _NT), 0.0)
        halves = [L >> (i + 1) for i in range(L.bit_length() - 1)]
        pick_mid = [(tj == (tq // (2 * m)) * (2 * m) + m - 1).astype(BF16) for m in halves]
        cmid_all = _sel_dot(jnp.concatenate(pick_mid, axis=0), cum)
        for i, m in enumerate(halves):
            blk = 2 * m
            cmid = cmid_all[i * L:(i + 1) * L]
            later = (tq1 % blk) >= m
            qs = jnp.where(later, q * jnp.exp(jnp.minimum(cum - cmid, 0.0)), 0.0)
            ks = jnp.where(later, 0.0, k * jnp.exp(jnp.minimum(cmid - cum, 0.0)))
            a_lvl = _bdot(_stack_heads(qs, head_of_lane), ks, _NT)
            acc = acc + jnp.where((tq4 // blk) == (tj4 // blk), a_lvl, 0.0)
        st = st_scr[d]
        o_inter = _bdot(_stack_heads(q * jnp.exp(cum), head_of_lane), st, _NT)
        for h in range(GLA_HEADS):
            rs = slice(h * L, (h + 1) * L)
            ls = slice(h * GLA_DV, (h + 1) * GLA_DV)
            o_ref[0, :, ls] = _bdot(acc[rs], v[:, ls]) + o_inter[rs]
        vst = jnp.concatenate([v[:, h * GLA_DV:(h + 1) * GLA_DV] for h in range(GLA_HEADS)], axis=0)
        kst = _stack_heads(k * jnp.exp(tot - cum), head_of_lane)
        st_scr[d] = st * jnp.exp(tot) + _bdot(vst, kst, _TN)

    @pl.when(c == pl.num_programs(1) - 1)
    def _():
        sfin_ref[0] = st_scr[...]


def _gla_scan(u, b0, B, T, wup, gb, s0, col_major):
    nc = T // CHUNK
    st = pl.BlockSpec((1, 2, GLA_DV, GLA_QK), lambda b, c: (b, 0, 0, 0))
    qks, vs, dts = (_seq_specs(col, b0, nc, col_major) for col in (COL_QK, COL_V, COL_DT))
    os_ = _seq_specs((D_GLA, 0), 0, nc)
    return pl.pallas_call(
        _gla_kernel,
        grid=(B, nc),
        in_specs=[qks[0], vs[0], dts[0], qks[1], vs[1], dts[1],
                  pl.BlockSpec((2, LANE, GLA_QK), lambda b, c: (0, 0, 0)),
                  pl.BlockSpec((2, 1, GLA_QK), lambda b, c: (0, 0, 0)), st],
        out_specs=[os_[0], os_[1], st],
        out_shape=[jax.ShapeDtypeStruct((B, T, D_GLA), F32), jax.ShapeDtypeStruct((B, T, D_GLA), F32),
                   jax.ShapeDtypeStruct((B, 2, GLA_DV, GLA_QK), F32)],
        scratch_shapes=[pltpu.VMEM((2, GLA_DV, GLA_QK), F32)],
        compiler_params=_params("arbitrary", "arbitrary"),
    )(u, u, u, u, u, u, wup, gb, s0)


def _rwkv_kernel(xf_ref, hpf_ref, hnf_ref, xb_ref, hpb_ref, hnb_ref, mu_ref, w0_ref, w2_ref, a0_ref, a2_ref,
                 g2_ref, kkw_ref, ka_ref, rk_ref, s0_ref, of_ref, ob_ref, gate_ref, bonus_ref, sfin_ref, s_scr):
    c = pl.program_id(1)
    nc = pl.num_programs(1)
    C = CHUNK
    N = RWKV_N
    D = D_RWKV

    @pl.when(c == 0)
    def _():
        s_scr[...] = s0_ref[0]

    row = lax.broadcasted_iota(jnp.int32, (C, C), 0)
    col = lax.broadcasted_iota(jnp.int32, (C, C), 1)
    row1 = lax.broadcasted_iota(jnp.int32, (C, 1), 0)
    eye = (col == row).astype(F32)
    same_head = (lax.broadcasted_iota(jnp.int32, (D, D), 0) // N) == (lax.broadcasted_iota(jnp.int32, (D, D), 1) // N)

    def a_icl(ad, d):
        return jax.nn.sigmoid(a0_ref[d] + _dot(ad, a2_ref[d], _NN, 3))

    def k_eff(kr, a):
        return kr * (1.0 + (a - 1.0) * ka_ref[...])

    chains = []
    for d, (x_ref, hp_ref, hn_ref, o_ref) in enumerate(((xf_ref, hpf_ref, hnf_ref, of_ref),
                                                        (xb_ref, hpb_ref, hnb_ref, ob_ref))):
        cc = c if d == 0 else nc - 1 - c
        incl, strict = _time_masks(d)
        x = x_ref[0]
        prev_row = jnp.where(cc > 0, hp_ref[0][7:8, :], 0.0)
        next_row = jnp.where(cc < nc - 1, hn_ref[0][0:1, :], 0.0)
        prev = jnp.where(row1 == 0, prev_row, pltpu.roll(x, 1, axis=0))
        nxt = jnp.where(row1 == C - 1, next_row, pltpu.roll(x, C - 1, axis=0))
        xm = x + mu_ref[0:1, :] * (prev - x) + mu_ref[1:2, :] * (nxt - x)
        r = xm[:, 0:D]
        kr = xm[:, D:2 * D]
        v = xm[:, 2 * D:3 * D]
        wd = xm[:, 3 * D:3 * D + LANE]
        ad = xm[:, 3 * D + LANE:3 * D + 2 * LANE]
        w_log = _log_sigmoid(w0_ref[d] + _dot(jnp.tanh(wd), w2_ref[d], _NN, 3)) - 0.5
        lw = -jnp.exp(w_log)
        a = a_icl(ad, d)
        kk = kr * kkw_ref[...]
        kk = kk * lax.rsqrt(_dot_sel(kk * kk, same_head) + 1e-12)
        k = k_eff(kr, a)
        if d == 0:
            gd = xm[:, 3 * D + 2 * LANE:3 * D + 3 * LANE]
            gate_ref[0] = _dot(jax.nn.sigmoid(gd), g2_ref[...], _NN, 3)
            k_both = k + k_eff(kr, a_icl(ad, 1))
            bonus_ref[0] = _dot_sel(r * k_both * rk_ref[...], same_head) * v

        cum = _sel_dot(incl, lw)
        tot = jnp.sum(lw, axis=0, keepdims=True)
        b = a * kk
        kp = kk * jnp.exp(cum - lw)
        rp = r * jnp.exp(cum)
        pinv = jnp.exp(-cum)
        kinv = k * pinv
        binv = b * pinv
        pend = jnp.exp(tot - cum)
        kd = k * pend
        bd = b * pend
        ptot = jnp.exp(tot)
        for h in range(RWKV_HEADS):
            sl = slice(h * N, (h + 1) * N)
            chains.append(dict(
                d=d, h=h, sl=sl, incl=incl, strict=strict, o_ref=o_ref, v=v[:, sl], kd=kd[:, sl], bd=bd[:, sl],
                ptot=ptot[:, sl],
                x=jnp.concatenate([kp[:, sl], rp[:, sl]], axis=0).astype(BF16),
                y=jnp.concatenate([kinv[:, sl], binv[:, sl]], axis=0).astype(BF16)))

    for ch in chains:
        aa = _bdot(ch['x'], ch['y'], _NT)
        ch['a_kk'] = jnp.where(ch['strict'], aa[:C, :C], 0.0)
        ch['L'] = jnp.where(ch['strict'], aa[:C, C:], 0.0)
        ch['a_rk'] = jnp.where(ch['incl'], aa[C:, :C], 0.0)
        ch['a_rb'] = jnp.where(ch['incl'], aa[C:, C:], 0.0)
    def same_block(m):
        return (row // m) == (col // m)

    for ch in chains:
        ch['pw'] = jnp.where(same_block(RWKV_INV_BASE), ch['L'], 0.0)
        ch['t'] = eye - ch['pw']
    n = 2
    while n < RWKV_INV_BASE:
        for ch in chains:
            ch['pw'] = _bdot(ch['pw'], ch['pw'])
        for ch in chains:
            ch['t'] = ch['t'] + _bdot(ch['t'], ch['pw'])
        n *= 2
    m = RWKV_INV_BASE
    while m < C:
        off = same_block(2 * m) & jnp.logical_not(same_block(m))
        for ch in chains:
            ch['pw'] = _bdot(ch['t'], jnp.where(off, ch['L'], 0.0))
        for ch in chains:
            ch['t'] = ch['t'] - _bdot(ch['pw'], ch['t'])
        m *= 2
    for ch in chains:
        ch['s0'] = s_scr[ch['d'], ch['h']]
        ch['xs'] = _bdot(ch['x'], ch['s0'], _NT)
        ch['g'] = ch['xs'][:C] + _bdot(ch['a_kk'], ch['v'])
    for ch in chains:
        ch['u'] = _bdot(ch['t'], ch['g'])
    for ch in chains:
        o = ch['xs'][C:] + _bdot(ch['a_rk'], ch['v']) - _bdot(ch['a_rb'], ch['u'])
        ch['o_ref'][0, :, ch['sl']] = o
        s_scr[ch['d'], ch['h']] = (ch['s0'] * ch['ptot'] + _bdot(ch['v'], ch['kd'], _TN)
                                   - _bdot(ch['u'], ch['bd'], _TN))

    @pl.when(c == nc - 1)
    def _():
        sfin_ref[0] = s_scr[...]


def _rwkv_scan(u, b0, B, T, prm, s0, col_major):
    nc = T // CHUNK
    w, j = COL_RW
    xs = _seq_specs(COL_RW, b0, nc, col_major)
    per = P_IN_PAD // w
    sub = CHUNK // 8

    def halo(chunk_of, step):
        edge = sub - 1 if step < 0 else 0

        def index(b, c):
            k = jnp.clip(chunk_of(c) + step, 0, nc - 1)
            return (b + b0, edge, k * per + j) if col_major else (b + b0, k * sub + edge, j)

        return pl.BlockSpec((1, 8, w), index)

    fwd, bwd = (lambda c: c), (lambda c: nc - 1 - c)
    halo_prev = [halo(fwd, -1), halo(bwd, -1)]
    halo_next = [halo(fwd, 1), halo(bwd, 1)]
    st = pl.BlockSpec((1, 2, RWKV_HEADS, RWKV_N, RWKV_N), lambda b, c: (b, 0, 0, 0, 0))
    os_ = _seq_specs((D_RWKV, 0), 0, nc)

    def full(a):
        return pl.BlockSpec(a.shape, lambda b, c, _n=a.ndim: (0,) * _n)

    names = ('mu', 'w0', 'w2', 'a0', 'a2', 'g2', 'k_k', 'k_a', 'r_k')
    tok = jax.ShapeDtypeStruct((B, T, D_RWKV), F32)
    return pl.pallas_call(
        _rwkv_kernel,
        grid=(B, nc),
        in_specs=[xs[0], halo_prev[0], halo_next[0], xs[1], halo_prev[1], halo_next[1]]
                 + [full(prm[n]) for n in names] + [st],
        out_specs=[os_[0], os_[1], os_[0], os_[0], st],
        out_shape=[tok, tok, tok, tok, jax.ShapeDtypeStruct((B, 2, RWKV_HEADS, RWKV_N, RWKV_N), F32)],
        scratch_shapes=[pltpu.VMEM((2, RWKV_HEADS, RWKV_N, RWKV_N), F32)],
        compiler_params=_params("arbitrary", "arbitrary"),
    )(u, u, u, u, u, u, *[prm[n] for n in names], s0)


def _epilogue_kernel(yf_ref, yb_ref, z_ref, x_ref, gf_ref, gb_ref, g_ref, rf_ref, rb_ref, gate_ref, bonus_ref,
                     dx_ref, sng_ref, gng_ref, lng_ref, lnb_ref, o_ref):
    y = (yf_ref[0] + yb_ref[0] + x_ref[0] * dx_ref[...]) * _silu(z_ref[0])
    half = D_SSD // SSD_GROUPS
    for g in range(SSD_GROUPS):
        ls = slice(g * half, (g + 1) * half)
        yy = y[:, ls]
        yy = yy * lax.rsqrt(jnp.mean(yy * yy, axis=-1, keepdims=True) + RMS_EPS)
        o_ref[0, :, ls] = (yy * sng_ref[:, ls]).astype(BF16)
    go = gf_ref[0] + gb_ref[0]
    gg = g_ref[0]
    for h in range(GLA_HEADS):
        ls = slice(h * GLA_DV, (h + 1) * GLA_DV)
        oo = go[:, ls]
        oo = oo * lax.rsqrt(jnp.mean(oo * oo, axis=-1, keepdims=True) + RMS_EPS)
        o_ref[0, :, D_SSD + h * GLA_DV:D_SSD + (h + 1) * GLA_DV] = (oo * gng_ref[...] * _silu(gg[:, ls])).astype(BF16)
    ro = rf_ref[0] + rb_ref[0]
    same_head = ((lax.broadcasted_iota(jnp.int32, (D_RWKV, D_RWKV), 0) // RWKV_N)
                 == (lax.broadcasted_iota(jnp.int32, (D_RWKV, D_RWKV), 1) // RWKV_N))
    mean = _dot_sel(ro, same_head) * (1.0 / RWKV_N)
    cen = ro - mean
    var = _dot_sel(cen * cen, same_head) * (1.0 / RWKV_N)
    o = cen * lax.rsqrt(var + RWKV_LN_EPS) * lng_ref[...] + lnb_ref[...]
    o_ref[0, :, D_SSD + D_GLA:] = ((o + bonus_ref[0]) * gate_ref[0]).astype(BF16)


def _epilogue(u, b0, B, T, ssd_y, gla_o, rwkv_o, prm, col_major):
    names = ('dx', 'ssd_norm_g', 'gla_norm_g', 'ln_g', 'ln_b')

    def vec(a):
        return pl.BlockSpec(a.shape, lambda b, i: (0, 0))

    if col_major:
        tm, same = CHUNK, lambda c: c
        tok = lambda width_idx, off: _chunk_spec(width_idx, off, same)
        tok_u = lambda width_idx: _chunk_spec(width_idx, b0, same, True)
        out_spec = _chunk_spec((D_MODEL, 0), 0, same, True, D_MODEL)
        out_shape = jax.ShapeDtypeStruct((B, GRID_W, (T // GRID_W) * D_MODEL), BF16)
    else:
        tm = min(T, 512)

        def tok(width_idx, off):
            w, j = width_idx
            return pl.BlockSpec((1, tm, w), lambda b, i: (b + off, i, j))

        tok_u = lambda width_idx: tok(width_idx, b0)
        out_spec = tok((D_MODEL, 0), 0)
        out_shape = jax.ShapeDtypeStruct((B, T, D_MODEL), BF16)
    y1, g1, r1 = tok((D_SSD, 0), 0), tok((D_GLA, 0), 0), tok((D_RWKV, 0), 0)
    return pl.pallas_call(
        _epilogue_kernel,
        grid=(B, T // tm),
        in_specs=[y1, y1, tok_u(COL_Z), tok_u(COL_X), g1, g1, tok_u(COL_G), r1, r1, r1, r1]
                 + [vec(prm[n]) for n in names],
        out_specs=out_spec,
        out_shape=out_shape,
        compiler_params=_params("arbitrary", "arbitrary"),
    )(ssd_y[0], ssd_y[1], u, u, gla_o[0], gla_o[1], u, rwkv_o[0], rwkv_o[1], rwkv_o[2], rwkv_o[3],
      *[prm[n] for n in names]).reshape(B, T, D_MODEL)


def _ada_kernel(c_ref, w_ref, b_ref, o_ref):
    o_ref[0] = _dot(_silu(c_ref[...]), w_ref[0], _NN, 1) + b_ref[0]


def _ada(cvec8, ada_w, ada_b):
    tn = 1536
    n = ada_w.shape[-1]
    return pl.pallas_call(
        _ada_kernel,
        grid=(DEPTH, n // tn),
        in_specs=[pl.BlockSpec((8, D_MODEL), lambda l, j: (0, 0)),
                  pl.BlockSpec((1, D_MODEL, tn), lambda l, j: (l, 0, j)),
                  pl.BlockSpec((1, 1, tn), lambda l, j: (l, 0, j))],
        out_specs=pl.BlockSpec((1, 8, tn), lambda l, j: (l, 0, j)),
        out_shape=jax.ShapeDtypeStruct((DEPTH, 8, n), F32),
        compiler_params=_params("arbitrary", "arbitrary"),
    )(cvec8, ada_w, ada_b.reshape(DEPTH, 1, n))


def _norm_mod(x, g, shift, scale):
    y = x * lax.rsqrt(jnp.mean(x * x, axis=-1, keepdims=True) + RMS_EPS)
    return (y * g) * (1.0 + scale) + shift


def _inproj_kernel(x_ref, g_ref, sh_ref, sc_ref, w_ref, o_ref, h_scr):
    @pl.when(pl.program_id(2) == 0)
    def _():
        h_scr[...] = _norm_mod(x_ref[0], g_ref[...], sh_ref[0], sc_ref[0]).astype(BF16)

    o_ref[0] = jnp.dot(h_scr[...], w_ref[...], preferred_element_type=F32)


def _inproj(x, g, shift, scale, w):
    G, R, D = x.shape
    N = w.shape[1]
    tm, tn = 1024, 1024
    vec = pl.BlockSpec((1, 1, D), lambda gi, i, j: (gi, 0, 0))
    return pl.pallas_call(
        _inproj_kernel,
        grid=(G, R // tm, N // tn),
        in_specs=[pl.BlockSpec((1, tm, D), lambda gi, i, j: (gi, i, 0)),
                  pl.BlockSpec((1, D), lambda gi, i, j: (0, 0)),
                  vec, vec,
                  pl.BlockSpec((D, tn), lambda gi, i, j: (0, j))],
        out_specs=pl.BlockSpec((1, tm, tn), lambda gi, i, j: (gi, i, j)),
        out_shape=jax.ShapeDtypeStruct((G, R, N), F32),
        scratch_shapes=[pltpu.VMEM((tm, D), BF16)],
        compiler_params=_params("arbitrary", "arbitrary", "arbitrary"),
    )(x, g.reshape(1, D), shift.reshape(G, 1, D), scale.reshape(G, 1, D), w)


def _outproj_kernel(a_ref, w_ref, x_ref, gate_ref, o_ref):
    o_ref[0] = x_ref[0] + gate_ref[0] * jnp.dot(a_ref[0], w_ref[...], preferred_element_type=F32)


def _outproj(a, w, xres, gate):
    G, R, Kd = a.shape
    N = w.shape[1]
    tm, tn = 1024, 1024
    return pl.pallas_call(
        _outproj_kernel,
        grid=(G, R // tm, N // tn),
        in_specs=[pl.BlockSpec((1, tm, Kd), lambda gi, i, j: (gi, i, 0)),
                  pl.BlockSpec((Kd, tn), lambda gi, i, j: (0, j)),
                  pl.BlockSpec((1, tm, tn), lambda gi, i, j: (gi, i, j)),
                  pl.BlockSpec((1, 1, tn), lambda gi, i, j: (gi, 0, j))],
        out_specs=pl.BlockSpec((1, tm, tn), lambda gi, i, j: (gi, i, j)),
        out_shape=jax.ShapeDtypeStruct((G, R, N), F32),
        compiler_params=_params("arbitrary", "arbitrary", "arbitrary"),
    )(a, w, xres, gate.reshape(G, 1, N))


def _ffn_in_kernel(x_ref, g_ref, sh_ref, sc_ref, rw_ref, h_ref, lg_ref):
    h = _norm_mod(x_ref[0], g_ref[...], sh_ref[0], sc_ref[0])
    h_ref[0] = h
    lg_ref[0] = _dot(h, rw_ref[...], _NN, 6)


def _ffn_in(x, g, shift, scale, router_w_pad):
    G, R, D = x.shape
    tm = 512
    vec = pl.BlockSpec((1, 1, D), lambda gi, i: (gi, 0, 0))
    return pl.pallas_call(
        _ffn_in_kernel,
        grid=(G, R // tm),
        in_specs=[pl.BlockSpec((1, tm, D), lambda gi, i: (gi, i, 0)),
                  pl.BlockSpec((1, D), lambda gi, i: (0, 0)),
                  vec, vec,
                  pl.BlockSpec((D, LANE), lambda gi, i: (0, 0))],
        out_specs=[pl.BlockSpec((1, tm, D), lambda gi, i: (gi, i, 0)),
                   pl.BlockSpec((1, tm, LANE), lambda gi, i: (gi, i, 0))],
        out_shape=[jax.ShapeDtypeStruct((G, R, D), F32), jax.ShapeDtypeStruct((G, R, LANE), F32)],
        compiler_params=_params("arbitrary", "arbitrary"),
    )(x, g.reshape(1, D), shift.reshape(G, 1, D), scale.reshape(G, 1, D), router_w_pad)


def _moe_kernel(te_ref, na_ref, x_ref, gw_ref, wg_ref, wu_ref, wd_ref, o_ref, wg_s, wu_s, wd_s):
    i = pl.program_id(0)
    active = i < na_ref[0]
    new_expert = (i == 0) | (te_ref[i] != te_ref[jnp.maximum(i - 1, 0)])

    @pl.when(active & new_expert)
    def _():
        wg_s[...] = wg_ref[0, 0].astype(BF16)
        wu_s[...] = wu_ref[0, 0].astype(BF16)
        wd_s[...] = wd_ref[0, 0].astype(BF16)

    @pl.when(active)
    def _():
        x = x_ref[...].astype(BF16)
        hg = jnp.dot(x, wg_s[...], preferred_element_type=F32)
        hu = jnp.dot(x, wu_s[...], preferred_element_type=F32)
        he = (_silu(hg) * hu).astype(BF16)
        o_ref[...] = gw_ref[...] * jnp.dot(he, wd_s[...], preferred_element_type=F32)

    @pl.when(jnp.logical_not(active))
    def _():
        o_ref[...] = jnp.zeros_like(o_ref)


def _moe_experts(tile_expert, n_active, xs, gate_w, w_gate, w_up, w_down, layer):
    P, D = xs.shape
    tm = MOE_TM

    def expert_w(shape):
        return pl.BlockSpec((1, 1) + shape, lambda i, te, na: (layer, te[i], 0, 0), pipeline_mode=pl.Buffered(1))

    grid_spec = pltpu.PrefetchScalarGridSpec(
        num_scalar_prefetch=2,
        grid=(P // tm,),
        in_specs=[pl.BlockSpec((tm, D), lambda i, te, na: (i, 0)),
                  pl.BlockSpec((tm, 1), lambda i, te, na: (i, 0)),
                  expert_w((D, D_EXPERT)), expert_w((D, D_EXPERT)), expert_w((D_EXPERT, D))],
        out_specs=pl.BlockSpec((tm, D), lambda i, te, na: (i, 0)),
        scratch_shapes=[pltpu.VMEM((D, D_EXPERT), BF16), pltpu.VMEM((D, D_EXPERT), BF16),
                        pltpu.VMEM((D_EXPERT, D), BF16)],
    )
    return pl.pallas_call(
        _moe_kernel,
        grid_spec=grid_spec,
        out_shape=jax.ShapeDtypeStruct((P, D), F32),
        compiler_params=_params("arbitrary"),
    )(tile_expert, n_active, xs, gate_w, w_gate, w_up, w_down)


def _route(logits, router_b):
    aff = jax.nn.sigmoid(logits)
    sel = (aff + router_b.astype(F32)).reshape(-1, MOE_GROUPS, EXPERTS_PER_GROUP)
    lane = lax.broadcasted_iota(jnp.int32, sel.shape, 2)

    def top2(x, ids):
        i1 = jnp.argmax(x, axis=-1)
        x2 = jnp.where(ids == i1[..., None], -jnp.inf, x)
        i2 = jnp.argmax(x2, axis=-1)
        return jnp.max(x, axis=-1), jnp.max(x2, axis=-1), i1, i2

    m1, m2, _, _ = top2(sel, lane)
    grp = jnp.argmax(m1 + m2, axis=-1)
    in_grp = jnp.take_along_axis(sel, grp[:, None, None], axis=1)[:, 0]
    _, _, l1, l2 = top2(in_grp, lane[:, 0])
    idx = grp[:, None] * EXPERTS_PER_GROUP + jnp.stack([l1, l2], axis=-1)
    wts = jnp.take_along_axis(aff, idx, axis=1)
    wts = wts / jnp.sum(wts, axis=-1, keepdims=True)
    return idx.astype(jnp.int32), wts


def _moe(h, logits, router_b, w_gate, w_up, w_down, layer):
    M, D = h.shape
    tm = MOE_TM
    idx, wts = _route(logits, router_b)
    n_asg = M * MOE_TOP_K
    P = n_asg + N_EXPERTS * tm
    onehot = jax.nn.one_hot(idx.reshape(-1), N_EXPERTS, dtype=jnp.int32)
    csum = jnp.cumsum(onehot, axis=0)
    counts = csum[-1]
    pcounts = ((counts + tm - 1) // tm) * tm
    pends = jnp.cumsum(pcounts)
    pstarts = pends - pcounts
    dest = jnp.sum(onehot * (pstarts[None, :] + csum - 1), axis=1)
    fields = jnp.stack([jnp.arange(n_asg, dtype=jnp.int32) // MOE_TOP_K,
                        lax.bitcast_convert_type(wts.reshape(-1), jnp.int32)], axis=1)
    slots = jnp.zeros((P, 2), jnp.int32).at[dest].set(fields)
    rows_tok = slots[:, 0]
    gate_w = lax.bitcast_convert_type(slots[:, 1], F32)
    pos = dest.reshape(M, MOE_TOP_K)
    tile_start = jnp.arange(P // tm, dtype=jnp.int32) * tm
    tile_expert = jnp.minimum(jnp.searchsorted(pends, tile_start, side='right'), N_EXPERTS - 1).astype(jnp.int32)
    n_active = (pends[-1] // tm).astype(jnp.int32).reshape(1)
    xs = jnp.take(h, rows_tok, axis=0, mode='clip')
    y = _moe_experts(tile_expert, n_active, xs, gate_w.reshape(P, 1), w_gate, w_up, w_down, layer)
    yg = lax.optimization_barrier(jnp.take(y, pos.T.reshape(-1), axis=0, mode='clip'))
    return yg[:M] + yg[M:]


def _final_norm_kernel(x_ref, g_ref, o_ref):
    x = x_ref[...]
    o_ref[...] = x * lax.rsqrt(jnp.mean(x * x, axis=-1, keepdims=True) + RMS_EPS) * g_ref[...]


def _final_norm(x, g):
    M, D = x.shape
    tm = 512
    return pl.pallas_call(
        _final_norm_kernel,
        grid=(M // tm,),
        in_specs=[pl.BlockSpec((tm, D), lambda i: (i, 0)), pl.BlockSpec((1, D), lambda i: (0, 0))],
        out_specs=pl.BlockSpec((tm, D), lambda i: (i, 0)),
        out_shape=jax.ShapeDtypeStruct((M, D), F32),
        compiler_params=_params("arbitrary"),
    )(x, g.reshape(1, D))


def _split_cols(u, sizes):
    out, start = [], 0
    for s in sizes:
        out.append(u[..., start:start + s])
        start += s
    return out


def _regroup_w_in(w):
    z, xbc, dt, q, k, v, g, glr, rw = _split_cols(w, IN_SPLITS)
    pad = jnp.zeros((w.shape[0], P_IN_PAD - sum(IN_SPLITS)), w.dtype)
    return jnp.concatenate([rw, dt, glr, pad, z, xbc, q, k, v, g], axis=1).astype(BF16)


def _lane_row(v, offset=0):
    v = v.reshape(-1)
    return jnp.zeros((1, LANE), F32).at[0, offset:offset + v.shape[0]].set(v)


def _rows_at(m, offset):
    return jnp.zeros((LANE, m.shape[1]), F32).at[offset:offset + m.shape[0]].set(m)


def _mixer_params(p, l):
    two = range(2)
    rw = dict(
        mu=jnp.pad(p['rwkv_mu'][l], ((0, 0), (0, COL_RW[0] - P_RWKV))),
        w0=p['rwkv_w0'][l].reshape(2, 1, D_RWKV),
        w2=jnp.stack([_rows_at(p['rwkv_w2'][l][d], d * RWKV_LW) for d in two]),
        a0=p['rwkv_a0'][l].reshape(2, 1, D_RWKV),
        a2=jnp.stack([_rows_at(p['rwkv_a2'][l][d], d * RWKV_LA) for d in two]),
        g2=p['rwkv_g2'][l],
        k_k=p['rwkv_k_k'][l].reshape(1, D_RWKV),
        k_a=p['rwkv_k_a'][l].reshape(1, D_RWKV),
        r_k=p['rwkv_r_k'][l].reshape(1, D_RWKV))
    return dict(
        conv_w=p['ssd_conv_w'][l].reshape(9, SSD_CONV_CH),
        conv_b=p['ssd_conv_b'][l].reshape(1, SSD_CONV_CH),
        dt_bias=_lane_row(p['ssd_dt_bias'][l]),
        ssd_a=_lane_row(-jnp.exp(p['ssd_a_log'][l])),
        gla_up=jnp.stack([_rows_at(p['gla_gate_up'][l][d], 2 * SSD_HEADS + d * GLA_LR) for d in two]),
        gla_b=p['gla_gate_b'][l].reshape(2, 1, GLA_QK),
        rwkv=rw,
        epi=dict(dx=jnp.repeat(p['ssd_d'][l], SSD_HEAD_DIM).reshape(1, D_SSD),
                 ssd_norm_g=p['ssd_norm_g'][l].reshape(1, D_SSD),
                 gla_norm_g=p['gla_norm_g'][l].reshape(1, GLA_DV),
                 ln_g=p['rwkv_ln_g'][l].reshape(1, D_RWKV),
                 ln_b=p['rwkv_ln_b'][l].reshape(1, D_RWKV)))


def _token_mixers(u, b0, B, T, s0, mp, col_major):
    s_ssd, s_gla, s_rwkv = s0
    st_ssd = s_ssd.transpose(0, 1, 4, 2, 3).reshape(B, 2, SSD_STATE, D_SSD)
    st_gla = s_gla.transpose(0, 1, 4, 2, 3).reshape(B, 2, GLA_DV, GLA_QK)
    yf, yb, f_ssd = _ssd_scan(u, b0, B, T, mp['dt_bias'], mp['ssd_a'], st_ssd, col_major)
    gf, gb, f_gla = _gla_scan(u, b0, B, T, mp['gla_up'], mp['gla_b'], st_gla, col_major)
    rf, rb, gate, bonus, f_rwkv = _rwkv_scan(u, b0, B, T, mp['rwkv'], s_rwkv, col_major)
    mix = _epilogue(u, b0, B, T, (yf, yb), (gf, gb), (rf, rb, gate, bonus), mp['epi'], col_major)
    f_ssd = f_ssd.reshape(B, 2, SSD_STATE, SSD_HEADS, SSD_HEAD_DIM).transpose(0, 1, 3, 4, 2)
    f_gla = f_gla.reshape(B, 2, GLA_DV, GLA_HEADS, GLA_DK).transpose(0, 1, 3, 4, 2)
    return mix, (f_ssd, f_gla, f_rwkv)


def kernel(x_prompt, x_sample, state_ssd, state_gla, state_rwkv, c, c_ctx, ada_w, ada_b, norm_mix_g, norm_ffn_g,
           w_in, ssd_conv_w, ssd_conv_b, ssd_dt_bias, ssd_a_log, ssd_d, ssd_norm_g, gla_gate_up, gla_gate_b,
           gla_norm_g, rwkv_mu, rwkv_w0, rwkv_w2, rwkv_a0, rwkv_a2, rwkv_g2, rwkv_k_k, rwkv_k_a, rwkv_r_k,
           rwkv_ln_g, rwkv_ln_b, w_out, router_w, router_b, moe_w_gate, moe_w_up, moe_w_down, final_norm_g):
    p = dict(ssd_conv_w=ssd_conv_w, ssd_conv_b=ssd_conv_b, ssd_dt_bias=ssd_dt_bias, ssd_a_log=ssd_a_log,
             ssd_d=ssd_d, ssd_norm_g=ssd_norm_g, gla_gate_up=gla_gate_up, gla_gate_b=gla_gate_b,
             gla_norm_g=gla_norm_g, rwkv_mu=rwkv_mu, rwkv_w0=rwkv_w0, rwkv_w2=rwkv_w2, rwkv_a0=rwkv_a0,
             rwkv_a2=rwkv_a2, rwkv_g2=rwkv_g2, rwkv_k_k=rwkv_k_k, rwkv_k_a=rwkv_k_a, rwkv_r_k=rwkv_r_k,
             rwkv_ln_g=rwkv_ln_g, rwkv_ln_b=rwkv_ln_b)
    D = D_MODEL
    nb, seq = x_prompt.shape[:2]
    db, dseq = x_sample.shape[:2]
    grp = nb * seq
    assert dseq == grp and seq % CHUNK == 0 and dseq == GRID_W * GRID_W
    G = 1 + db
    x = jnp.concatenate([x_prompt.reshape(1, grp, D), x_sample], axis=0)
    cvec = jnp.concatenate([c_ctx[None, :], c, jnp.zeros((8 - G, D), F32)], axis=0)
    mods = _ada(cvec, ada_w, ada_b)[:, :G]
    router_w_pad = jnp.pad(router_w, ((0, 0), (0, LANE - N_EXPERTS)))
    zero_states = (jnp.zeros((nb, 2, SSD_HEADS, SSD_HEAD_DIM, SSD_STATE), F32),
                   jnp.zeros((nb, 2, GLA_HEADS, GLA_DK, GLA_DV), F32),
                   jnp.zeros((nb, 2, RWKV_HEADS, RWKV_N, RWKV_N), F32))
    ctx_states = []
    for l in range(DEPTH):
        sh1, sc1, g1, sh2, sc2, g2 = jnp.split(mods[l], 6, axis=-1)
        mp = _mixer_params(p, l)
        u = _inproj(x, norm_mix_g[l], sh1, sc1, _regroup_w_in(w_in[l]))
        u = _conv_inplace(u, mp['conv_w'], mp['conv_b'], 0, 1, seq, 1)
        u = _conv_inplace(u, mp['conv_w'], mp['conv_b'], 1, db, GRID_W, dseq // GRID_W)
        mix_p, new = _token_mixers(u.reshape(G * nb, seq, P_IN_PAD), 0, nb, seq, zero_states, mp, False)
        s0 = (state_ssd[:, l], state_gla[:, l], state_rwkv[:, l])
        if l % 2 == 1:
            mix_s, _ = _token_mixers(u.reshape(G, GRID_W, GRID_W * P_IN_PAD), 1, db, dseq, s0, mp, True)
        else:
            mix_s, _ = _token_mixers(u, 1, db, dseq, s0, mp, False)
        ctx_states.append(new)
        mix = jnp.concatenate([mix_p.reshape(1, grp, D), mix_s], axis=0)
        x = _outproj(mix, w_out[l].astype(BF16), x, g1)
        h, logits = _ffn_in(x, norm_ffn_g[l], sh2, sc2, router_w_pad)
        moe = _moe(h.reshape(G * grp, D), logits.reshape(G * grp, LANE)[:, :N_EXPERTS], router_b,
                   moe_w_gate, moe_w_up, moe_w_down, l)
        x = x + g2[:, None, :] * moe.reshape(G, grp, D)
    y = _final_norm(x.reshape(G * grp, D), final_norm_g).reshape(G, grp, D)
    y_prompt = y[0].reshape(nb, seq, D)
    y_sample = y[1:]
    new_ssd = jnp.stack([s[0] for s in ctx_states], axis=1)
    new_gla = jnp.stack([s[1] for s in ctx_states], axis=1)
    new_rwkv = jnp.stack([s[2] for s in ctx_states], axis=1)
    return (y_prompt, y_sample, new_ssd, new_gla, new_rwkv)
```

```python
import functools

import jax
import jax.numpy as jnp
from jax import lax
from jax.experimental import pallas as pl
from jax.experimental.pallas import tpu as pltpu

F32 = jnp.float32
BF16 = jnp.bfloat16

D_MODEL = 2048
DEPTH = 2
GRID_W = 64
SSD_HEAD_DIM = 64
D_SSD = 1024
SSD_HEADS = 16
SSD_GROUPS = 2
SSD_STATE = 128
SSD_BC = SSD_GROUPS * SSD_STATE
SSD_CONV_CH = D_SSD + 2 * SSD_BC
GLA_HEADS = 4
D_GLA = 512
GLA_DV = 128
GLA_DK = 64
GLA_QK = GLA_HEADS * GLA_DK
GLA_LR = 16
GLA_GATE_NORM = 16.0
RWKV_N = 64
D_RWKV = 512
RWKV_HEADS = 8
RWKV_LW = 64
RWKV_LA = 64
RWKV_LG = 128
P_RWKV = 3 * D_RWKV + 2 * RWKV_LW + 2 * RWKV_LA + RWKV_LG
IN_SPLITS = (D_SSD, SSD_CONV_CH, 2 * SSD_HEADS, GLA_QK, GLA_QK, D_GLA, D_GLA, 2 * GLA_LR, P_RWKV)
N_EXPERTS = 16
MOE_GROUPS = 4
EXPERTS_PER_GROUP = 4
MOE_TOP_K = 2
D_EXPERT = 1024
RMS_EPS = 1e-6
RWKV_LN_EPS = 64e-5

LANE = 128
CHUNK = 64
RWKV_INV_BASE = 8
MOE_TM = 256
VMEM_LIMIT = 56 * 1024 * 1024

P_IN_PAD = 6144
COL_RW = (2048, 0)
COL_DT = (LANE, 15)
COL_Z = (1024, 2)
COL_X = (1024, 3)
COL_BC = (512, 8)
COL_QK = (512, 9)
COL_V = (512, 10)
COL_G = (512, 11)
CONV_TILE = 256
CONV_FIRST_TILE = 12


def _dot(a, b, contract, passes):
    dn = (contract, ((), ()))
    if passes == 6:
        return lax.dot_general(a, b, dn, precision=lax.Precision.HIGHEST, preferred_element_type=F32)
    ah = a.astype(BF16)
    bh = b.astype(BF16)
    out = lax.dot_general(ah, bh, dn, preferred_element_type=F32)
    if passes == 3:
        al = (a - ah.astype(F32)).astype(BF16)
        bl = (b - bh.astype(F32)).astype(BF16)
        out = out + lax.dot_general(ah, bl, dn, preferred_element_type=F32)
        out = out + lax.dot_general(al, bh, dn, preferred_element_type=F32)
    return out


_NN = ((1,), (0,))
_NT = ((1,), (1,))
_TN = ((0,), (0,))


def _params(*sem):
    return pltpu.CompilerParams(dimension_semantics=sem, vmem_limit_bytes=VMEM_LIMIT)


def _split3(x):
    hi = x.astype(BF16)
    r1 = x - hi.astype(F32)
    mid = r1.astype(BF16)
    lo = (r1 - mid.astype(F32)).astype(BF16)
    return hi, mid, lo


def _sel_dot(sel, x, contract=_NN):
    dn = (contract, ((), ()))
    s = sel.astype(BF16)
    hi, mid, lo = _split3(x)
    out = lax.dot_general(s, hi, dn, preferred_element_type=F32)
    out = out + lax.dot_general(s, mid, dn, preferred_element_type=F32)
    return out + lax.dot_general(s, lo, dn, preferred_element_type=F32)


def _dot_sel(x, sel):
    dn = (_NN, ((), ()))
    s = sel.astype(BF16)
    hi, mid, lo = _split3(x)
    out = lax.dot_general(hi, s, dn, preferred_element_type=F32)
    out = out + lax.dot_general(mid, s, dn, preferred_element_type=F32)
    return out + lax.dot_general(lo, s, dn, preferred_element_type=F32)


def _bdot(a, b, contract=_NN):
    return lax.dot_general(a.astype(BF16), b.astype(BF16), (contract, ((), ())), preferred_element_type=F32)


def _silu(x):
    return x * jax.nn.sigmoid(x)


def _softplus(x):
    return jnp.maximum(x, 0.0) + jnp.log(1.0 + jnp.exp(-jnp.abs(x)))


def _log_sigmoid(x):
    return jnp.minimum(x, 0.0) - jnp.log(1.0 + jnp.exp(-jnp.abs(x)))


def _time_masks(d):
    row = lax.broadcasted_iota(jnp.int32, (CHUNK, CHUNK), 0)
    col = lax.broadcasted_iota(jnp.int32, (CHUNK, CHUNK), 1)
    if d == 0:
        return col <= row, col < row
    return col >= row, col > row


def _seq_specs(width_idx, b0, nc):
    w, j = width_idx
    return [pl.BlockSpec((1, CHUNK, w), lambda b, c: (b + b0, c, j)),
            pl.BlockSpec((1, CHUNK, w), lambda b, c: (b + b0, nc - 1 - c, j))]


def _conv_kernel(x_ref, w_ref, b_ref, o_ref, *, width, rows, col_major):
    x = x_ref[0]
    T = x.shape[0]
    t = lax.broadcasted_iota(jnp.int32, (T, 1), 0)
    if col_major:
        col, row = (t // rows) % width, t % rows
    else:
        col, row = t % width, (t // width) % rows
    acc = jnp.zeros_like(x) + b_ref[...]
    for dr in (-1, 0, 1):
        if rows == 1 and dr != 0:
            continue
        for dc in (-1, 0, 1):
            off = dc * rows + dr if col_major else dr * width + dc
            xs = x if off == 0 else pltpu.roll(x, (-off) % T, axis=0)
            valid = (col + dc >= 0) & (col + dc < width) & (row + dr >= 0) & (row + dr < rows)
            tap = (dr + 1) * 3 + (dc + 1)
            acc = acc + jnp.where(valid, xs, 0.0) * w_ref[tap:tap + 1, :]
    o_ref[0] = _silu(acc)


def _conv_inplace(u, w9, bias, g0, ng, width, rows, col_major=False):
    G, T, P = u.shape
    nt = SSD_CONV_CH // CONV_TILE
    blk = pl.BlockSpec((1, T, CONV_TILE), lambda g, j: (g + g0, 0, CONV_FIRST_TILE + j))
    return pl.pallas_call(
        functools.partial(_conv_kernel, width=width, rows=rows, col_major=col_major),
        grid=(ng, nt),
        in_specs=[blk,
                  pl.BlockSpec((9, CONV_TILE), lambda g, j: (0, j)),
                  pl.BlockSpec((1, CONV_TILE), lambda g, j: (0, j))],
        out_specs=blk,
        out_shape=jax.ShapeDtypeStruct(u.shape, u.dtype),
        input_output_aliases={0: 0},
        compiler_params=_params("arbitrary", "arbitrary"),
    )(u, w9, bias)


def _ssd_kernel(xf_ref, bcf_ref, dtf_ref, xb_ref, bcb_ref, dtb_ref, bias_ref, a_ref, s0_ref,
                yf_ref, yb_ref, sfin_ref, st_scr):
    c = pl.program_id(1)
    L = CHUNK
    HP = D_SSD

    @pl.when(c == 0)
    def _():
        st_scr[...] = s0_ref[0]

    rowx = lax.broadcasted_iota(jnp.int32, (L, HP), 0)
    s_of_lane = lax.broadcasted_iota(jnp.int32, (L, HP), 1) % SSD_HEAD_DIM
    e_row = lax.broadcasted_iota(jnp.int32, (LANE, HP), 0)
    e_head = lax.broadcasted_iota(jnp.int32, (LANE, HP), 1) // SSD_HEAD_DIM
    lane2 = lax.broadcasted_iota(jnp.int32, (L, LANE), 1)
    for d, (x_ref, bc_ref, dt_ref, y_ref) in enumerate(((xf_ref, bcf_ref, dtf_ref, yf_ref),
                                                        (xb_ref, bcb_ref, dtb_ref, yb_ref))):
        incl, _ = _time_masks(d)
        strict_t = (s_of_lane < rowx) if d == 0 else (s_of_lane > rowx)
        incl_t = (s_of_lane <= rowx) if d == 0 else (s_of_lane >= rowx)
        expand = e_row == d * SSD_HEADS + e_head
        dtv = _softplus(dt_ref[0] + bias_ref[...])
        dtx = _dot_sel(dtv, expand)
        dax = dtx * a_ref[d]
        cumx = _sel_dot(incl, dax)
        totx = jnp.sum(dax, axis=0, keepdims=True)
        segx = _sel_dot(incl, jnp.where(strict_t, dax, 0.0))
        decx = jnp.where(incl_t, jnp.exp(jnp.minimum(segx, 0.0)), 0.0)
        x = x_ref[0]
        xdt = x * dtx
        xend = xdt * jnp.exp(totx - cumx)
        ecum = jnp.exp(cumx)
        etot = jnp.exp(totx)
        bc = bc_ref[0]
        for g in range(SSD_GROUPS):
            bm = bc[:, g * SSD_STATE:(g + 1) * SSD_STATE]
            cm = bc[:, SSD_BC + g * SSD_STATE:SSD_BC + (g + 1) * SSD_STATE]
            sc = _bdot(cm, bm, _NT)
            sc2 = jnp.concatenate([sc, sc], axis=1)
            for i in range(SSD_HEADS // SSD_GROUPS // 2):
                ls = slice((g * 4 + i) * LANE, (g * 4 + i + 1) * LANE)
                xp = xdt[:, ls]
                xbd = jnp.concatenate([jnp.where(lane2 < SSD_HEAD_DIM, xp, 0.0),
                                       jnp.where(lane2 >= SSD_HEAD_DIM, xp, 0.0)], axis=0)
                st = st_scr[d, :, ls]
                y_ref[0, :, ls] = _bdot(decx[:, ls] * sc2, xbd) + _bdot(cm, st) * ecum[:, ls]
                st_scr[d, :, ls] = st * etot[:, ls] + _bdot(bm, xend[:, ls], _TN)

    @pl.when(c == pl.num_programs(1) - 1)
    def _():
        sfin_ref[0] = st_scr[...]


def _ssd_scan(u, b0, B, T, bias128, a_heads, s0):
    nc = T // CHUNK
    st = pl.BlockSpec((1, 2, SSD_STATE, D_SSD), lambda b, c: (b, 0, 0, 0))
    vec = pl.BlockSpec((1, LANE), lambda b, c: (0, 0))
    avec = pl.BlockSpec((2, 1, D_SSD), lambda b, c: (0, 0, 0))
    xs, bcs, dts = (_seq_specs(col, b0, nc) for col in (COL_X, COL_BC, COL_DT))
    ys = _seq_specs((D_SSD, 0), 0, nc)
    return pl.pallas_call(
        _ssd_kernel,
        grid=(B, nc),
        in_specs=[xs[0], bcs[0], dts[0], xs[1], bcs[1], dts[1], vec, avec, st],
        out_specs=[ys[0], ys[1], st],
        out_shape=[jax.ShapeDtypeStruct((B, T, D_SSD), F32), jax.ShapeDtypeStruct((B, T, D_SSD), F32),
                   jax.ShapeDtypeStruct((B, 2, SSD_STATE, D_SSD), F32)],
        scratch_shapes=[pltpu.VMEM((2, SSD_STATE, D_SSD), F32)],
        compiler_params=_params("arbitrary", "arbitrary"),
    )(u, u, u, u, u, u, bias128, a_heads, s0)


def _stack_heads(x, head_of_lane):
    return jnp.concatenate([jnp.where(head_of_lane == h, x, 0.0) for h in range(GLA_HEADS)], axis=0)


def _gla_kernel(qkf_ref, vf_ref, dtf_ref, qkb_ref, vb_ref, dtb_ref, wup_ref, gb_ref, s0_ref,
                of_ref, ob_ref, sfin_ref, st_scr):
    c = pl.program_id(1)
    L = CHUNK

    @pl.when(c == 0)
    def _():
        st_scr[...] = s0_ref[0]

    row = lax.broadcasted_iota(jnp.int32, (L, L), 0)
    col = lax.broadcasted_iota(jnp.int32, (L, L), 1)
    row1 = lax.broadcasted_iota(jnp.int32, (L, 1), 0)
    head_of_lane = lax.broadcasted_iota(jnp.int32, (L, GLA_QK), 1) // GLA_DK
    row4 = lax.broadcasted_iota(jnp.int32, (GLA_HEADS * L, L), 0) % L
    col4 = lax.broadcasted_iota(jnp.int32, (GLA_HEADS * L, L), 1)
    for d, (qk_ref, v_ref, dt_ref, o_ref) in enumerate(((qkf_ref, vf_ref, dtf_ref, of_ref),
                                                        (qkb_ref, vb_ref, dtb_ref, ob_ref))):
        incl, _ = _time_masks(d)
        flip = (lambda i: i) if d == 0 else (lambda i: L - 1 - i)
        tq, tj, tq1, tq4, tj4 = flip(row), flip(col), flip(row1), flip(row4), flip(col4)
        qk = qk_ref[0]
        q = qk[:, :GLA_QK] * GLA_DK ** -0.5
        k = qk[:, GLA_QK:]
        v = v_ref[0]
        lg = _log_sigmoid(_dot(dt_ref[0], wup_ref[d], _NN, 3) + gb_ref[d]) / GLA_GATE_NORM
        cum = _sel_dot(incl, lg)
        tot = jnp.sum(lg, axis=0, keepdims=True)
        acc = jnp.where(row4 == col4, _bdot(_stack_heads(q, head_of_lane), k, _NT), 0.0)
        halves = [L >> (i + 1) for i in range(L.bit_length() - 1)]
        pick_mid = [(tj == (tq // (2 * m)) * (2 * m) + m - 1).astype(BF16) for m in halves]
        cmid_all = _sel_dot(jnp.concatenate(pick_mid, axis=0), cum)
        for i, m in enumerate(halves):
            blk = 2 * m
            cmid = cmid_all[i * L:(i + 1) * L]
            later = (tq1 % blk) >= m
            qs = jnp.where(later, q * jnp.exp(jnp.minimum(cum - cmid, 0.0)), 0.0)
            ks = jnp.where(later, 0.0, k * jnp.exp(jnp.minimum(cmid - cum, 0.0)))
            a_lvl = _bdot(_stack_heads(qs, head_of_lane), ks, _NT)
            acc = acc + jnp.where((tq4 // blk) == (tj4 // blk), a_lvl, 0.0)
        st = st_scr[d]
        o_inter = _bdot(_stack_heads(q * jnp.exp(cum), head_of_lane), st, _NT)
        for h in range(GLA_HEADS):
            rs = slice(h * L, (h + 1) * L)
            ls = slice(h * GLA_DV, (h + 1) * GLA_DV)
            o_ref[0, :, ls] = _bdot(acc[rs], v[:, ls]) + o_inter[rs]
        vst = jnp.concatenate([v[:, h * GLA_DV:(h + 1) * GLA_DV] for h in range(GLA_HEADS)], axis=0)
        kst = _stack_heads(k * jnp.exp(tot - cum), head_of_lane)
        st_scr[d] = st * jnp.exp(tot) + _bdot(vst, kst, _TN)

    @pl.when(c == pl.num_programs(1) - 1)
    def _():
        sfin_ref[0] = st_scr[...]


def _gla_scan(u, b0, B, T, wup, gb, s0):
    nc = T // CHUNK
    st = pl.BlockSpec((1, 2, GLA_DV, GLA_QK), lambda b, c: (b, 0, 0, 0))
    qks, vs, dts = (_seq_specs(col, b0, nc) for col in (COL_QK, COL_V, COL_DT))
    os_ = _seq_specs((D_GLA, 0), 0, nc)
    return pl.pallas_call(
        _gla_kernel,
        grid=(B, nc),
        in_specs=[qks[0], vs[0], dts[0], qks[1], vs[1], dts[1],
                  pl.BlockSpec((2, LANE, GLA_QK), lambda b, c: (0, 0, 0)),
                  pl.BlockSpec((2, 1, GLA_QK), lambda b, c: (0, 0, 0)), st],
        out_specs=[os_[0], os_[1], st],
        out_shape=[jax.ShapeDtypeStruct((B, T, D_GLA), F32), jax.ShapeDtypeStruct((B, T, D_GLA), F32),
                   jax.ShapeDtypeStruct((B, 2, GLA_DV, GLA_QK), F32)],
        scratch_shapes=[pltpu.VMEM((2, GLA_DV, GLA_QK), F32)],
        compiler_params=_params("arbitrary", "arbitrary"),
    )(u, u, u, u, u, u, wup, gb, s0)


def _rwkv_kernel(xf_ref, hpf_ref, hnf_ref, xb_ref, hpb_ref, hnb_ref, mu_ref, w0_ref, w2_ref, a0_ref, a2_ref,
                 g2_ref, kkw_ref, ka_ref, rk_ref, s0_ref, of_ref, ob_ref, gate_ref, bonus_ref, sfin_ref, s_scr):
    c = pl.program_id(1)
    nc = pl.num_programs(1)
    C = CHUNK
    N = RWKV_N
    D = D_RWKV

    @pl.when(c == 0)
    def _():
        s_scr[...] = s0_ref[0]

    row = lax.broadcasted_iota(jnp.int32, (C, C), 0)
    col = lax.broadcasted_iota(jnp.int32, (C, C), 1)
    row1 = lax.broadcasted_iota(jnp.int32, (C, 1), 0)
    eye = (col == row).astype(F32)
    same_head = (lax.broadcasted_iota(jnp.int32, (D, D), 0) // N) == (lax.broadcasted_iota(jnp.int32, (D, D), 1) // N)

    def a_icl(ad, d):
        return jax.nn.sigmoid(a0_ref[d] + _dot(ad, a2_ref[d], _NN, 3))

    def k_eff(kr, a):
        return kr * (1.0 + (a - 1.0) * ka_ref[...])

    chains = []
    for d, (x_ref, hp_ref, hn_ref, o_ref) in enumerate(((xf_ref, hpf_ref, hnf_ref, of_ref),
                                                        (xb_ref, hpb_ref, hnb_ref, ob_ref))):
        cc = c if d == 0 else nc - 1 - c
        incl, strict = _time_masks(d)
        x = x_ref[0]
        prev_row = jnp.where(cc > 0, hp_ref[0][7:8, :], 0.0)
        next_row = jnp.where(cc < nc - 1, hn_ref[0][0:1, :], 0.0)
        prev = jnp.where(row1 == 0, prev_row, pltpu.roll(x, 1, axis=0))
        nxt = jnp.where(row1 == C - 1, next_row, pltpu.roll(x, C - 1, axis=0))
        xm = x + mu_ref[0:1, :] * (prev - x) + mu_ref[1:2, :] * (nxt - x)
        r = xm[:, 0:D]
        kr = xm[:, D:2 * D]
        v = xm[:, 2 * D:3 * D]
        wd = xm[:, 3 * D:3 * D + LANE]
        ad = xm[:, 3 * D + LANE:3 * D + 2 * LANE]
        w_log = _log_sigmoid(w0_ref[d] + _dot(jnp.tanh(wd), w2_ref[d], _NN, 3)) - 0.5
        lw = -jnp.exp(w_log)
        a = a_icl(ad, d)
        kk = kr * kkw_ref[...]
        kk = kk * lax.rsqrt(_dot_sel(kk * kk, same_head) + 1e-12)
        k = k_eff(kr, a)
        if d == 0:
            gd = xm[:, 3 * D + 2 * LANE:3 * D + 3 * LANE]
            gate_ref[0] = _dot(jax.nn.sigmoid(gd), g2_ref[...], _NN, 3)
            k_both = k + k_eff(kr, a_icl(ad, 1))
            bonus_ref[0] = _dot_sel(r * k_both * rk_ref[...], same_head) * v

        cum = _sel_dot(incl, lw)
        tot = jnp.sum(lw, axis=0, keepdims=True)
        b = a * kk
        kp = kk * jnp.exp(cum - lw)
        rp = r * jnp.exp(cum)
        pinv = jnp.exp(-cum)
        kinv = k * pinv
        binv = b * pinv
        pend = jnp.exp(tot - cum)
        kd = k * pend
        bd = b * pend
        ptot = jnp.exp(tot)
        for h in range(RWKV_HEADS):
            sl = slice(h * N, (h + 1) * N)
            chains.append(dict(
                d=d, h=h, sl=sl, incl=incl, strict=strict, o_ref=o_ref, v=v[:, sl], kd=kd[:, sl], bd=bd[:, sl],
                ptot=ptot[:, sl],
                x=jnp.concatenate([kp[:, sl], rp[:, sl]], axis=0).astype(BF16),
                y=jnp.concatenate([kinv[:, sl], binv[:, sl]], axis=0).astype(BF16)))

    for ch in chains:
        aa = _bdot(ch['x'], ch['y'], _NT)
        ch['a_kk'] = jnp.where(ch['strict'], aa[:C, :C], 0.0)
        ch['L'] = jnp.where(ch['strict'], aa[:C, C:], 0.0)
        ch['a_rk'] = jnp.where(ch['incl'], aa[C:, :C], 0.0)
        ch['a_rb'] = jnp.where(ch['incl'], aa[C:, C:], 0.0)
    def same_block(m):
        return (row // m) == (col // m)

    for ch in chains:
        ch['pw'] = jnp.where(same_block(RWKV_INV_BASE), ch['L'], 0.0)
        ch['t'] = eye - ch['pw']
    n = 2
    while n < RWKV_INV_BASE:
        for ch in chains:
            ch['pw'] = _bdot(ch['pw'], ch['pw'])
        for ch in chains:
            ch['t'] = ch['t'] + _bdot(ch['t'], ch['pw'])
        n *= 2
    m = RWKV_INV_BASE
    while m < C:
        off = same_block(2 * m) & jnp.logical_not(same_block(m))
        for ch in chains:
            ch['pw'] = _bdot(ch['t'], jnp.where(off, ch['L'], 0.0))
        for ch in chains:
            ch['t'] = ch['t'] - _bdot(ch['pw'], ch['t'])
        m *= 2
    for ch in chains:
        ch['s0'] = s_scr[ch['d'], ch['h']]
        ch['xs'] = _bdot(ch['x'], ch['s0'], _NT)
        ch['g'] = ch['xs'][:C] + _bdot(ch['a_kk'], ch['v'])
    for ch in chains:
        ch['u'] = _bdot(ch['t'], ch['g'])
    for ch in chains:
        o = ch['xs'][C:] + _bdot(ch['a_rk'], ch['v']) - _bdot(ch['a_rb'], ch['u'])
        ch['o_ref'][0, :, ch['sl']] = o
        s_scr[ch['d'], ch['h']] = (ch['s0'] * ch['ptot'] + _bdot(ch['v'], ch['kd'], _TN)
                                   - _bdot(ch['u'], ch['bd'], _TN))

    @pl.when(c == nc - 1)
    def _():
        sfin_ref[0] = s_scr[...]


def _rwkv_scan(u, b0, B, T, prm, s0):
    nc = T // CHUNK
    w, j = COL_RW
    xs = _seq_specs(COL_RW, b0, nc)
    sub = CHUNK // 8

    def halo(chunk_of, step):
        edge = sub - 1 if step < 0 else 0
        return pl.BlockSpec((1, 8, w),
                            lambda b, c: (b + b0, jnp.clip(chunk_of(c) + step, 0, nc - 1) * sub + edge, j))

    fwd, bwd = (lambda c: c), (lambda c: nc - 1 - c)
    halo_prev = [halo(fwd, -1), halo(bwd, -1)]
    halo_next = [halo(fwd, 1), halo(bwd, 1)]
    st = pl.BlockSpec((1, 2, RWKV_HEADS, RWKV_N, RWKV_N), lambda b, c: (b, 0, 0, 0, 0))
    os_ = _seq_specs((D_RWKV, 0), 0, nc)

    def full(a):
        return pl.BlockSpec(a.shape, lambda b, c, _n=a.ndim: (0,) * _n)

    names = ('mu', 'w0', 'w2', 'a0', 'a2', 'g2', 'k_k', 'k_a', 'r_k')
    tok = jax.ShapeDtypeStruct((B, T, D_RWKV), F32)
    return pl.pallas_call(
        _rwkv_kernel,
        grid=(B, nc),
        in_specs=[xs[0], halo_prev[0], halo_next[0], xs[1], halo_prev[1], halo_next[1]]
                 + [full(prm[n]) for n in names] + [st],
        out_specs=[os_[0], os_[1], os_[0], os_[0], st],
        out_shape=[tok, tok, tok, tok, jax.ShapeDtypeStruct((B, 2, RWKV_HEADS, RWKV_N, RWKV_N), F32)],
        scratch_shapes=[pltpu.VMEM((2, RWKV_HEADS, RWKV_N, RWKV_N), F32)],
        compiler_params=_params("arbitrary", "arbitrary"),
    )(u, u, u, u, u, u, *[prm[n] for n in names], s0)


def _epilogue_kernel(yf_ref, yb_ref, z_ref, x_ref, gf_ref, gb_ref, g_ref, rf_ref, rb_ref, gate_ref, bonus_ref,
                     dx_ref, sng_ref, gng_ref, lng_ref, lnb_ref, o_ref):
    y = (yf_ref[0] + yb_ref[0] + x_ref[0] * dx_ref[...]) * _silu(z_ref[0])
    half = D_SSD // SSD_GROUPS
    for g in range(SSD_GROUPS):
        ls = slice(g * half, (g + 1) * half)
        yy = y[:, ls]
        yy = yy * lax.rsqrt(jnp.mean(yy * yy, axis=-1, keepdims=True) + RMS_EPS)
        o_ref[0, :, ls] = (yy * sng_ref[:, ls]).astype(BF16)
    go = gf_ref[0] + gb_ref[0]
    gg = g_ref[0]
    for h in range(GLA_HEADS):
        ls = slice(h * GLA_DV, (h + 1) * GLA_DV)
        oo = go[:, ls]
        oo = oo * lax.rsqrt(jnp.mean(oo * oo, axis=-1, keepdims=True) + RMS_EPS)
        o_ref[0, :, D_SSD + h * GLA_DV:D_SSD + (h + 1) * GLA_DV] = (oo * gng_ref[...] * _silu(gg[:, ls])).astype(BF16)
    ro = rf_ref[0] + rb_ref[0]
    same_head = ((lax.broadcasted_iota(jnp.int32, (D_RWKV, D_RWKV), 0) // RWKV_N)
                 == (lax.broadcasted_iota(jnp.int32, (D_RWKV, D_RWKV), 1) // RWKV_N))
    mean = _dot_sel(ro, same_head) * (1.0 / RWKV_N)
    cen = ro - mean
    var = _dot_sel(cen * cen, same_head) * (1.0 / RWKV_N)
    o = cen * lax.rsqrt(var + RWKV_LN_EPS) * lng_ref[...] + lnb_ref[...]
    o_ref[0, :, D_SSD + D_GLA:] = ((o + bonus_ref[0]) * gate_ref[0]).astype(BF16)


def _epilogue(u, b0, B, T, ssd_y, gla_o, rwkv_o, prm):
    tm = min(T, 512)

    def tok(width_idx, off):
        w, j = width_idx
        return pl.BlockSpec((1, tm, w), lambda b, i: (b + off, i, j))

    def vec(a):
        return pl.BlockSpec(a.shape, lambda b, i: (0, 0))

    y1, g1, r1 = tok((D_SSD, 0), 0), tok((D_GLA, 0), 0), tok((D_RWKV, 0), 0)
    names = ('dx', 'ssd_norm_g', 'gla_norm_g', 'ln_g', 'ln_b')
    return pl.pallas_call(
        _epilogue_kernel,
        grid=(B, T // tm),
        in_specs=[y1, y1, tok(COL_Z, b0), tok(COL_X, b0), g1, g1, tok(COL_G, b0), r1, r1, r1, r1]
                 + [vec(prm[n]) for n in names],
        out_specs=tok((D_MODEL, 0), 0),
        out_shape=jax.ShapeDtypeStruct((B, T, D_MODEL), BF16),
        compiler_params=_params("arbitrary", "arbitrary"),
    )(ssd_y[0], ssd_y[1], u, u, gla_o[0], gla_o[1], u, rwkv_o[0], rwkv_o[1], rwkv_o[2], rwkv_o[3],
      *[prm[n] for n in names])


def _ada_kernel(c_ref, w_ref, b_ref, o_ref):
    o_ref[0] = _dot(_silu(c_ref[...]), w_ref[0], _NN, 1) + b_ref[0]


def _ada(cvec8, ada_w, ada_b):
    tn = 1536
    n = ada_w.shape[-1]
    return pl.pallas_call(
        _ada_kernel,
        grid=(DEPTH, n // tn),
        in_specs=[pl.BlockSpec((8, D_MODEL), lambda l, j: (0, 0)),
                  pl.BlockSpec((1, D_MODEL, tn), lambda l, j: (l, 0, j)),
                  pl.BlockSpec((1, 1, tn), lambda l, j: (l, 0, j))],
        out_specs=pl.BlockSpec((1, 8, tn), lambda l, j: (l, 0, j)),
        out_shape=jax.ShapeDtypeStruct((DEPTH, 8, n), F32),
        compiler_params=_params("arbitrary", "arbitrary"),
    )(cvec8, ada_w, ada_b.reshape(DEPTH, 1, n))


def _norm_mod(x, g, shift, scale):
    y = x * lax.rsqrt(jnp.mean(x * x, axis=-1, keepdims=True) + RMS_EPS)
    return (y * g) * (1.0 + scale) + shift


def _inproj_kernel(x_ref, g_ref, sh_ref, sc_ref, w_ref, o_ref, h_scr):
    @pl.when(pl.program_id(2) == 0)
    def _():
        h_scr[...] = _norm_mod(x_ref[0], g_ref[...], sh_ref[0], sc_ref[0]).astype(BF16)

    o_ref[0] = jnp.dot(h_scr[...], w_ref[...], preferred_element_type=F32)


def _inproj(x, g, shift, scale, w):
    G, R, D = x.shape
    N = w.shape[1]
    tm, tn = 1024, 1024
    vec = pl.BlockSpec((1, 1, D), lambda gi, i, j: (gi, 0, 0))
    return pl.pallas_call(
        _inproj_kernel,
        grid=(G, R // tm, N // tn),
        in_specs=[pl.BlockSpec((1, tm, D), lambda gi, i, j: (gi, i, 0)),
                  pl.BlockSpec((1, D), lambda gi, i, j: (0, 0)),
                  vec, vec,
                  pl.BlockSpec((D, tn), lambda gi, i, j: (0, j))],
        out_specs=pl.BlockSpec((1, tm, tn), lambda gi, i, j: (gi, i, j)),
        out_shape=jax.ShapeDtypeStruct((G, R, N), F32),
        scratch_shapes=[pltpu.VMEM((tm, D), BF16)],
        compiler_params=_params("arbitrary", "arbitrary", "arbitrary"),
    )(x, g.reshape(1, D), shift.reshape(G, 1, D), scale.reshape(G, 1, D), w)


def _outproj_kernel(a_ref, w_ref, x_ref, gate_ref, o_ref):
    o_ref[0] = x_ref[0] + gate_ref[0] * jnp.dot(a_ref[0], w_ref[...], preferred_element_type=F32)


def _outproj(a, w, xres, gate):
    G, R, Kd = a.shape
    N = w.shape[1]
    tm, tn = 1024, 1024
    return pl.pallas_call(
        _outproj_kernel,
        grid=(G, R // tm, N // tn),
        in_specs=[pl.BlockSpec((1, tm, Kd), lambda gi, i, j: (gi, i, 0)),
                  pl.BlockSpec((Kd, tn), lambda gi, i, j: (0, j)),
                  pl.BlockSpec((1, tm, tn), lambda gi, i, j: (gi, i, j)),
                  pl.BlockSpec((1, 1, tn), lambda gi, i, j: (gi, 0, j))],
        out_specs=pl.BlockSpec((1, tm, tn), lambda gi, i, j: (gi, i, j)),
        out_shape=jax.ShapeDtypeStruct((G, R, N), F32),
        compiler_params=_params("arbitrary", "arbitrary", "arbitrary"),
    )(a, w, xres, gate.reshape(G, 1, N))


def _ffn_in_kernel(x_ref, g_ref, sh_ref, sc_ref, rw_ref, h_ref, lg_ref):
    h = _norm_mod(x_ref[0], g_ref[...], sh_ref[0], sc_ref[0])
    h_ref[0] = h
    lg_ref[0] = _dot(h, rw_ref[...], _NN, 6)


def _ffn_in(x, g, shift, scale, router_w_pad):
    G, R, D = x.shape
    tm = 512
    vec = pl.BlockSpec((1, 1, D), lambda gi, i: (gi, 0, 0))
    return pl.pallas_call(
        _ffn_in_kernel,
        grid=(G, R // tm),
        in_specs=[pl.BlockSpec((1, tm, D), lambda gi, i: (gi, i, 0)),
                  pl.BlockSpec((1, D), lambda gi, i: (0, 0)),
                  vec, vec,
                  pl.BlockSpec((D, LANE), lambda gi, i: (0, 0))],
        out_specs=[pl.BlockSpec((1, tm, D), lambda gi, i: (gi, i, 0)),
                   pl.BlockSpec((1, tm, LANE), lambda gi, i: (gi, i, 0))],
        out_shape=[jax.ShapeDtypeStruct((G, R, D), F32), jax.ShapeDtypeStruct((G, R, LANE), F32)],
        compiler_params=_params("arbitrary", "arbitrary"),
    )(x, g.reshape(1, D), shift.reshape(G, 1, D), scale.reshape(G, 1, D), router_w_pad)


def _moe_kernel(te_ref, na_ref, x_ref, gw_ref, wg_ref, wu_ref, wd_ref, o_ref, wg_s, wu_s, wd_s):
    i = pl.program_id(0)
    active = i < na_ref[0]
    new_expert = (i == 0) | (te_ref[i] != te_ref[jnp.maximum(i - 1, 0)])

    @pl.when(active & new_expert)
    def _():
        wg_s[...] = wg_ref[0, 0].astype(BF16)
        wu_s[...] = wu_ref[0, 0].astype(BF16)
        wd_s[...] = wd_ref[0, 0].astype(BF16)

    @pl.when(active)
    def _():
        x = x_ref[...].astype(BF16)
        hg = jnp.dot(x, wg_s[...], preferred_element_type=F32)
        hu = jnp.dot(x, wu_s[...], preferred_element_type=F32)
        he = (_silu(hg) * hu).astype(BF16)
        o_ref[...] = gw_ref[...] * jnp.dot(he, wd_s[...], preferred_element_type=F32)

    @pl.when(jnp.logical_not(active))
    def _():
        o_ref[...] = jnp.zeros_like(o_ref)


def _moe_experts(tile_expert, n_active, xs, gate_w, w_gate, w_up, w_down, layer):
    P, D = xs.shape
    tm = MOE_TM

    def expert_w(shape):
        return pl.BlockSpec((1, 1) + shape, lambda i, te, na: (layer, te[i], 0, 0), pipeline_mode=pl.Buffered(1))

    grid_spec = pltpu.PrefetchScalarGridSpec(
        num_scalar_prefetch=2,
        grid=(P // tm,),
        in_specs=[pl.BlockSpec((tm, D), lambda i, te, na: (i, 0)),
                  pl.BlockSpec((tm, 1), lambda i, te, na: (i, 0)),
                  expert_w((D, D_EXPERT)), expert_w((D, D_EXPERT)), expert_w((D_EXPERT, D))],
        out_specs=pl.BlockSpec((tm, D), lambda i, te, na: (i, 0)),
        scratch_shapes=[pltpu.VMEM((D, D_EXPERT), BF16), pltpu.VMEM((D, D_EXPERT), BF16),
                        pltpu.VMEM((D_EXPERT, D), BF16)],
    )
    return pl.pallas_call(
        _moe_kernel,
        grid_spec=grid_spec,
        out_shape=jax.ShapeDtypeStruct((P, D), F32),
        compiler_params=_params("arbitrary"),
    )(tile_expert, n_active, xs, gate_w, w_gate, w_up, w_down)


def _route(logits, router_b):
    aff = jax.nn.sigmoid(logits)
    sel = (aff + router_b.astype(F32)).reshape(-1, MOE_GROUPS, EXPERTS_PER_GROUP)
    lane = lax.broadcasted_iota(jnp.int32, sel.shape, 2)

    def top2(x, ids):
        i1 = jnp.argmax(x, axis=-1)
        x2 = jnp.where(ids == i1[..., None], -jnp.inf, x)
        i2 = jnp.argmax(x2, axis=-1)
        return jnp.max(x, axis=-1), jnp.max(x2, axis=-1), i1, i2

    m1, m2, _, _ = top2(sel, lane)
    grp = jnp.argmax(m1 + m2, axis=-1)
    in_grp = jnp.take_along_axis(sel, grp[:, None, None], axis=1)[:, 0]
    _, _, l1, l2 = top2(in_grp, lane[:, 0])
    idx = grp[:, None] * EXPERTS_PER_GROUP + jnp.stack([l1, l2], axis=-1)
    wts = jnp.take_along_axis(aff, idx, axis=1)
    wts = wts / jnp.sum(wts, axis=-1, keepdims=True)
    return idx.astype(jnp.int32), wts


def _moe(h, logits, router_b, w_gate, w_up, w_down, layer):
    M, D = h.shape
    tm = MOE_TM
    idx, wts = _route(logits, router_b)
    n_asg = M * MOE_TOP_K
    P = n_asg + N_EXPERTS * tm
    onehot = jax.nn.one_hot(idx.reshape(-1), N_EXPERTS, dtype=jnp.int32)
    csum = jnp.cumsum(onehot, axis=0)
    counts = csum[-1]
    pcounts = ((counts + tm - 1) // tm) * tm
    pends = jnp.cumsum(pcounts)
    pstarts = pends - pcounts
    dest = jnp.sum(onehot * (pstarts[None, :] + csum - 1), axis=1)
    fields = jnp.stack([jnp.arange(n_asg, dtype=jnp.int32) // MOE_TOP_K,
                        lax.bitcast_convert_type(wts.reshape(-1), jnp.int32)], axis=1)
    slots = jnp.zeros((P, 2), jnp.int32).at[dest].set(fields)
    rows_tok = slots[:, 0]
    gate_w = lax.bitcast_convert_type(slots[:, 1], F32)
    pos = dest.reshape(M, MOE_TOP_K)
    tile_start = jnp.arange(P // tm, dtype=jnp.int32) * tm
    tile_expert = jnp.minimum(jnp.searchsorted(pends, tile_start, side='right'), N_EXPERTS - 1).astype(jnp.int32)
    n_active = (pends[-1] // tm).astype(jnp.int32).reshape(1)
    xs = jnp.take(h, rows_tok, axis=0, mode='clip')
    y = _moe_experts(tile_expert, n_active, xs, gate_w.reshape(P, 1), w_gate, w_up, w_down, layer)
    return lax.optimization_barrier(jnp.take(y, pos.T.reshape(-1), axis=0, mode='clip'))


def _ffn_residual_kernel(x_ref, y0_ref, y1_ref, gate_ref, o_ref):
    o_ref[0] = x_ref[0] + gate_ref[0] * (y0_ref[0, 0] + y1_ref[0, 0])


def _ffn_residual_norm_kernel(x_ref, y0_ref, y1_ref, gate_ref, g_ref, o_ref):
    x = x_ref[0] + gate_ref[0] * (y0_ref[0, 0] + y1_ref[0, 0])
    o_ref[0] = x * lax.rsqrt(jnp.mean(x * x, axis=-1, keepdims=True) + RMS_EPS) * g_ref[...]


def _ffn_residual(x, yg, gate, final_g=None):
    G, R, D = x.shape
    tm = 512
    tok = pl.BlockSpec((1, tm, D), lambda g, i: (g, i, 0))
    in_specs = [tok,
                pl.BlockSpec((1, 1, tm, D), lambda g, i: (0, g, i, 0)),
                pl.BlockSpec((1, 1, tm, D), lambda g, i: (1, g, i, 0)),
                pl.BlockSpec((1, 1, D), lambda g, i: (g, 0, 0))]
    args = [x, yg, yg, gate.reshape(G, 1, D)]
    if final_g is not None:
        in_specs.append(pl.BlockSpec((1, D), lambda g, i: (0, 0)))
        args.append(final_g.reshape(1, D))
    return pl.pallas_call(
        _ffn_residual_kernel if final_g is None else _ffn_residual_norm_kernel,
        grid=(G, R // tm),
        in_specs=in_specs,
        out_specs=tok,
        out_shape=jax.ShapeDtypeStruct((G, R, D), F32),
        compiler_params=_params("arbitrary", "arbitrary"),
    )(*args)


def _split_cols(u, sizes):
    out, start = [], 0
    for s in sizes:
        out.append(u[..., start:start + s])
        start += s
    return out


def _regroup_w_in(w):
    z, xbc, dt, q, k, v, g, glr, rw = _split_cols(w, IN_SPLITS)
    pad = jnp.zeros((w.shape[0], P_IN_PAD - sum(IN_SPLITS)), w.dtype)
    return jnp.concatenate([rw, dt, glr, pad, z, xbc, q, k, v, g], axis=1).astype(BF16)


def _lane_row(v, offset=0):
    v = v.reshape(-1)
    return jnp.zeros((1, LANE), F32).at[0, offset:offset + v.shape[0]].set(v)


def _rows_at(m, offset):
    return jnp.zeros((LANE, m.shape[1]), F32).at[offset:offset + m.shape[0]].set(m)


def _mixer_params(p, l):
    two = range(2)
    rw = dict(
        mu=jnp.pad(p['rwkv_mu'][l], ((0, 0), (0, COL_RW[0] - P_RWKV))),
        w0=p['rwkv_w0'][l].reshape(2, 1, D_RWKV),
        w2=jnp.stack([_rows_at(p['rwkv_w2'][l][d], d * RWKV_LW) for d in two]),
        a0=p['rwkv_a0'][l].reshape(2, 1, D_RWKV),
        a2=jnp.stack([_rows_at(p['rwkv_a2'][l][d], d * RWKV_LA) for d in two]),
        g2=p['rwkv_g2'][l],
        k_k=p['rwkv_k_k'][l].reshape(1, D_RWKV),
        k_a=p['rwkv_k_a'][l].reshape(1, D_RWKV),
        r_k=p['rwkv_r_k'][l].reshape(1, D_RWKV))
    return dict(
        conv_w=p['ssd_conv_w'][l].reshape(9, SSD_CONV_CH),
        conv_b=p['ssd_conv_b'][l].reshape(1, SSD_CONV_CH),
        dt_bias=_lane_row(p['ssd_dt_bias'][l]),
        ssd_a=jnp.repeat(-jnp.exp(p['ssd_a_log'][l]), SSD_HEAD_DIM, axis=1).reshape(2, 1, D_SSD),
        gla_up=jnp.stack([_rows_at(p['gla_gate_up'][l][d], 2 * SSD_HEADS + d * GLA_LR) for d in two]),
        gla_b=p['gla_gate_b'][l].reshape(2, 1, GLA_QK),
        rwkv=rw,
        epi=dict(dx=jnp.repeat(p['ssd_d'][l], SSD_HEAD_DIM).reshape(1, D_SSD),
                 ssd_norm_g=p['ssd_norm_g'][l].reshape(1, D_SSD),
                 gla_norm_g=p['gla_norm_g'][l].reshape(1, GLA_DV),
                 ln_g=p['rwkv_ln_g'][l].reshape(1, D_RWKV),
                 ln_b=p['rwkv_ln_b'][l].reshape(1, D_RWKV)))


def _token_mixers(u, b0, B, T, s0, mp):
    s_ssd, s_gla, s_rwkv = s0
    st_ssd = s_ssd.transpose(0, 1, 4, 2, 3).reshape(B, 2, SSD_STATE, D_SSD)
    st_gla = s_gla.transpose(0, 1, 4, 2, 3).reshape(B, 2, GLA_DV, GLA_QK)
    yf, yb, f_ssd = _ssd_scan(u, b0, B, T, mp['dt_bias'], mp['ssd_a'], st_ssd)
    gf, gb, f_gla = _gla_scan(u, b0, B, T, mp['gla_up'], mp['gla_b'], st_gla)
    rf, rb, gate, bonus, f_rwkv = _rwkv_scan(u, b0, B, T, mp['rwkv'], s_rwkv)
    mix = _epilogue(u, b0, B, T, (yf, yb), (gf, gb), (rf, rb, gate, bonus), mp['epi'])
    f_ssd = f_ssd.reshape(B, 2, SSD_STATE, SSD_HEADS, SSD_HEAD_DIM).transpose(0, 1, 3, 4, 2)
    f_gla = f_gla.reshape(B, 2, GLA_DV, GLA_HEADS, GLA_DK).transpose(0, 1, 3, 4, 2)
    return mix, (f_ssd, f_gla, f_rwkv)


def kernel(x_prompt, x_sample, state_ssd, state_gla, state_rwkv, c, c_ctx, ada_w, ada_b, norm_mix_g, norm_ffn_g,
           w_in, ssd_conv_w, ssd_conv_b, ssd_dt_bias, ssd_a_log, ssd_d, ssd_norm_g, gla_gate_up, gla_gate_b,
           gla_norm_g, rwkv_mu, rwkv_w0, rwkv_w2, rwkv_a0, rwkv_a2, rwkv_g2, rwkv_k_k, rwkv_k_a, rwkv_r_k,
           rwkv_ln_g, rwkv_ln_b, w_out, router_w, router_b, moe_w_gate, moe_w_up, moe_w_down, final_norm_g):
    p = dict(ssd_conv_w=ssd_conv_w, ssd_conv_b=ssd_conv_b, ssd_dt_bias=ssd_dt_bias, ssd_a_log=ssd_a_log,
             ssd_d=ssd_d, ssd_norm_g=ssd_norm_g, gla_gate_up=gla_gate_up, gla_gate_b=gla_gate_b,
             gla_norm_g=gla_norm_g, rwkv_mu=rwkv_mu, rwkv_w0=rwkv_w0, rwkv_w2=rwkv_w2, rwkv_a0=rwkv_a0,
             rwkv_a2=rwkv_a2, rwkv_g2=rwkv_g2, rwkv_k_k=rwkv_k_k, rwkv_k_a=rwkv_k_a, rwkv_r_k=rwkv_r_k,
             rwkv_ln_g=rwkv_ln_g, rwkv_ln_b=rwkv_ln_b)
    D = D_MODEL
    nb, seq = x_prompt.shape[:2]
    db, dseq = x_sample.shape[:2]
    grp = nb * seq
    assert dseq == grp and seq % CHUNK == 0 and dseq == GRID_W * GRID_W
    G = 1 + db
    x = jnp.concatenate([x_prompt.reshape(1, grp, D), x_sample], axis=0)
    cvec = jnp.concatenate([c_ctx[None, :], c, jnp.zeros((8 - G, D), F32)], axis=0)
    mods = _ada(cvec, ada_w, ada_b)[:, :G]
    router_w_pad = jnp.pad(router_w, ((0, 0), (0, LANE - N_EXPERTS)))
    zero_states = (jnp.zeros((nb, 2, SSD_HEADS, SSD_HEAD_DIM, SSD_STATE), F32),
                   jnp.zeros((nb, 2, GLA_HEADS, GLA_DK, GLA_DV), F32),
                   jnp.zeros((nb, 2, RWKV_HEADS, RWKV_N, RWKV_N), F32))
    def regrid(t):
        lat = t[1:].reshape(db, GRID_W, GRID_W, D).swapaxes(1, 2).reshape(db, dseq, D)
        return jnp.concatenate([t[:1], lat], axis=0)

    ctx_states = []
    col_major = False
    for l in range(DEPTH):
        sh1, sc1, g1, sh2, sc2, g2 = jnp.split(mods[l], 6, axis=-1)
        mp = _mixer_params(p, l)
        if col_major != (l % 2 == 1):
            x, col_major = regrid(x), not col_major
        u = _inproj(x, norm_mix_g[l], sh1, sc1, _regroup_w_in(w_in[l]))
        u = _conv_inplace(u, mp['conv_w'], mp['conv_b'], 0, 1, seq, 1)
        u = _conv_inplace(u, mp['conv_w'], mp['conv_b'], 1, db, GRID_W, dseq // GRID_W, col_major)
        mix_p, new = _token_mixers(u.reshape(G * nb, seq, P_IN_PAD), 0, nb, seq, zero_states, mp)
        s0 = (state_ssd[:, l], state_gla[:, l], state_rwkv[:, l])
        mix_s, _ = _token_mixers(u, 1, db, dseq, s0, mp)
        ctx_states.append(new)
        mix = jnp.concatenate([mix_p.reshape(1, grp, D), mix_s], axis=0)
        x = _outproj(mix, w_out[l].astype(BF16), x, g1)
        h, logits = _ffn_in(x, norm_ffn_g[l], sh2, sc2, router_w_pad)
        moe = _moe(h.reshape(G * grp, D), logits.reshape(G * grp, LANE)[:, :N_EXPERTS], router_b,
                   moe_w_gate, moe_w_up, moe_w_down, l)
        x = _ffn_residual(x, moe.reshape(MOE_TOP_K, G, grp, D), g2, final_norm_g if l == DEPTH - 1 else None)
    y = x
    if col_major:
        y = regrid(y)
    y_prompt = y[0].reshape(nb, seq, D)
    y_sample = y[1:]
    new_ssd = jnp.stack([s[0] for s in ctx_states], axis=1)
    new_gla = jnp.stack([s[1] for s in ctx_states], axis=1)
    new_rwkv = jnp.stack([s[2] for s in ctx_states], axis=1)
    return (y_prompt, y_sample, new_ssd, new_gla, new_rwkv)
```

```python
import functools

import jax
import jax.numpy as jnp
from jax import lax
from jax.experimental import pallas as pl
from jax.experimental.pallas import tpu as pltpu

F32 = jnp.float32
BF16 = jnp.bfloat16

D_MODEL = 2048
DEPTH = 2
GRID_W = 64
SSD_HEAD_DIM = 64
D_SSD = 1024
SSD_HEADS = 16
SSD_GROUPS = 2
SSD_STATE = 128
SSD_BC = SSD_GROUPS * SSD_STATE
SSD_CONV_CH = D_SSD + 2 * SSD_BC
GLA_HEADS = 4
D_GLA = 512
GLA_DV = 128
GLA_DK = 64
GLA_QK = GLA_HEADS * GLA_DK
GLA_LR = 16
GLA_GATE_NORM = 16.0
RWKV_N = 64
D_RWKV = 512
RWKV_HEADS = 8
RWKV_LW = 64
RWKV_LA = 64
RWKV_LG = 128
P_RWKV = 3 * D_RWKV + 2 * RWKV_LW + 2 * RWKV_LA + RWKV_LG
IN_SPLITS = (D_SSD, SSD_CONV_CH, 2 * SSD_HEADS, GLA_QK, GLA_QK, D_GLA, D_GLA, 2 * GLA_LR, P_RWKV)
N_EXPERTS = 16
MOE_GROUPS = 4
EXPERTS_PER_GROUP = 4
MOE_TOP_K = 2
D_EXPERT = 1024
RMS_EPS = 1e-6
RWKV_LN_EPS = 64e-5

LANE = 128
CHUNK = 64
RWKV_INV_BASE = 8
MOE_TM = 256
VMEM_LIMIT = 56 * 1024 * 1024

P_IN_PAD = 6144
COL_RW = (2048, 0)
COL_DT = (LANE, 15)
COL_Z = (1024, 2)
COL_X = (1024, 3)
COL_BC = (512, 8)
COL_QK = (512, 9)
COL_V = (512, 10)
COL_G = (512, 11)
CONV_TILE = 256
CONV_FIRST_TILE = 12


def _dot(a, b, contract, passes):
    dn = (contract, ((), ()))
    if passes == 6:
        return lax.dot_general(a, b, dn, precision=lax.Precision.HIGHEST, preferred_element_type=F32)
    ah = a.astype(BF16)
    bh = b.astype(BF16)
    out = lax.dot_general(ah, bh, dn, preferred_element_type=F32)
    if passes == 3:
        al = (a - ah.astype(F32)).astype(BF16)
        bl = (b - bh.astype(F32)).astype(BF16)
        out = out + lax.dot_general(ah, bl, dn, preferred_element_type=F32)
        out = out + lax.dot_general(al, bh, dn, preferred_element_type=F32)
    return out


_NN = ((1,), (0,))
_NT = ((1,), (1,))
_TN = ((0,), (0,))


def _params(*sem):
    return pltpu.CompilerParams(dimension_semantics=sem, vmem_limit_bytes=VMEM_LIMIT)


def _split3(x):
    hi = x.astype(BF16)
    r1 = x - hi.astype(F32)
    mid = r1.astype(BF16)
    lo = (r1 - mid.astype(F32)).astype(BF16)
    return hi, mid, lo


def _sel_dot(sel, x, contract=_NN):
    dn = (contract, ((), ()))
    s = sel.astype(BF16)
    hi, mid, lo = _split3(x)
    out = lax.dot_general(s, hi, dn, preferred_element_type=F32)
    out = out + lax.dot_general(s, mid, dn, preferred_element_type=F32)
    return out + lax.dot_general(s, lo, dn, preferred_element_type=F32)


def _dot_sel(x, sel):
    dn = (_NN, ((), ()))
    s = sel.astype(BF16)
    hi, mid, lo = _split3(x)
    out = lax.dot_general(hi, s, dn, preferred_element_type=F32)
    out = out + lax.dot_general(mid, s, dn, preferred_element_type=F32)
    return out + lax.dot_general(lo, s, dn, preferred_element_type=F32)


def _bdot(a, b, contract=_NN):
    return lax.dot_general(a.astype(BF16), b.astype(BF16), (contract, ((), ())), preferred_element_type=F32)


def _silu(x):
    return x * jax.nn.sigmoid(x)


def _softplus(x):
    return jnp.maximum(x, 0.0) + jnp.log(1.0 + jnp.exp(-jnp.abs(x)))


def _log_sigmoid(x):
    return jnp.minimum(x, 0.0) - jnp.log(1.0 + jnp.exp(-jnp.abs(x)))


def _time_masks(d):
    row = lax.broadcasted_iota(jnp.int32, (CHUNK, CHUNK), 0)
    col = lax.broadcasted_iota(jnp.int32, (CHUNK, CHUNK), 1)
    if d == 0:
        return col <= row, col < row
    return col >= row, col > row


def _seq_specs(width_idx, b0, nc):
    w, j = width_idx
    return [pl.BlockSpec((1, CHUNK, w), lambda b, c: (b + b0, c, j)),
            pl.BlockSpec((1, CHUNK, w), lambda b, c: (b + b0, nc - 1 - c, j))]


def _conv_kernel(x_ref, w_ref, b_ref, o_ref, *, width, rows, col_major):
    x = x_ref[0]
    T = x.shape[0]
    t = lax.broadcasted_iota(jnp.int32, (T, 1), 0)
    if col_major:
        col, row = (t // rows) % width, t % rows
    else:
        col, row = t % width, (t // width) % rows
    acc = jnp.zeros_like(x) + b_ref[...]
    for dr in (-1, 0, 1):
        if rows == 1 and dr != 0:
            continue
        for dc in (-1, 0, 1):
            off = dc * rows + dr if col_major else dr * width + dc
            xs = x if off == 0 else pltpu.roll(x, (-off) % T, axis=0)
            valid = (col + dc >= 0) & (col + dc < width) & (row + dr >= 0) & (row + dr < rows)
            tap = (dr + 1) * 3 + (dc + 1)
            acc = acc + jnp.where(valid, xs, 0.0) * w_ref[tap:tap + 1, :]
    o_ref[0] = _silu(acc)


def _conv_inplace(u, w9, bias, g0, ng, width, rows, col_major=False):
    G, T, P = u.shape
    nt = SSD_CONV_CH // CONV_TILE
    blk = pl.BlockSpec((1, T, CONV_TILE), lambda g, j: (g + g0, 0, CONV_FIRST_TILE + j))
    return pl.pallas_call(
        functools.partial(_conv_kernel, width=width, rows=rows, col_major=col_major),
        grid=(ng, nt),
        in_specs=[blk,
                  pl.BlockSpec((9, CONV_TILE), lambda g, j: (0, j)),
                  pl.BlockSpec((1, CONV_TILE), lambda g, j: (0, j))],
        out_specs=blk,
        out_shape=jax.ShapeDtypeStruct(u.shape, u.dtype),
        input_output_aliases={0: 0},
        compiler_params=_params("arbitrary", "arbitrary"),
    )(u, w9, bias)


def _ssd_kernel(xf_ref, bcf_ref, dtf_ref, xb_ref, bcb_ref, dtb_ref, bias_ref, a_ref, ex_ref, s0_ref,
                yf_ref, yb_ref, sfin_ref, st_scr):
    c = pl.program_id(1)
    L = CHUNK
    HP = D_SSD

    @pl.when(c == 0)
    def _():
        st_scr[...] = s0_ref[0]

    rowx = lax.broadcasted_iota(jnp.int32, (L, HP), 0)
    s_of_lane = lax.broadcasted_iota(jnp.int32, (L, HP), 1) % SSD_HEAD_DIM
    lane2 =lax.broadcasted_iota(jnp.int32, (L, LANE), 1)
    for d, (x_ref, bc_ref, dt_ref, y_ref) in enumerate(((xf_ref, bcf_ref, dtf_ref, yf_ref),
                                                        (xb_ref, bcb_ref, dtb_ref, yb_ref))):
        incl, _ = _time_masks(d)
        strict_t = (s_of_lane < rowx) if d == 0 else (s_of_lane > rowx)
        incl_t = (s_of_lane <= rowx) if d == 0 else (s_of_lane >= rowx)
        expand = ex_ref[d]
        dtv = _softplus(dt_ref[0] + bias_ref[...])
        dtx = _dot_sel(dtv, expand)
        dax = dtx * a_ref[d]
        cumx = _sel_dot(incl, dax)
        totx = jnp.sum(dax, axis=0, keepdims=True)
        segx = _sel_dot(incl, jnp.where(strict_t, dax, 0.0))
        decx = jnp.where(incl_t, jnp.exp(jnp.minimum(segx, 0.0)), 0.0)
        x = x_ref[0]
        xdt = x * dtx
        xend = xdt * jnp.exp(totx - cumx)
        ecum = jnp.exp(cumx)
        etot = jnp.exp(totx)
        bc = bc_ref[0]
        for g in range(SSD_GROUPS):
            bm = bc[:, g * SSD_STATE:(g + 1) * SSD_STATE]
            cm = bc[:, SSD_BC + g * SSD_STATE:SSD_BC + (g + 1) * SSD_STATE]
            sc = _bdot(cm, bm, _NT)
            sc2 = jnp.concatenate([sc, sc], axis=1)
            for i in range(SSD_HEADS // SSD_GROUPS // 2):
                ls = slice((g * 4 + i) * LANE, (g * 4 + i + 1) * LANE)
                xp = xdt[:, ls]
                xbd = jnp.concatenate([jnp.where(lane2 < SSD_HEAD_DIM, xp, 0.0),
                                       jnp.where(lane2 >= SSD_HEAD_DIM, xp, 0.0)], axis=0)
                st = st_scr[d, :, ls]
                y_ref[0, :, ls] = _bdot(decx[:, ls] * sc2, xbd) + _bdot(cm, st) * ecum[:, ls]
                st_scr[d, :, ls] = st * etot[:, ls] + _bdot(bm, xend[:, ls], _TN)

    @pl.when(c == pl.num_programs(1) - 1)
    def _():
        sfin_ref[0] = st_scr[...]


def _ssd_scan(u, b0, B, T, bias128, a_heads, expand, s0):
    nc = T // CHUNK
    st = pl.BlockSpec((1, 2, SSD_STATE, D_SSD), lambda b, c: (b, 0, 0, 0))
    vec = pl.BlockSpec((1, LANE), lambda b, c: (0, 0))
    avec = pl.BlockSpec((2, 1, D_SSD), lambda b, c: (0, 0, 0))
    exm = pl.BlockSpec((2, LANE, D_SSD), lambda b, c: (0, 0, 0))
    xs, bcs, dts = (_seq_specs(col, b0, nc) for col in (COL_X, COL_BC, COL_DT))
    ys = _seq_specs((D_SSD, 0), 0, nc)
    return pl.pallas_call(
        _ssd_kernel,
        grid=(B, nc),
        in_specs=[xs[0], bcs[0], dts[0], xs[1], bcs[1], dts[1], vec, avec, exm, st],
        out_specs=[ys[0], ys[1], st],
        out_shape=[jax.ShapeDtypeStruct((B, T, D_SSD), F32), jax.ShapeDtypeStruct((B, T, D_SSD), F32),
                   jax.ShapeDtypeStruct((B, 2, SSD_STATE, D_SSD), F32)],
        scratch_shapes=[pltpu.VMEM((2, SSD_STATE, D_SSD), F32)],
        compiler_params=_params("arbitrary", "arbitrary"),
    )(u, u, u, u, u, u, bias128, a_heads, expand, s0)


def _stack_heads(x, head_of_lane):
    return jnp.concatenate([jnp.where(head_of_lane == h, x, 0.0) for h in range(GLA_HEADS)], axis=0)


def _gla_kernel(qkf_ref, vf_ref, dtf_ref, qkb_ref, vb_ref, dtb_ref, wup_ref, gb_ref, pm_ref, s0_ref,
                of_ref, ob_ref, sfin_ref, st_scr):
    c = pl.program_id(1)
    L = CHUNK

    @pl.when(c == 0)
    def _():
        st_scr[...] = s0_ref[0]

    row = lax.broadcasted_iota(jnp.int32, (L, L), 0)
    col = lax.broadcasted_iota(jnp.int32, (L, L), 1)
    row1 = lax.broadcasted_iota(jnp.int32, (L, 1), 0)
    head_of_lane = lax.broadcasted_iota(jnp.int32, (L, GLA_QK), 1) // GLA_DK
    row4 = lax.broadcasted_iota(jnp.int32, (GLA_HEADS * L, L), 0) % L
    col4 = lax.broadcasted_iota(jnp.int32, (GLA_HEADS * L, L), 1)
    for d, (qk_ref, v_ref, dt_ref, o_ref) in enumerate(((qkf_ref, vf_ref, dtf_ref, of_ref),
                                                        (qkb_ref, vb_ref, dtb_ref, ob_ref))):
        incl, _ = _time_masks(d)
        flip = (lambda i: i) if d == 0 else (lambda i: L - 1 - i)
        tq, tj, tq1, tq4, tj4 = flip(row), flip(col), flip(row1), flip(row4), flip(col4)
        qk = qk_ref[0]
        q = qk[:, :GLA_QK] * GLA_DK ** -0.5
        k = qk[:, GLA_QK:]
        v = v_ref[0]
        lg = _log_sigmoid(_dot(dt_ref[0], wup_ref[d], _NN, 3) + gb_ref[d]) / GLA_GATE_NORM
        cum = _sel_dot(incl, lg)
        tot = jnp.sum(lg, axis=0, keepdims=True)
        acc = jnp.where(row4 == col4, _bdot(_stack_heads(q, head_of_lane), k, _NT), 0.0)
        halves = [L >> (i + 1) for i in range(L.bit_length() - 1)]
        cmid_all = _sel_dot(pm_ref[d], cum)
        for i, m in enumerate(halves):
            blk = 2 * m
            cmid = cmid_all[i * L:(i + 1) * L]
            later = (tq1 % blk) >= m
            qs = jnp.where(later, q * jnp.exp(jnp.minimum(cum - cmid, 0.0)), 0.0)
            ks = jnp.where(later, 0.0, k * jnp.exp(jnp.minimum(cmid - cum, 0.0)))
            a_lvl = _bdot(_stack_heads(qs, head_of_lane), ks, _NT)
            acc = acc + jnp.where((tq4 // blk) == (tj4 // blk), a_lvl, 0.0)
        st = st_scr[d]
        o_inter = _bdot(_stack_heads(q * jnp.exp(cum), head_of_lane), st, _NT)
        for h in range(GLA_HEADS):
            rs = slice(h * L, (h + 1) * L)
            ls = slice(h * GLA_DV, (h + 1) * GLA_DV)
            o_ref[0, :, ls] = _bdot(acc[rs], v[:, ls]) + o_inter[rs]
        vst = jnp.concatenate([v[:, h * GLA_DV:(h + 1) * GLA_DV] for h in range(GLA_HEADS)], axis=0)
        kst = _stack_heads(k * jnp.exp(tot - cum), head_of_lane)
        st_scr[d] = st * jnp.exp(tot) + _bdot(vst, kst, _TN)

    @pl.when(c == pl.num_programs(1) - 1)
    def _():
        sfin_ref[0] = st_scr[...]


def _gla_scan(u, b0, B, T, wup, gb, pick_mid, s0):
    nc = T // CHUNK
    st = pl.BlockSpec((1, 2, GLA_DV, GLA_QK), lambda b, c: (b, 0, 0, 0))
    qks, vs, dts = (_seq_specs(col, b0, nc) for col in (COL_QK, COL_V, COL_DT))
    os_ = _seq_specs((D_GLA, 0), 0, nc)
    return pl.pallas_call(
        _gla_kernel,
        grid=(B, nc),
        in_specs=[qks[0], vs[0], dts[0], qks[1], vs[1], dts[1],
                  pl.BlockSpec((2, LANE, GLA_QK), lambda b, c: (0, 0, 0)),
                  pl.BlockSpec((2, 1, GLA_QK), lambda b, c: (0, 0, 0)),
                  pl.BlockSpec(pick_mid.shape, lambda b, c: (0, 0, 0)), st],
        out_specs=[os_[0], os_[1], st],
        out_shape=[jax.ShapeDtypeStruct((B, T, D_GLA), F32), jax.ShapeDtypeStruct((B, T, D_GLA), F32),
                   jax.ShapeDtypeStruct((B, 2, GLA_DV, GLA_QK), F32)],
        scratch_shapes=[pltpu.VMEM((2, GLA_DV, GLA_QK), F32)],
        compiler_params=_params("arbitrary", "arbitrary"),
    )(u, u, u, u, u, u, wup, gb, pick_mid, s0)


def _rwkv_kernel(xf_ref, hpf_ref, hnf_ref, xb_ref, hpb_ref, hnb_ref, mu_ref, w0_ref, w2_ref, a0_ref, a2_ref,
                 g2_ref, kkw_ref, ka_ref, rk_ref, s0_ref, of_ref, ob_ref, gate_ref, bonus_ref, sfin_ref, s_scr):
    c = pl.program_id(1)
    nc = pl.num_programs(1)
    C = CHUNK
    N = RWKV_N
    D = D_RWKV

    @pl.when(c == 0)
    def _():
        s_scr[...] = s0_ref[0]

    row = lax.broadcasted_iota(jnp.int32, (C, C), 0)
    col = lax.broadcasted_iota(jnp.int32, (C, C), 1)
    row1 = lax.broadcasted_iota(jnp.int32, (C, 1), 0)
    eye = (col == row).astype(F32)
    same_head = (lax.broadcasted_iota(jnp.int32, (D, D), 0) // N) == (lax.broadcasted_iota(jnp.int32, (D, D), 1) // N)

    def a_icl(ad, d):
        return jax.nn.sigmoid(a0_ref[d] + _dot(ad, a2_ref[d], _NN, 3))

    def k_eff(kr, a):
        return kr * (1.0 + (a - 1.0) * ka_ref[...])

    chains = []
    for d, (x_ref, hp_ref, hn_ref, o_ref) in enumerate(((xf_ref, hpf_ref, hnf_ref, of_ref),
                                                        (xb_ref, hpb_ref, hnb_ref, ob_ref))):
        cc = c if d == 0 else nc - 1 - c
        incl, strict = _time_masks(d)
        x = x_ref[0]
        prev_row = jnp.where(cc > 0, hp_ref[0][7:8, :], 0.0)
        next_row = jnp.where(cc < nc - 1, hn_ref[0][0:1, :], 0.0)
        prev = jnp.where(row1 == 0, prev_row, pltpu.roll(x, 1, axis=0))
        nxt = jnp.where(row1 == C - 1, next_row, pltpu.roll(x, C - 1, axis=0))
        xm = x + mu_ref[0:1, :] * (prev - x) + mu_ref[1:2, :] * (nxt - x)
        r = xm[:, 0:D]
        kr = xm[:, D:2 * D]
        v = xm[:, 2 * D:3 * D]
        wd = xm[:, 3 * D:3 * D + LANE]
        ad = xm[:, 3 * D + LANE:3 * D + 2 * LANE]
        w_log = _log_sigmoid(w0_ref[d] + _dot(jnp.tanh(wd), w2_ref[d], _NN, 3)) - 0.5
        lw = -jnp.exp(w_log)
        a = a_icl(ad, d)
        kk = kr * kkw_ref[...]
        kk = kk * lax.rsqrt(_dot_sel(kk * kk, same_head) + 1e-12)
        k = k_eff(kr, a)
        if d == 0:
            gd = xm[:, 3 * D + 2 * LANE:3 * D + 3 * LANE]
            gate_ref[0] = _dot(jax.nn.sigmoid(gd), g2_ref[...], _NN, 3)
            k_both = k + k_eff(kr, a_icl(ad, 1))
            bonus_ref[0] = _dot_sel(r * k_both * rk_ref[...], same_head) * v

        cum = _sel_dot(incl, lw)
        tot = jnp.sum(lw, axis=0, keepdims=True)
        b = a * kk
        kp = kk * jnp.exp(cum - lw)
        rp = r * jnp.exp(cum)
        pinv = jnp.exp(-cum)
        kinv = k * pinv
        binv = b * pinv
        pend = jnp.exp(tot - cum)
        kd = k * pend
        bd = b * pend
        ptot = jnp.exp(tot)
        for h in range(RWKV_HEADS):
            sl = slice(h * N, (h + 1) * N)
            chains.append(dict(
                d=d, h=h, sl=sl, incl=incl, strict=strict, o_ref=o_ref, v=v[:, sl], kd=kd[:, sl], bd=bd[:, sl],
                ptot=ptot[:, sl],
                x=jnp.concatenate([kp[:, sl], rp[:, sl]], axis=0).astype(BF16),
                y=jnp.concatenate([kinv[:, sl], binv[:, sl]], axis=0).astype(BF16)))

    for ch in chains:
        aa = _bdot(ch['x'], ch['y'], _NT)
        ch['a_kk'] = jnp.where(ch['strict'], aa[:C, :C], 0.0)
        ch['L'] = jnp.where(ch['strict'], aa[:C, C:], 0.0)
        ch['a_rk'] = jnp.where(ch['incl'], aa[C:, :C], 0.0)
        ch['a_rb'] = jnp.where(ch['incl'], aa[C:, C:], 0.0)
    def same_block(m):
        return (row // m) == (col // m)

    for ch in chains:
        ch['pw'] = jnp.where(same_block(RWKV_INV_BASE), ch['L'], 0.0)
        ch['t'] = eye - ch['pw']
    n = 2
    while n < RWKV_INV_BASE:
        for ch in chains:
            ch['pw'] = _bdot(ch['pw'], ch['pw'])
        for ch in chains:
            ch['t'] = ch['t'] + _bdot(ch['t'], ch['pw'])
        n *= 2
    m = RWKV_INV_BASE
    while m < C:
        off = same_block(2 * m) & jnp.logical_not(same_block(m))
        for ch in chains:
            ch['pw'] = _bdot(ch['t'], jnp.where(off, ch['L'], 0.0))
        for ch in chains:
            ch['t'] = ch['t'] - _bdot(ch['pw'], ch['t'])
        m *= 2
    for ch in chains:
        ch['s0'] = s_scr[ch['d'], ch['h']]
        ch['xs'] = _bdot(ch['x'], ch['s0'], _NT)
        ch['g'] = ch['xs'][:C] + _bdot(ch['a_kk'], ch['v'])
    for ch in chains:
        ch['u'] = _bdot(ch['t'], ch['g'])
    for ch in chains:
        o = ch['xs'][C:] + _bdot(ch['a_rk'], ch['v']) - _bdot(ch['a_rb'], ch['u'])
        ch['o_ref'][0, :, ch['sl']] = o
        s_scr[ch['d'], ch['h']] = (ch['s0'] * ch['ptot'] + _bdot(ch['v'], ch['kd'], _TN)
                                   - _bdot(ch['u'], ch['bd'], _TN))

    @pl.when(c == nc - 1)
    def _():
        sfin_ref[0] = s_scr[...]


def _rwkv_scan(u, b0, B, T, prm, s0):
    nc = T // CHUNK
    w, j = COL_RW
    xs = _seq_specs(COL_RW, b0, nc)
    sub = CHUNK // 8

    def halo(chunk_of, step):
        edge = sub - 1 if step < 0 else 0
        return pl.BlockSpec((1, 8, w),
                            lambda b, c: (b + b0, jnp.clip(chunk_of(c) + step, 0, nc - 1) * sub + edge, j))

    fwd, bwd = (lambda c: c), (lambda c: nc - 1 - c)
    halo_prev = [halo(fwd, -1), halo(bwd, -1)]
    halo_next = [halo(fwd, 1), halo(bwd, 1)]
    st = pl.BlockSpec((1, 2, RWKV_HEADS, RWKV_N, RWKV_N), lambda b, c: (b, 0, 0, 0, 0))
    os_ = _seq_specs((D_RWKV, 0), 0, nc)

    def full(a):
        return pl.BlockSpec(a.shape, lambda b, c, _n=a.ndim: (0,) * _n)

    names = ('mu', 'w0', 'w2', 'a0', 'a2', 'g2', 'k_k', 'k_a', 'r_k')
    tok = jax.ShapeDtypeStruct((B, T, D_RWKV), F32)
    return pl.pallas_call(
        _rwkv_kernel,
        grid=(B, nc),
        in_specs=[xs[0], halo_prev[0], halo_next[0], xs[1], halo_prev[1], halo_next[1]]
                 + [full(prm[n]) for n in names] + [st],
        out_specs=[os_[0], os_[1], os_[0], os_[0], st],
        out_shape=[tok, tok, tok, tok, jax.ShapeDtypeStruct((B, 2, RWKV_HEADS, RWKV_N, RWKV_N), F32)],
        scratch_shapes=[pltpu.VMEM((2, RWKV_HEADS, RWKV_N, RWKV_N), F32)],
        compiler_params=_params("arbitrary", "arbitrary"),
    )(u, u, u, u, u, u, *[prm[n] for n in names], s0)


def _epilogue_kernel(yf_ref, yb_ref, z_ref, x_ref, gf_ref, gb_ref, g_ref, rf_ref, rb_ref, gate_ref, bonus_ref,
                     dx_ref, sng_ref, gng_ref, lng_ref, lnb_ref, sh_ref, o_ref):
    y = (yf_ref[0] + yb_ref[0] + x_ref[0] * dx_ref[...]) * _silu(z_ref[0])
    half = D_SSD // SSD_GROUPS
    for g in range(SSD_GROUPS):
        ls = slice(g * half, (g + 1) * half)
        yy = y[:, ls]
        yy = yy * lax.rsqrt(jnp.mean(yy * yy, axis=-1, keepdims=True) + RMS_EPS)
        o_ref[0, :, ls] = (yy * sng_ref[:, ls]).astype(BF16)
    go = gf_ref[0] + gb_ref[0]
    gg = g_ref[0]
    for h in range(GLA_HEADS):
        ls = slice(h * GLA_DV, (h + 1) * GLA_DV)
        oo = go[:, ls]
        oo = oo * lax.rsqrt(jnp.mean(oo * oo, axis=-1, keepdims=True) + RMS_EPS)
        o_ref[0, :, D_SSD + h * GLA_DV:D_SSD + (h + 1) * GLA_DV] = (oo * gng_ref[...] * _silu(gg[:, ls])).astype(BF16)
    ro = rf_ref[0] + rb_ref[0]
    same_head = sh_ref[...]
    mean = _dot_sel(ro, same_head) * (1.0 / RWKV_N)
    cen = ro - mean
    var = _dot_sel(cen * cen, same_head) * (1.0 / RWKV_N)
    o = cen * lax.rsqrt(var + RWKV_LN_EPS) * lng_ref[...] + lnb_ref[...]
    o_ref[0, :, D_SSD + D_GLA:] = ((o + bonus_ref[0]) * gate_ref[0]).astype(BF16)


def _epilogue(u, b0, B, T, ssd_y, gla_o, rwkv_o, prm):
    tm = min(T, 512)

    def tok(width_idx, off):
        w, j = width_idx
        return pl.BlockSpec((1, tm, w), lambda b, i: (b + off, i, j))

    def vec(a):
        return pl.BlockSpec(a.shape, lambda b, i: (0, 0))

    y1, g1, r1 = tok((D_SSD, 0), 0), tok((D_GLA, 0), 0), tok((D_RWKV, 0), 0)
    names = ('dx', 'ssd_norm_g', 'gla_norm_g', 'ln_g', 'ln_b', 'same_head')
    return pl.pallas_call(
        _epilogue_kernel,
        grid=(B, T // tm),
        in_specs=[y1, y1, tok(COL_Z, b0), tok(COL_X, b0), g1, g1, tok(COL_G, b0), r1, r1, r1, r1]
                 + [vec(prm[n]) for n in names],
        out_specs=tok((D_MODEL, 0), 0),
        out_shape=jax.ShapeDtypeStruct((B, T, D_MODEL), BF16),
        compiler_params=_params("arbitrary", "arbitrary"),
    )(ssd_y[0], ssd_y[1], u, u, gla_o[0], gla_o[1], u, rwkv_o[0], rwkv_o[1], rwkv_o[2], rwkv_o[3],
      *[prm[n] for n in names])


def _ada_kernel(c_ref, w_ref, b_ref, o_ref):
    o_ref[0] = _dot(_silu(c_ref[...]), w_ref[0], _NN, 1) + b_ref[0]


def _ada(cvec8, ada_w, ada_b):
    tn = 1536
    n = ada_w.shape[-1]
    return pl.pallas_call(
        _ada_kernel,
        grid=(DEPTH, n // tn),
        in_specs=[pl.BlockSpec((8, D_MODEL), lambda l, j: (0, 0)),
                  pl.BlockSpec((1, D_MODEL, tn), lambda l, j: (l, 0, j)),
                  pl.BlockSpec((1, 1, tn), lambda l, j: (l, 0, j))],
        out_specs=pl.BlockSpec((1, 8, tn), lambda l, j: (l, 0, j)),
        out_shape=jax.ShapeDtypeStruct((DEPTH, 8, n), F32),
        compiler_params=_params("arbitrary", "arbitrary"),
    )(cvec8, ada_w, ada_b.reshape(DEPTH, 1, n))


def _norm_mod(x, g, shift, scale):
    y = x * lax.rsqrt(jnp.mean(x * x, axis=-1, keepdims=True) + RMS_EPS)
    return (y * g) * (1.0 + scale) + shift


def _inproj_kernel(x_ref, g_ref, sh_ref, sc_ref, w_ref, o_ref, h_scr):
    @pl.when(pl.program_id(2) == 0)
    def _():
        h_scr[...] = _norm_mod(x_ref[0], g_ref[...], sh_ref[0], sc_ref[0]).astype(BF16)

    o_ref[0] = jnp.dot(h_scr[...], w_ref[...], preferred_element_type=F32)


def _inproj(x, g, shift, scale, w):
    G, R, D = x.shape
    N = w.shape[1]
    tm, tn = 1024, 1024
    vec = pl.BlockSpec((1, 1, D), lambda gi, i, j: (gi, 0, 0))
    return pl.pallas_call(
        _inproj_kernel,
        grid=(G, R // tm, N // tn),
        in_specs=[pl.BlockSpec((1, tm, D), lambda gi, i, j: (gi, i, 0)),
                  pl.BlockSpec((1, D), lambda gi, i, j: (0, 0)),
                  vec, vec,
                  pl.BlockSpec((D, tn), lambda gi, i, j: (0, j))],
        out_specs=pl.BlockSpec((1, tm, tn), lambda gi, i, j: (gi, i, j)),
        out_shape=jax.ShapeDtypeStruct((G, R, N), F32),
        scratch_shapes=[pltpu.VMEM((tm, D), BF16)],
        compiler_params=_params("arbitrary", "arbitrary", "arbitrary"),
    )(x, g.reshape(1, D), shift.reshape(G, 1, D), scale.reshape(G, 1, D), w)


def _outproj_kernel(a_ref, w_ref, x_ref, gate_ref, o_ref):
    o_ref[0] = x_ref[0] + gate_ref[0] * jnp.dot(a_ref[0], w_ref[...], preferred_element_type=F32)


def _outproj(a, w, xres, gate):
    G, R, Kd = a.shape
    N = w.shape[1]
    tm, tn = 1024, 1024
    return pl.pallas_call(
        _outproj_kernel,
        grid=(G, R // tm, N // tn),
        in_specs=[pl.BlockSpec((1, tm, Kd), lambda gi, i, j: (gi, i, 0)),
                  pl.BlockSpec((Kd, tn), lambda gi, i, j: (0, j)),
                  pl.BlockSpec((1, tm, tn), lambda gi, i, j: (gi, i, j)),
                  pl.BlockSpec((1, 1, tn), lambda gi, i, j: (gi, 0, j))],
        out_specs=pl.BlockSpec((1, tm, tn), lambda gi, i, j: (gi, i, j)),
        out_shape=jax.ShapeDtypeStruct((G, R, N), F32),
        compiler_params=_params("arbitrary", "arbitrary", "arbitrary"),
    )(a, w, xres, gate.reshape(G, 1, N))


def _ffn_in_kernel(x_ref, g_ref, sh_ref, sc_ref, rw_ref, h_ref, lg_ref):
    h = _norm_mod(x_ref[0], g_ref[...], sh_ref[0], sc_ref[0])
    h_ref[0] = h
    lg_ref[0] = _dot(h, rw_ref[...], _NN, 6)


def _ffn_in(x, g, shift, scale, router_w_pad):
    G, R, D = x.shape
    tm = 512
    vec = pl.BlockSpec((1, 1, D), lambda gi, i: (gi, 0, 0))
    return pl.pallas_call(
        _ffn_in_kernel,
        grid=(G, R // tm),
        in_specs=[pl.BlockSpec((1, tm, D), lambda gi, i: (gi, i, 0)),
                  pl.BlockSpec((1, D), lambda gi, i: (0, 0)),
                  vec, vec,
                  pl.BlockSpec((D, LANE), lambda gi, i: (0, 0))],
        out_specs=[pl.BlockSpec((1, tm, D), lambda gi, i: (gi, i, 0)),
                   pl.BlockSpec((1, tm, LANE), lambda gi, i: (gi, i, 0))],
        out_shape=[jax.ShapeDtypeStruct((G, R, D), F32), jax.ShapeDtypeStruct((G, R, LANE), F32)],
        compiler_params=_params("arbitrary", "arbitrary"),
    )(x, g.reshape(1, D), shift.reshape(G, 1, D), scale.reshape(G, 1, D), router_w_pad)


def _moe_kernel(te_ref, na_ref, x_ref, gw_ref, wg_ref, wu_ref, wd_ref, o_ref, wg_s, wu_s, wd_s):
    i = pl.program_id(0)
    active = i < na_ref[0]
    new_expert = (i == 0) | (te_ref[i] != te_ref[jnp.maximum(i - 1, 0)])

    @pl.when(active & new_expert)
    def _():
        wg_s[...] = wg_ref[0, 0].astype(BF16)
        wu_s[...] = wu_ref[0, 0].astype(BF16)
        wd_s[...] = wd_ref[0, 0].astype(BF16)

    @pl.when(active)
    def _():
        x = x_ref[...].astype(BF16)
        hg = jnp.dot(x, wg_s[...], preferred_element_type=F32)
        hu = jnp.dot(x, wu_s[...], preferred_element_type=F32)
        he = (_silu(hg) * hu).astype(BF16)
        o_ref[...] = gw_ref[...] * jnp.dot(he, wd_s[...], preferred_element_type=F32)

    @pl.when(jnp.logical_not(active))
    def _():
        o_ref[...] = jnp.zeros_like(o_ref)


def _moe_experts(tile_expert, n_active, xs, gate_w, w_gate, w_up, w_down, layer):
    P, D = xs.shape
    tm = MOE_TM

    def expert_w(shape):
        return pl.BlockSpec((1, 1) + shape, lambda i, te, na: (layer, te[i], 0, 0), pipeline_mode=pl.Buffered(1))

    grid_spec = pltpu.PrefetchScalarGridSpec(
        num_scalar_prefetch=2,
        grid=(P // tm,),
        in_specs=[pl.BlockSpec((tm, D), lambda i, te, na: (i, 0)),
                  pl.BlockSpec((tm, 1), lambda i, te, na: (i, 0)),
                  expert_w((D, D_EXPERT)), expert_w((D, D_EXPERT)), expert_w((D_EXPERT, D))],
        out_specs=pl.BlockSpec((tm, D), lambda i, te, na: (i, 0)),
        scratch_shapes=[pltpu.VMEM((D, D_EXPERT), BF16), pltpu.VMEM((D, D_EXPERT), BF16),
                        pltpu.VMEM((D_EXPERT, D), BF16)],
    )
    return pl.pallas_call(
        _moe_kernel,
        grid_spec=grid_spec,
        out_shape=jax.ShapeDtypeStruct((P, D), F32),
        compiler_params=_params("arbitrary"),
    )(tile_expert, n_active, xs, gate_w, w_gate, w_up, w_down)


def _route(logits, router_b):
    aff = jax.nn.sigmoid(logits)
    sel = (aff + router_b.astype(F32)).reshape(-1, MOE_GROUPS, EXPERTS_PER_GROUP)
    lane = lax.broadcasted_iota(jnp.int32, sel.shape, 2)

    def top2(x, ids):
        i1 = jnp.argmax(x, axis=-1)
        x2 = jnp.where(ids == i1[..., None], -jnp.inf, x)
        i2 = jnp.argmax(x2, axis=-1)
        return jnp.max(x, axis=-1), jnp.max(x2, axis=-1), i1, i2

    m1, m2, _, _ = top2(sel, lane)
    grp = jnp.argmax(m1 + m2, axis=-1)
    in_grp = jnp.take_along_axis(sel, grp[:, None, None], axis=1)[:, 0]
    _, _, l1, l2 = top2(in_grp, lane[:, 0])
    idx = grp[:, None] * EXPERTS_PER_GROUP + jnp.stack([l1, l2], axis=-1)
    wts = jnp.take_along_axis(aff, idx, axis=1)
    wts = wts / jnp.sum(wts, axis=-1, keepdims=True)
    return idx.astype(jnp.int32), wts


def _moe(h, logits, router_b, w_gate, w_up, w_down, layer):
    M, D = h.shape
    tm = MOE_TM
    idx, wts = _route(logits, router_b)
    n_asg = M * MOE_TOP_K
    P = n_asg + N_EXPERTS * tm
    onehot = jax.nn.one_hot(idx.reshape(-1), N_EXPERTS, dtype=jnp.int32)
    csum = jnp.cumsum(onehot, axis=0)
    counts = csum[-1]
    pcounts = ((counts + tm - 1) // tm) * tm
    pends = jnp.cumsum(pcounts)
    pstarts = pends - pcounts
    dest = jnp.sum(onehot * (pstarts[None, :] + csum - 1), axis=1)
    fields = jnp.stack([jnp.arange(n_asg, dtype=jnp.int32) // MOE_TOP_K,
                        lax.bitcast_convert_type(wts.reshape(-1), jnp.int32)], axis=1)
    slots = jnp.zeros((P, 2), jnp.int32).at[dest].set(fields)
    rows_tok = slots[:, 0]
    gate_w = lax.bitcast_convert_type(slots[:, 1], F32)
    pos = dest.reshape(M, MOE_TOP_K)
    tile_start = jnp.arange(P // tm, dtype=jnp.int32) * tm
    tile_expert = jnp.minimum(jnp.searchsorted(pends, tile_start, side='right'), N_EXPERTS - 1).astype(jnp.int32)
    n_active = (pends[-1] // tm).astype(jnp.int32).reshape(1)
    xs = jnp.take(h, rows_tok, axis=0, mode='clip')
    y = _moe_experts(tile_expert, n_active, xs, gate_w.reshape(P, 1), w_gate, w_up, w_down, layer)
    return lax.optimization_barrier(jnp.take(y, pos.T.reshape(-1), axis=0, mode='clip'))


def _ffn_residual_kernel(x_ref, y0_ref, y1_ref, gate_ref, o_ref):
    o_ref[0] = x_ref[0] + gate_ref[0] * (y0_ref[0, 0] + y1_ref[0, 0])


def _ffn_residual_norm_kernel(x_ref, y0_ref, y1_ref, gate_ref, g_ref, o_ref):
    x = x_ref[0] + gate_ref[0] * (y0_ref[0, 0] + y1_ref[0, 0])
    o_ref[0] = x * lax.rsqrt(jnp.mean(x * x, axis=-1, keepdims=True) + RMS_EPS) * g_ref[...]


def _ffn_residual(x, yg, gate, final_g=None):
    G, R, D = x.shape
    tm = 512
    tok = pl.BlockSpec((1, tm, D), lambda g, i: (g, i, 0))
    in_specs = [tok,
                pl.BlockSpec((1, 1, tm, D), lambda g, i: (0, g, i, 0)),
                pl.BlockSpec((1, 1, tm, D), lambda g, i: (1, g, i, 0)),
                pl.BlockSpec((1, 1, D), lambda g, i: (g, 0, 0))]
    args = [x, yg, yg, gate.reshape(G, 1, D)]
    if final_g is not None:
        in_specs.append(pl.BlockSpec((1, D), lambda g, i: (0, 0)))
        args.append(final_g.reshape(1, D))
    return pl.pallas_call(
        _ffn_residual_kernel if final_g is None else _ffn_residual_norm_kernel,
        grid=(G, R // tm),
        in_specs=in_specs,
        out_specs=tok,
        out_shape=jax.ShapeDtypeStruct((G, R, D), F32),
        compiler_params=_params("arbitrary", "arbitrary"),
    )(*args)


def _split_cols(u, sizes):
    out, start = [], 0
    for s in sizes:
        out.append(u[..., start:start + s])
        start += s
    return out


def _regroup_w_in(w):
    z, xbc, dt, q, k, v, g, glr, rw = _split_cols(w, IN_SPLITS)
    pad = jnp.zeros((w.shape[0], P_IN_PAD - sum(IN_SPLITS)), w.dtype)
    return jnp.concatenate([rw, dt, glr, pad, z, xbc, q, k, v, g], axis=1).astype(BF16)


def _gla_pick_mid():
    out = []
    for d in range(2):
        t = jnp.arange(CHUNK) if d == 0 else CHUNK - 1 - jnp.arange(CHUNK)
        tq, tj = t[:, None], t[None, :]
        halves = [CHUNK >> (i + 1) for i in range(CHUNK.bit_length() - 1)]
        out.append(jnp.concatenate([tj == (tq // (2 * m)) * (2 * m) + m - 1 for m in halves], axis=0))
    return jnp.stack(out).astype(BF16)


def _same_head_matrix():
    head = jnp.arange(D_RWKV) // RWKV_N
    return (head[:, None] == head[None, :]).astype(BF16)


def _lane_row(v, offset=0):
    v = v.reshape(-1)
    return jnp.zeros((1, LANE), F32).at[0, offset:offset + v.shape[0]].set(v)


def _rows_at(m, offset):
    return jnp.zeros((LANE, m.shape[1]), F32).at[offset:offset + m.shape[0]].set(m)


def _mixer_params(p, l):
    two = range(2)
    rw = dict(
        mu=jnp.pad(p['rwkv_mu'][l], ((0, 0), (0, COL_RW[0] - P_RWKV))),
        w0=p['rwkv_w0'][l].reshape(2, 1, D_RWKV),
        w2=jnp.stack([_rows_at(p['rwkv_w2'][l][d], d * RWKV_LW) for d in two]),
        a0=p['rwkv_a0'][l].reshape(2, 1, D_RWKV),
        a2=jnp.stack([_rows_at(p['rwkv_a2'][l][d], d * RWKV_LA) for d in two]),
        g2=p['rwkv_g2'][l],
        k_k=p['rwkv_k_k'][l].reshape(1, D_RWKV),
        k_a=p['rwkv_k_a'][l].reshape(1, D_RWKV),
        r_k=p['rwkv_r_k'][l].reshape(1, D_RWKV))
    dt_lane = jnp.arange(LANE)[:, None]
    head_of_lane = jnp.arange(D_SSD)[None, :] // SSD_HEAD_DIM
    return dict(
        ssd_expand=jnp.stack([dt_lane == d * SSD_HEADS + head_of_lane for d in two]).astype(BF16),
        conv_w=p['ssd_conv_w'][l].reshape(9, SSD_CONV_CH),
        conv_b=p['ssd_conv_b'][l].reshape(1, SSD_CONV_CH),
        dt_bias=_lane_row(p['ssd_dt_bias'][l]),
        ssd_a=jnp.repeat(-jnp.exp(p['ssd_a_log'][l]), SSD_HEAD_DIM, axis=1).reshape(2, 1, D_SSD),
        gla_up=jnp.stack([_rows_at(p['gla_gate_up'][l][d], 2 * SSD_HEADS + d * GLA_LR) for d in two]),
        gla_b=p['gla_gate_b'][l].reshape(2, 1, GLA_QK),
        gla_pick_mid=_gla_pick_mid(),
        rwkv=rw,
        epi=dict(dx=jnp.repeat(p['ssd_d'][l], SSD_HEAD_DIM).reshape(1, D_SSD),
                 ssd_norm_g=p['ssd_norm_g'][l].reshape(1, D_SSD),
                 gla_norm_g=p['gla_norm_g'][l].reshape(1, GLA_DV),
                 ln_g=p['rwkv_ln_g'][l].reshape(1, D_RWKV),
                 ln_b=p['rwkv_ln_b'][l].reshape(1, D_RWKV),
                 same_head=_same_head_matrix()))


def _token_mixers(u, b0, B, T, s0, mp):
    s_ssd, s_gla, s_rwkv = s0
    st_ssd = s_ssd.transpose(0, 1, 4, 2, 3).reshape(B, 2, SSD_STATE, D_SSD)
    st_gla = s_gla.transpose(0, 1, 4, 2, 3).reshape(B, 2, GLA_DV, GLA_QK)
    yf, yb, f_ssd = _ssd_scan(u, b0, B, T, mp['dt_bias'], mp['ssd_a'], mp['ssd_expand'], st_ssd)
    gf, gb, f_gla = _gla_scan(u, b0, B, T, mp['gla_up'], mp['gla_b'], mp['gla_pick_mid'], st_gla)
    rf, rb, gate, bonus, f_rwkv = _rwkv_scan(u, b0, B, T, mp['rwkv'], s_rwkv)
    mix = _epilogue(u, b0, B, T, (yf, yb), (gf, gb), (rf, rb, gate, bonus), mp['epi'])
    f_ssd = f_ssd.reshape(B, 2, SSD_STATE, SSD_HEADS, SSD_HEAD_DIM).transpose(0, 1, 3, 4, 2)
    f_gla = f_gla.reshape(B, 2, GLA_DV, GLA_HEADS, GLA_DK).transpose(0, 1, 3, 4, 2)
    return mix, (f_ssd, f_gla, f_rwkv)


def kernel(x_prompt, x_sample, state_ssd, state_gla, state_rwkv, c, c_ctx, ada_w, ada_b, norm_mix_g, norm_ffn_g,
           w_in, ssd_conv_w, ssd_conv_b, ssd_dt_bias, ssd_a_log, ssd_d, ssd_norm_g, gla_gate_up, gla_gate_b,
           gla_norm_g, rwkv_mu, rwkv_w0, rwkv_w2, rwkv_a0, rwkv_a2, rwkv_g2, rwkv_k_k, rwkv_k_a, rwkv_r_k,
           rwkv_ln_g, rwkv_ln_b, w_out, router_w, router_b, moe_w_gate, moe_w_up, moe_w_down, final_norm_g):
    p = dict(ssd_conv_w=ssd_conv_w, ssd_conv_b=ssd_conv_b, ssd_dt_bias=ssd_dt_bias, ssd_a_log=ssd_a_log,
             ssd_d=ssd_d, ssd_norm_g=ssd_norm_g, gla_gate_up=gla_gate_up, gla_gate_b=gla_gate_b,
             gla_norm_g=gla_norm_g, rwkv_mu=rwkv_mu, rwkv_w0=rwkv_w0, rwkv_w2=rwkv_w2, rwkv_a0=rwkv_a0,
             rwkv_a2=rwkv_a2, rwkv_g2=rwkv_g2, rwkv_k_k=rwkv_k_k, rwkv_k_a=rwkv_k_a, rwkv_r_k=rwkv_r_k,
             rwkv_ln_g=rwkv_ln_g, rwkv_ln_b=rwkv_ln_b)
    D = D_MODEL
    nb, seq = x_prompt.shape[:2]
    db, dseq = x_sample.shape[:2]
    grp = nb * seq
    assert dseq == grp and seq % CHUNK == 0 and dseq == GRID_W * GRID_W
    G = 1 + db
    x = jnp.concatenate([x_prompt.reshape(1, grp, D), x_sample], axis=0)
    cvec = jnp.concatenate([c_ctx[None, :], c, jnp.zeros((8 - G, D), F32)], axis=0)
    mods = _ada(cvec, ada_w, ada_b)[:, :G]
    router_w_pad = jnp.pad(router_w, ((0, 0), (0, LANE - N_EXPERTS)))
    zero_states = (jnp.zeros((nb, 2, SSD_HEADS, SSD_HEAD_DIM, SSD_STATE), F32),
                   jnp.zeros((nb, 2, GLA_HEADS, GLA_DK, GLA_DV), F32),
                   jnp.zeros((nb, 2, RWKV_HEADS, RWKV_N, RWKV_N), F32))
    def regrid(t):
        lat = t[1:].reshape(db, GRID_W, GRID_W, D).swapaxes(1, 2).reshape(db, dseq, D)
        return jnp.concatenate([t[:1], lat], axis=0)

    ctx_states = []
    col_major = False
    for l in range(DEPTH):
        sh1, sc1, g1, sh2, sc2, g2 = jnp.split(mods[l], 6, axis=-1)
        mp = _mixer_params(p, l)
        if col_major != (l % 2 == 1):
            x, col_major = regrid(x), not col_major
        u = _inproj(x, norm_mix_g[l], sh1, sc1, _regroup_w_in(w_in[l]))
        u = _conv_inplace(u, mp['conv_w'], mp['conv_b'], 0, 1, seq, 1)
        u = _conv_inplace(u, mp['conv_w'], mp['conv_b'], 1, db, GRID_W, dseq // GRID_W, col_major)
        mix_p, new = _token_mixers(u.reshape(G * nb, seq, P_IN_PAD), 0, nb, seq, zero_states, mp)
        s0 = (state_ssd[:, l], state_gla[:, l], state_rwkv[:, l])
        mix_s, _ = _token_mixers(u, 1, db, dseq, s0, mp)
        ctx_states.append(new)
        mix = jnp.concatenate([mix_p.reshape(1, grp, D), mix_s], axis=0)
        x = _outproj(mix, w_out[l].astype(BF16), x, g1)
        h, logits = _ffn_in(x, norm_ffn_g[l], sh2, sc2, router_w_pad)
        moe = _moe(h.reshape(G * grp, D), logits.reshape(G * grp, LANE)[:, :N_EXPERTS], router_b,
                   moe_w_gate, moe_w_up, moe_w_down, l)
        x = _ffn_residual(x, moe.reshape(MOE_TOP_K, G, grp, D), g2, final_norm_g if l == DEPTH - 1 else None)
    y = x
    if col_major:
        y = regrid(y)
    y_prompt = y[0].reshape(nb, seq, D)
    y_sample = y[1:]
    new_ssd = jnp.stack([s[0] for s in ctx_states], axis=1)
    new_gla = jnp.stack([s[1] for s in ctx_states], axis=1)
    new_rwkv = jnp.stack([s[2] for s in ctx_states], axis=1)
    return (y_prompt, y_sample, new_ssd, new_gla, new_rwkv)
```

```python
import functools

import jax
import jax.numpy as jnp
from jax import lax
from jax.experimental import pallas as pl
from jax.experimental.pallas import tpu as pltpu

F32 = jnp.float32
BF16 = jnp.bfloat16

D_MODEL = 2048
DEPTH = 2
GRID_W = 64
SSD_HEAD_DIM = 64
D_SSD = 1024
SSD_HEADS = 16
SSD_GROUPS = 2
SSD_STATE = 128
SSD_BC = SSD_GROUPS * SSD_STATE
SSD_CONV_CH = D_SSD + 2 * SSD_BC
GLA_HEADS = 4
D_GLA = 512
GLA_DV = 128
GLA_DK = 64
GLA_QK = GLA_HEADS * GLA_DK
GLA_LR = 16
GLA_GATE_NORM = 16.0
RWKV_N = 64
D_RWKV = 512
RWKV_HEADS = 8
RWKV_LW = 64
RWKV_LA = 64
RWKV_LG = 128
P_RWKV = 3 * D_RWKV + 2 * RWKV_LW + 2 * RWKV_LA + RWKV_LG
IN_SPLITS = (D_SSD, SSD_CONV_CH, 2 * SSD_HEADS, GLA_QK, GLA_QK, D_GLA, D_GLA, 2 * GLA_LR, P_RWKV)
N_EXPERTS = 16
MOE_GROUPS = 4
EXPERTS_PER_GROUP = 4
MOE_TOP_K = 2
D_EXPERT = 1024
RMS_EPS = 1e-6
RWKV_LN_EPS = 64e-5

LANE = 128
CHUNK = 64
RWKV_INV_BASE = 8
MOE_TM = 256
VMEM_LIMIT = 56 * 1024 * 1024

P_IN_PAD = 6144
COL_RW = (2048, 0)
COL_DT = (LANE, 15)
COL_Z = (1024, 2)
COL_X = (1024, 3)
COL_BC = (512, 8)
COL_QK = (512, 9)
COL_V = (512, 10)
COL_G = (512, 11)
CONV_TILE = 256
CONV_FIRST_TILE = 12


def _dot(a, b, contract, passes):
    dn = (contract, ((), ()))
    if passes == 6:
        return lax.dot_general(a, b, dn, precision=lax.Precision.HIGHEST, preferred_element_type=F32)
    ah = a.astype(BF16)
    bh = b.astype(BF16)
    out = lax.dot_general(ah, bh, dn, preferred_element_type=F32)
    if passes == 3:
        al = (a - ah.astype(F32)).astype(BF16)
        bl = (b - bh.astype(F32)).astype(BF16)
        out = out + lax.dot_general(ah, bl, dn, preferred_element_type=F32)
        out = out + lax.dot_general(al, bh, dn, preferred_element_type=F32)
    return out


_NN = ((1,), (0,))
_NT = ((1,), (1,))
_TN = ((0,), (0,))


def _params(*sem):
    return pltpu.CompilerParams(dimension_semantics=sem, vmem_limit_bytes=VMEM_LIMIT)


def _split3(x):
    hi = x.astype(BF16)
    r1 = x - hi.astype(F32)
    mid = r1.astype(BF16)
    lo = (r1 - mid.astype(F32)).astype(BF16)
    return hi, mid, lo


def _sel_dot(sel, x, contract=_NN):
    dn = (contract, ((), ()))
    s = sel.astype(BF16)
    hi, mid, lo = _split3(x)
    out = lax.dot_general(s, hi, dn, preferred_element_type=F32)
    out = out + lax.dot_general(s, mid, dn, preferred_element_type=F32)
    return out + lax.dot_general(s, lo, dn, preferred_element_type=F32)


def _dot_sel(x, sel):
    dn = (_NN, ((), ()))
    s = sel.astype(BF16)
    hi, mid, lo = _split3(x)
    out = lax.dot_general(hi, s, dn, preferred_element_type=F32)
    out = out + lax.dot_general(mid, s, dn, preferred_element_type=F32)
    return out + lax.dot_general(lo, s, dn, preferred_element_type=F32)


def _bdot(a, b, contract=_NN):
    return lax.dot_general(a.astype(BF16), b.astype(BF16), (contract, ((), ())), preferred_element_type=F32)


def _silu(x):
    return x * jax.nn.sigmoid(x)


def _softplus(x):
    return jnp.maximum(x, 0.0) + jnp.log(1.0 + jnp.exp(-jnp.abs(x)))


def _log_sigmoid(x):
    return jnp.minimum(x, 0.0) - jnp.log(1.0 + jnp.exp(-jnp.abs(x)))


def _time_masks(d):
    row = lax.broadcasted_iota(jnp.int32, (CHUNK, CHUNK), 0)
    col = lax.broadcasted_iota(jnp.int32, (CHUNK, CHUNK), 1)
    if d == 0:
        return col <= row, col < row
    return col >= row, col > row


def _seq_specs(width_idx, b0, nc):
    w, j = width_idx
    return [pl.BlockSpec((1, CHUNK, w), lambda b, c: (b + b0, c, j)),
            pl.BlockSpec((1, CHUNK, w), lambda b, c: (b + b0, nc - 1 - c, j))]


def _conv_kernel(x_ref, w_ref, b_ref, o_ref, *, width, rows, col_major):
    x = x_ref[0]
    T = x.shape[0]
    t = lax.broadcasted_iota(jnp.int32, (T, 1), 0)
    if col_major:
        col, row = (t // rows) % width, t % rows
    else:
        col, row = t % width, (t // width) % rows
    acc = jnp.zeros_like(x) + b_ref[...]
    for dr in (-1, 0, 1):
        if rows == 1 and dr != 0:
            continue
        for dc in (-1, 0, 1):
            off = dc * rows + dr if col_major else dr * width + dc
            xs = x if off == 0 else pltpu.roll(x, (-off) % T, axis=0)
            valid = (col + dc >= 0) & (col + dc < width) & (row + dr >= 0) & (row + dr < rows)
            tap = (dr + 1) * 3 + (dc + 1)
            acc = acc + jnp.where(valid, xs, 0.0) * w_ref[tap:tap + 1, :]
    o_ref[0] = _silu(acc)


def _conv_inplace(u, w9, bias, g0, ng, width, rows, col_major=False):
    G, T, P = u.shape
    nt = SSD_CONV_CH // CONV_TILE
    blk = pl.BlockSpec((1, T, CONV_TILE), lambda g, j: (g + g0, 0, CONV_FIRST_TILE + j))
    return pl.pallas_call(
        functools.partial(_conv_kernel, width=width, rows=rows, col_major=col_major),
        grid=(ng, nt),
        in_specs=[blk,
                  pl.BlockSpec((9, CONV_TILE), lambda g, j: (0, j)),
                  pl.BlockSpec((1, CONV_TILE), lambda g, j: (0, j))],
        out_specs=blk,
        out_shape=jax.ShapeDtypeStruct(u.shape, u.dtype),
        input_output_aliases={0: 0},
        compiler_params=_params("arbitrary", "arbitrary"),
    )(u, w9, bias)


def _ssd_kernel(xf_ref, bcf_ref, dtf_ref, xb_ref, bcb_ref, dtb_ref, bias_ref, a_ref, s0_ref,
                yf_ref, yb_ref, sfin_ref, st_scr):
    c = pl.program_id(1)
    L = CHUNK
    HP = D_SSD

    @pl.when(c == 0)
    def _():
        st_scr[...] = s0_ref[0]

    rowx = lax.broadcasted_iota(jnp.int32, (L, HP), 0)
    s_of_lane = lax.broadcasted_iota(jnp.int32, (L, HP), 1) % SSD_HEAD_DIM
    e_row = lax.broadcasted_iota(jnp.int32, (LANE, HP), 0)
    e_head = lax.broadcasted_iota(jnp.int32, (LANE, HP), 1) // SSD_HEAD_DIM
    lane2 = lax.broadcasted_iota(jnp.int32, (L, LANE), 1)
    for d, (x_ref, bc_ref, dt_ref, y_ref) in enumerate(((xf_ref, bcf_ref, dtf_ref, yf_ref),
                                                        (xb_ref, bcb_ref, dtb_ref, yb_ref))):
        incl, _ = _time_masks(d)
        strict_t = (s_of_lane < rowx) if d == 0 else (s_of_lane > rowx)
        incl_t = (s_of_lane <= rowx) if d == 0 else (s_of_lane >= rowx)
        expand = e_row == d * SSD_HEADS + e_head
        dtv = _softplus(dt_ref[0] + bias_ref[...])
        dtx = _dot_sel(dtv, expand)
        dax = dtx * a_ref[d]
        cumx = _sel_dot(incl, dax)
        totx = jnp.sum(dax, axis=0, keepdims=True)
        segx = _sel_dot(incl, jnp.where(strict_t, dax, 0.0))
        decx = jnp.where(incl_t, jnp.exp(jnp.minimum(segx, 0.0)), 0.0)
        x = x_ref[0]
        xdt = x * dtx
        xend = xdt * jnp.exp(totx - cumx)
        ecum = jnp.exp(cumx)
        etot = jnp.exp(totx)
        bc = bc_ref[0]
        for g in range(SSD_GROUPS):
            bm = bc[:, g * SSD_STATE:(g + 1) * SSD_STATE]
            cm = bc[:, SSD_BC + g * SSD_STATE:SSD_BC + (g + 1) * SSD_STATE]
            sc = _bdot(cm, bm, _NT)
            sc2 = jnp.concatenate([sc, sc], axis=1)
            for i in range(SSD_HEADS // SSD_GROUPS // 2):
                ls = slice((g * 4 + i) * LANE, (g * 4 + i + 1) * LANE)
                xp = xdt[:, ls]
                xbd = jnp.concatenate([jnp.where(lane2 < SSD_HEAD_DIM, xp, 0.0),
                                       jnp.where(lane2 >= SSD_HEAD_DIM, xp, 0.0)], axis=0)
                st = st_scr[d, :, ls]
                y_ref[0, :, ls] = _bdot(decx[:, ls] * sc2, xbd) + _bdot(cm, st) * ecum[:, ls]
                st_scr[d, :, ls] = st * etot[:, ls] + _bdot(bm, xend[:, ls], _TN)

    @pl.when(c == pl.num_programs(1) - 1)
    def _():
        sfin_ref[0] = st_scr[...]


def _ssd_scan(u, b0, B, T, bias128, a_heads, s0):
    nc = T // CHUNK
    st = pl.BlockSpec((1, 2, SSD_STATE, D_SSD), lambda b, c: (b, 0, 0, 0))
    vec = pl.BlockSpec((1, LANE), lambda b, c: (0, 0))
    avec = pl.BlockSpec((2, 1, D_SSD), lambda b, c: (0, 0, 0))
    xs, bcs, dts = (_seq_specs(col, b0, nc) for col in (COL_X, COL_BC, COL_DT))
    ys = _seq_specs((D_SSD, 0), 0, nc)
    return pl.pallas_call(
        _ssd_kernel,
        grid=(B, nc),
        in_specs=[xs[0], bcs[0], dts[0], xs[1], bcs[1], dts[1], vec, avec, st],
        out_specs=[ys[0], ys[1], st],
        out_shape=[jax.ShapeDtypeStruct((B, T, D_SSD), F32), jax.ShapeDtypeStruct((B, T, D_SSD), F32),
                   jax.ShapeDtypeStruct((B, 2, SSD_STATE, D_SSD), F32)],
        scratch_shapes=[pltpu.VMEM((2, SSD_STATE, D_SSD), F32)],
        compiler_params=_params("arbitrary", "arbitrary"),
    )(u, u, u, u, u, u, bias128, a_heads, s0)


def _stack_heads(x, head_of_lane):
    return jnp.concatenate([jnp.where(head_of_lane == h, x, 0.0) for h in range(GLA_HEADS)], axis=0)


def _gla_kernel(qkf_ref, vf_ref, dtf_ref, qkb_ref, vb_ref, dtb_ref, wup_ref, gb_ref, pm_ref, s0_ref,
                of_ref, ob_ref, sfin_ref, st_scr):
    c = pl.program_id(1)
    L = CHUNK

    @pl.when(c == 0)
    def _():
        st_scr[...] = s0_ref[0]

    row = lax.broadcasted_iota(jnp.int32, (L, L), 0)
    col = lax.broadcasted_iota(jnp.int32, (L, L), 1)
    row1 = lax.broadcasted_iota(jnp.int32, (L, 1), 0)
    head_of_lane = lax.broadcasted_iota(jnp.int32, (L, GLA_QK), 1) // GLA_DK
    row4 = lax.broadcasted_iota(jnp.int32, (GLA_HEADS * L, L), 0) % L
    col4 = lax.broadcasted_iota(jnp.int32, (GLA_HEADS * L, L), 1)
    for d, (qk_ref, v_ref, dt_ref, o_ref) in enumerate(((qkf_ref, vf_ref, dtf_ref, of_ref),
                                                        (qkb_ref, vb_ref, dtb_ref, ob_ref))):
        incl, _ = _time_masks(d)
        flip = (lambda i: i) if d == 0 else (lambda i: L - 1 - i)
        tq, tj, tq1, tq4, tj4 = flip(row), flip(col), flip(row1), flip(row4), flip(col4)
        qk = qk_ref[0]
        q = qk[:, :GLA_QK] * GLA_DK ** -0.5
        k = qk[:, GLA_QK:]
        v = v_ref[0]
        lg = _log_sigmoid(_dot(dt_ref[0], wup_ref[d], _NN, 3) + gb_ref[d]) / GLA_GATE_NORM
        cum = _sel_dot(incl, lg)
        tot = jnp.sum(lg, axis=0, keepdims=True)
        acc = jnp.where(row4 == col4, _bdot(_stack_heads(q, head_of_lane), k, _NT), 0.0)
        halves = [L >> (i + 1) for i in range(L.bit_length() - 1)]
        cmid_all = _sel_dot(pm_ref[d], cum)
        for i, m in enumerate(halves):
            blk = 2 * m
            cmid = cmid_all[i * L:(i + 1) * L]
            later = (tq1 % blk) >= m
            qs = jnp.where(later, q * jnp.exp(jnp.minimum(cum - cmid, 0.0)), 0.0)
            ks = jnp.where(later, 0.0, k * jnp.exp(jnp.minimum(cmid - cum, 0.0)))
            a_lvl = _bdot(_stack_heads(qs, head_of_lane), ks, _NT)
            acc = acc + jnp.where((tq4 // blk) == (tj4 // blk), a_lvl, 0.0)
        st = st_scr[d]
        o_inter = _bdot(_stack_heads(q * jnp.exp(cum), head_of_lane), st, _NT)
        for h in range(GLA_HEADS):
            rs = slice(h * L, (h + 1) * L)
            ls = slice(h * GLA_DV, (h + 1) * GLA_DV)
            o_ref[0, :, ls] = _bdot(acc[rs], v[:, ls]) + o_inter[rs]
        vst = jnp.concatenate([v[:, h * GLA_DV:(h + 1) * GLA_DV] for h in range(GLA_HEADS)], axis=0)
        kst = _stack_heads(k * jnp.exp(tot - cum), head_of_lane)
        st_scr[d] = st * jnp.exp(tot) + _bdot(vst, kst, _TN)

    @pl.when(c == pl.num_programs(1) - 1)
    def _():
        sfin_ref[0] = st_scr[...]


def _gla_scan(u, b0, B, T, wup, gb, pick_mid, s0):
    nc = T // CHUNK
    st = pl.BlockSpec((1, 2, GLA_DV, GLA_QK), lambda b, c: (b, 0, 0, 0))
    qks, vs, dts = (_seq_specs(col, b0, nc) for col in (COL_QK, COL_V, COL_DT))
    os_ = _seq_specs((D_GLA, 0), 0, nc)
    return pl.pallas_call(
        _gla_kernel,
        grid=(B, nc),
        in_specs=[qks[0], vs[0], dts[0], qks[1], vs[1], dts[1],
                  pl.BlockSpec((2, LANE, GLA_QK), lambda b, c: (0, 0, 0)),
                  pl.BlockSpec((2, 1, GLA_QK), lambda b, c: (0, 0, 0)),
                  pl.BlockSpec(pick_mid.shape, lambda b, c: (0, 0, 0)), st],
        out_specs=[os_[0], os_[1], st],
        out_shape=[jax.ShapeDtypeStruct((B, T, D_GLA), F32), jax.ShapeDtypeStruct((B, T, D_GLA), F32),
                   jax.ShapeDtypeStruct((B, 2, GLA_DV, GLA_QK), F32)],
        scratch_shapes=[pltpu.VMEM((2, GLA_DV, GLA_QK), F32)],
        compiler_params=_params("arbitrary", "arbitrary"),
    )(u, u, u, u, u, u, wup, gb, pick_mid, s0)


def _rwkv_kernel(xf_ref, hpf_ref, hnf_ref, xb_ref, hpb_ref, hnb_ref, mu_ref, w0_ref, w2_ref, a0_ref, a2_ref,
                 g2_ref, kkw_ref, ka_ref, rk_ref, s0_ref, of_ref, ob_ref, gate_ref, bonus_ref, sfin_ref, s_scr):
    c = pl.program_id(1)
    nc = pl.num_programs(1)
    C = CHUNK
    N = RWKV_N
    D = D_RWKV

    @pl.when(c == 0)
    def _():
        s_scr[...] = s0_ref[0]

    row = lax.broadcasted_iota(jnp.int32, (C, C), 0)
    col = lax.broadcasted_iota(jnp.int32, (C, C), 1)
    row1 = lax.broadcasted_iota(jnp.int32, (C, 1), 0)
    eye = (col == row).astype(F32)
    same_head = (lax.broadcasted_iota(jnp.int32, (D, D), 0) // N) == (lax.broadcasted_iota(jnp.int32, (D, D), 1) // N)

    def a_icl(ad, d):
        return jax.nn.sigmoid(a0_ref[d] + _dot(ad, a2_ref[d], _NN, 3))

    def k_eff(kr, a):
        return kr * (1.0 + (a - 1.0) * ka_ref[...])

    chains = []
    for d, (x_ref, hp_ref, hn_ref, o_ref) in enumerate(((xf_ref, hpf_ref, hnf_ref, of_ref),
                                                        (xb_ref, hpb_ref, hnb_ref, ob_ref))):
        cc = c if d == 0 else nc - 1 - c
        incl, strict = _time_masks(d)
        x = x_ref[0]
        prev_row = jnp.where(cc > 0, hp_ref[0][7:8, :], 0.0)
        next_row = jnp.where(cc < nc - 1, hn_ref[0][0:1, :], 0.0)
        prev = jnp.where(row1 == 0, prev_row, pltpu.roll(x, 1, axis=0))
        nxt = jnp.where(row1 == C - 1, next_row, pltpu.roll(x, C - 1, axis=0))
        xm = x + mu_ref[0:1, :] * (prev - x) + mu_ref[1:2, :] * (nxt - x)
        r = xm[:, 0:D]
        kr = xm[:, D:2 * D]
        v = xm[:, 2 * D:3 * D]
        wd = xm[:, 3 * D:3 * D + LANE]
        ad = xm[:, 3 * D + LANE:3 * D + 2 * LANE]
        w_log = _log_sigmoid(w0_ref[d] + _dot(jnp.tanh(wd), w2_ref[d], _NN, 3)) - 0.5
        lw = -jnp.exp(w_log)
        a = a_icl(ad, d)
        kk = kr * kkw_ref[...]
        kk = kk * lax.rsqrt(_dot_sel(kk * kk, same_head) + 1e-12)
        k = k_eff(kr, a)
        if d == 0:
            gd = xm[:, 3 * D + 2 * LANE:3 * D + 3 * LANE]
            gate_ref[0] = _dot(jax.nn.sigmoid(gd), g2_ref[...], _NN, 3)
            k_both = k + k_eff(kr, a_icl(ad, 1))
            bonus_ref[0] = _dot_sel(r * k_both * rk_ref[...], same_head) * v

        cum = _sel_dot(incl, lw)
        tot = jnp.sum(lw, axis=0, keepdims=True)
        b = a * kk
        kp = kk * jnp.exp(cum - lw)
        rp = r * jnp.exp(cum)
        pinv = jnp.exp(-cum)
        kinv = k * pinv
        binv = b * pinv
        pend = jnp.exp(tot - cum)
        kd = k * pend
        bd = b * pend
        ptot = jnp.exp(tot)
        for h in range(RWKV_HEADS):
            sl = slice(h * N, (h + 1) * N)
            chains.append(dict(
                d=d, h=h, sl=sl, incl=incl, strict=strict, o_ref=o_ref, v=v[:, sl], kd=kd[:, sl], bd=bd[:, sl],
                ptot=ptot[:, sl],
                x=jnp.concatenate([kp[:, sl], rp[:, sl]], axis=0).astype(BF16),
                y=jnp.concatenate([kinv[:, sl], binv[:, sl]], axis=0).astype(BF16)))

    for ch in chains:
        aa = _bdot(ch['x'], ch['y'], _NT)
        ch['a_kk'] = jnp.where(ch['strict'], aa[:C, :C], 0.0)
        ch['L'] = jnp.where(ch['strict'], aa[:C, C:], 0.0)
        ch['a_rk'] = jnp.where(ch['incl'], aa[C:, :C], 0.0)
        ch['a_rb'] = jnp.where(ch['incl'], aa[C:, C:], 0.0)
    def same_block(m):
        return (row // m) == (col // m)

    for ch in chains:
        ch['pw'] = jnp.where(same_block(RWKV_INV_BASE), ch['L'], 0.0)
        ch['t'] = eye - ch['pw']
    n = 2
    while n < RWKV_INV_BASE:
        for ch in chains:
            ch['pw'] = _bdot(ch['pw'], ch['pw'])
        for ch in chains:
            ch['t'] = ch['t'] + _bdot(ch['t'], ch['pw'])
        n *= 2
    m = RWKV_INV_BASE
    while m < C:
        off = same_block(2 * m) & jnp.logical_not(same_block(m))
        for ch in chains:
            ch['pw'] = _bdot(ch['t'], jnp.where(off, ch['L'], 0.0))
        for ch in chains:
            ch['t'] = ch['t'] - _bdot(ch['pw'], ch['t'])
        m *= 2
    for ch in chains:
        ch['s0'] = s_scr[ch['d'], ch['h']]
        ch['xs'] = _bdot(ch['x'], ch['s0'], _NT)
        ch['g'] = ch['xs'][:C] + _bdot(ch['a_kk'], ch['v'])
    for ch in chains:
        ch['u'] = _bdot(ch['t'], ch['g'])
    for ch in chains:
        o = ch['xs'][C:] + _bdot(ch['a_rk'], ch['v']) - _bdot(ch['a_rb'], ch['u'])
        ch['o_ref'][0, :, ch['sl']] = o
        s_scr[ch['d'], ch['h']] = (ch['s0'] * ch['ptot'] + _bdot(ch['v'], ch['kd'], _TN)
                                   - _bdot(ch['u'], ch['bd'], _TN))

    @pl.when(c == nc - 1)
    def _():
        sfin_ref[0] = s_scr[...]


def _rwkv_scan(u, b0, B, T, prm, s0):
    nc = T // CHUNK
    w, j = COL_RW
    xs = _seq_specs(COL_RW, b0, nc)
    sub = CHUNK // 8

    def halo(chunk_of, step):
        edge = sub - 1 if step < 0 else 0
        return pl.BlockSpec((1, 8, w),
                            lambda b, c: (b + b0, jnp.clip(chunk_of(c) + step, 0, nc - 1) * sub + edge, j))

    fwd, bwd = (lambda c: c), (lambda c: nc - 1 - c)
    halo_prev = [halo(fwd, -1), halo(bwd, -1)]
    halo_next = [halo(fwd, 1), halo(bwd, 1)]
    st = pl.BlockSpec((1, 2, RWKV_HEADS, RWKV_N, RWKV_N), lambda b, c: (b, 0, 0, 0, 0))
    os_ = _seq_specs((D_RWKV, 0), 0, nc)

    def full(a):
        return pl.BlockSpec(a.shape, lambda b, c, _n=a.ndim: (0,) * _n)

    names = ('mu', 'w0', 'w2', 'a0', 'a2', 'g2', 'k_k', 'k_a', 'r_k')
    tok = jax.ShapeDtypeStruct((B, T, D_RWKV), F32)
    return pl.pallas_call(
        _rwkv_kernel,
        grid=(B, nc),
        in_specs=[xs[0], halo_prev[0], halo_next[0], xs[1], halo_prev[1], halo_next[1]]
                 + [full(prm[n]) for n in names] + [st],
        out_specs=[os_[0], os_[1], os_[0], os_[0], st],
        out_shape=[tok, tok, tok, tok, jax.ShapeDtypeStruct((B, 2, RWKV_HEADS, RWKV_N, RWKV_N), F32)],
        scratch_shapes=[pltpu.VMEM((2, RWKV_HEADS, RWKV_N, RWKV_N), F32)],
        compiler_params=_params("arbitrary", "arbitrary"),
    )(u, u, u, u, u, u, *[prm[n] for n in names], s0)


def _epilogue_kernel(yf_ref, yb_ref, z_ref, x_ref, gf_ref, gb_ref, g_ref, rf_ref, rb_ref, gate_ref, bonus_ref,
                     dx_ref, sng_ref, gng_ref, lng_ref, lnb_ref, sh_ref, o_ref):
    y = (yf_ref[0] + yb_ref[0] + x_ref[0] * dx_ref[...]) * _silu(z_ref[0])
    half = D_SSD // SSD_GROUPS
    for g in range(SSD_GROUPS):
        ls = slice(g * half, (g + 1) * half)
        yy = y[:, ls]
        yy = yy * lax.rsqrt(jnp.mean(yy * yy, axis=-1, keepdims=True) + RMS_EPS)
        o_ref[0, :, ls] = (yy * sng_ref[:, ls]).astype(BF16)
    go = gf_ref[0] + gb_ref[0]
    gg = g_ref[0]
    for h in range(GLA_HEADS):
        ls = slice(h * GLA_DV, (h + 1) * GLA_DV)
        oo = go[:, ls]
        oo = oo * lax.rsqrt(jnp.mean(oo * oo, axis=-1, keepdims=True) + RMS_EPS)
        o_ref[0, :, D_SSD + h * GLA_DV:D_SSD + (h + 1) * GLA_DV] = (oo * gng_ref[...] * _silu(gg[:, ls])).astype(BF16)
    ro = rf_ref[0] + rb_ref[0]
    same_head = sh_ref[...]
    mean = _dot_sel(ro, same_head) * (1.0 / RWKV_N)
    cen = ro - mean
    var = _dot_sel(cen * cen, same_head) * (1.0 / RWKV_N)
    o = cen * lax.rsqrt(var + RWKV_LN_EPS) * lng_ref[...] + lnb_ref[...]
    o_ref[0, :, D_SSD + D_GLA:] = ((o + bonus_ref[0]) * gate_ref[0]).astype(BF16)


def _epilogue(u, b0, B, T, ssd_y, gla_o, rwkv_o, prm):
    tm = min(T, 512)

    def tok(width_idx, off):
        w, j = width_idx
        return pl.BlockSpec((1, tm, w), lambda b, i: (b + off, i, j))

    def vec(a):
        return pl.BlockSpec(a.shape, lambda b, i: (0, 0))

    y1, g1, r1 = tok((D_SSD, 0), 0), tok((D_GLA, 0), 0), tok((D_RWKV, 0), 0)
    names = ('dx', 'ssd_norm_g', 'gla_norm_g', 'ln_g', 'ln_b', 'same_head')
    return pl.pallas_call(
        _epilogue_kernel,
        grid=(B, T // tm),
        in_specs=[y1, y1, tok(COL_Z, b0), tok(COL_X, b0), g1, g1, tok(COL_G, b0), r1, r1, r1, r1]
                 + [vec(prm[n]) for n in names],
        out_specs=tok((D_MODEL, 0), 0),
        out_shape=jax.ShapeDtypeStruct((B, T, D_MODEL), BF16),
        compiler_params=_params("arbitrary", "arbitrary"),
    )(ssd_y[0], ssd_y[1], u, u, gla_o[0], gla_o[1], u, rwkv_o[0], rwkv_o[1], rwkv_o[2], rwkv_o[3],
      *[prm[n] for n in names])


def _ada_kernel(c_ref, w_ref, b_ref, o_ref):
    o_ref[0] = _dot(_silu(c_ref[...]), w_ref[0], _NN, 1) + b_ref[0]


def _ada(cvec8, ada_w, ada_b):
    tn = 1536
    n = ada_w.shape[-1]
    return pl.pallas_call(
        _ada_kernel,
        grid=(DEPTH, n // tn),
        in_specs=[pl.BlockSpec((8, D_MODEL), lambda l, j: (0, 0)),
                  pl.BlockSpec((1, D_MODEL, tn), lambda l, j: (l, 0, j)),
                  pl.BlockSpec((1, 1, tn), lambda l, j: (l, 0, j))],
        out_specs=pl.BlockSpec((1, 8, tn), lambda l, j: (l, 0, j)),
        out_shape=jax.ShapeDtypeStruct((DEPTH, 8, n), F32),
        compiler_params=_params("arbitrary", "arbitrary"),
    )(cvec8, ada_w, ada_b.reshape(DEPTH, 1, n))


def _norm_mod(x, g, shift, scale):
    y = x * lax.rsqrt(jnp.mean(x * x, axis=-1, keepdims=True) + RMS_EPS)
    return (y * g) * (1.0 + scale) + shift


def _inproj_kernel(x_ref, g_ref, sh_ref, sc_ref, w_ref, o_ref, h_scr):
    @pl.when(pl.program_id(2) == 0)
    def _():
        h_scr[...] = _norm_mod(x_ref[0], g_ref[...], sh_ref[0], sc_ref[0]).astype(BF16)

    o_ref[0] = jnp.dot(h_scr[...], w_ref[...], preferred_element_type=F32)


def _inproj(x, g, shift, scale, w):
    G, R, D = x.shape
    N = w.shape[1]
    tm, tn = 1024, 1024
    vec = pl.BlockSpec((1, 1, D), lambda gi, i, j: (gi, 0, 0))
    return pl.pallas_call(
        _inproj_kernel,
        grid=(G, R // tm, N // tn),
        in_specs=[pl.BlockSpec((1, tm, D), lambda gi, i, j: (gi, i, 0)),
                  pl.BlockSpec((1, D), lambda gi, i, j: (0, 0)),
                  vec, vec,
                  pl.BlockSpec((D, tn), lambda gi, i, j: (0, j))],
        out_specs=pl.BlockSpec((1, tm, tn), lambda gi, i, j: (gi, i, j)),
        out_shape=jax.ShapeDtypeStruct((G, R, N), F32),
        scratch_shapes=[pltpu.VMEM((tm, D), BF16)],
        compiler_params=_params("arbitrary", "arbitrary", "arbitrary"),
    )(x, g.reshape(1, D), shift.reshape(G, 1, D), scale.reshape(G, 1, D), w)


def _outproj_kernel(a_ref, w_ref, x_ref, gate_ref, o_ref):
    o_ref[0] = x_ref[0] + gate_ref[0] * jnp.dot(a_ref[0], w_ref[...], preferred_element_type=F32)


def _outproj(a, w, xres, gate):
    G, R, Kd = a.shape
    N = w.shape[1]
    tm, tn = 1024, 1024
    return pl.pallas_call(
        _outproj_kernel,
        grid=(G, R // tm, N // tn),
        in_specs=[pl.BlockSpec((1, tm, Kd), lambda gi, i, j: (gi, i, 0)),
                  pl.BlockSpec((Kd, tn), lambda gi, i, j: (0, j)),
                  pl.BlockSpec((1, tm, tn), lambda gi, i, j: (gi, i, j)),
                  pl.BlockSpec((1, 1, tn), lambda gi, i, j: (gi, 0, j))],
        out_specs=pl.BlockSpec((1, tm, tn), lambda gi, i, j: (gi, i, j)),
        out_shape=jax.ShapeDtypeStruct((G, R, N), F32),
        compiler_params=_params("arbitrary", "arbitrary", "arbitrary"),
    )(a, w, xres, gate.reshape(G, 1, N))


def _ffn_in_kernel(x_ref, g_ref, sh_ref, sc_ref, rw_ref, h_ref, lg_ref):
    h = _norm_mod(x_ref[0], g_ref[...], sh_ref[0], sc_ref[0])
    h_ref[0] = h
    lg_ref[0] = _dot(h, rw_ref[...], _NN, 6)


def _ffn_in(x, g, shift, scale, router_w_pad):
    G, R, D = x.shape
    tm = 512
    vec = pl.BlockSpec((1, 1, D), lambda gi, i: (gi, 0, 0))
    return pl.pallas_call(
        _ffn_in_kernel,
        grid=(G, R // tm),
        in_specs=[pl.BlockSpec((1, tm, D), lambda gi, i: (gi, i, 0)),
                  pl.BlockSpec((1, D), lambda gi, i: (0, 0)),
                  vec, vec,
                  pl.BlockSpec((D, LANE), lambda gi, i: (0, 0))],
        out_specs=[pl.BlockSpec((1, tm, D), lambda gi, i: (gi, i, 0)),
                   pl.BlockSpec((1, tm, LANE), lambda gi, i: (gi, i, 0))],
        out_shape=[jax.ShapeDtypeStruct((G, R, D), F32), jax.ShapeDtypeStruct((G, R, LANE), F32)],
        compiler_params=_params("arbitrary", "arbitrary"),
    )(x, g.reshape(1, D), shift.reshape(G, 1, D), scale.reshape(G, 1, D), router_w_pad)


def _moe_kernel(te_ref, na_ref, x_ref, gw_ref, wg_ref, wu_ref, wd_ref, o_ref, wg_s, wu_s, wd_s):
    i = pl.program_id(0)
    active = i < na_ref[0]
    new_expert = (i == 0) | (te_ref[i] != te_ref[jnp.maximum(i - 1, 0)])

    @pl.when(active & new_expert)
    def _():
        wg_s[...] = wg_ref[0, 0].astype(BF16)
        wu_s[...] = wu_ref[0, 0].astype(BF16)
        wd_s[...] = wd_ref[0, 0].astype(BF16)

    @pl.when(active)
    def _():
        x = x_ref[...].astype(BF16)
        hg = jnp.dot(x, wg_s[...], preferred_element_type=F32)
        hu = jnp.dot(x, wu_s[...], preferred_element_type=F32)
        he = (_silu(hg) * hu).astype(BF16)
        o_ref[...] = gw_ref[...] * jnp.dot(he, wd_s[...], preferred_element_type=F32)

    @pl.when(jnp.logical_not(active))
    def _():
        o_ref[...] = jnp.zeros_like(o_ref)


def _moe_experts(tile_expert, n_active, xs, gate_w, w_gate, w_up, w_down, layer):
    P, D = xs.shape
    tm = MOE_TM

    def expert_w(shape):
        return pl.BlockSpec((1, 1) + shape, lambda i, te, na: (layer, te[i], 0, 0), pipeline_mode=pl.Buffered(1))

    grid_spec = pltpu.PrefetchScalarGridSpec(
        num_scalar_prefetch=2,
        grid=(P // tm,),
        in_specs=[pl.BlockSpec((tm, D), lambda i, te, na: (i, 0)),
                  pl.BlockSpec((tm, 1), lambda i, te, na: (i, 0)),
                  expert_w((D, D_EXPERT)), expert_w((D, D_EXPERT)), expert_w((D_EXPERT, D))],
        out_specs=pl.BlockSpec((tm, D), lambda i, te, na: (i, 0)),
        scratch_shapes=[pltpu.VMEM((D, D_EXPERT), BF16), pltpu.VMEM((D, D_EXPERT), BF16),
                        pltpu.VMEM((D_EXPERT, D), BF16)],
    )
    return pl.pallas_call(
        _moe_kernel,
        grid_spec=grid_spec,
        out_shape=jax.ShapeDtypeStruct((P, D), F32),
        compiler_params=_params("arbitrary"),
    )(tile_expert, n_active, xs, gate_w, w_gate, w_up, w_down)


def _route(logits, router_b):
    aff = jax.nn.sigmoid(logits)
    sel = (aff + router_b.astype(F32)).reshape(-1, MOE_GROUPS, EXPERTS_PER_GROUP)
    lane = lax.broadcasted_iota(jnp.int32, sel.shape, 2)

    def top2(x, ids):
        i1 = jnp.argmax(x, axis=-1)
        x2 = jnp.where(ids == i1[..., None], -jnp.inf, x)
        i2 = jnp.argmax(x2, axis=-1)
        return jnp.max(x, axis=-1), jnp.max(x2, axis=-1), i1, i2

    m1, m2, _, _ = top2(sel, lane)
    grp = jnp.argmax(m1 + m2, axis=-1)
    in_grp = jnp.take_along_axis(sel, grp[:, None, None], axis=1)[:, 0]
    _, _, l1, l2 = top2(in_grp, lane[:, 0])
    idx = grp[:, None] * EXPERTS_PER_GROUP + jnp.stack([l1, l2], axis=-1)
    wts = jnp.take_along_axis(aff, idx, axis=1)
    wts = wts / jnp.sum(wts, axis=-1, keepdims=True)
    return idx.astype(jnp.int32), wts


def _moe(h, logits, router_b, w_gate, w_up, w_down, layer):
    M, D = h.shape
    tm = MOE_TM
    idx, wts = _route(logits, router_b)
    n_asg = M * MOE_TOP_K
    P = n_asg + N_EXPERTS * tm
    onehot = jax.nn.one_hot(idx.reshape(-1), N_EXPERTS, dtype=jnp.int32)
    csum = jnp.cumsum(onehot, axis=0)
    counts = csum[-1]
    pcounts = ((counts + tm - 1) // tm) * tm
    pends = jnp.cumsum(pcounts)
    pstarts = pends - pcounts
    dest = jnp.sum(onehot * (pstarts[None, :] + csum - 1), axis=1)
    fields = jnp.stack([jnp.arange(n_asg, dtype=jnp.int32) // MOE_TOP_K,
                        lax.bitcast_convert_type(wts.reshape(-1), jnp.int32)], axis=1)
    slots = jnp.zeros((P, 2), jnp.int32).at[dest].set(fields)
    rows_tok = slots[:, 0]
    gate_w = lax.bitcast_convert_type(slots[:, 1], F32)
    pos = dest.reshape(M, MOE_TOP_K)
    tile_start = jnp.arange(P // tm, dtype=jnp.int32) * tm
    tile_expert = jnp.minimum(jnp.searchsorted(pends, tile_start, side='right'), N_EXPERTS - 1).astype(jnp.int32)
    n_active = (pends[-1] // tm).astype(jnp.int32).reshape(1)
    xs = jnp.take(h, rows_tok, axis=0, mode='clip')
    y = _moe_experts(tile_expert, n_active, xs, gate_w.reshape(P, 1), w_gate, w_up, w_down, layer)
    return lax.optimization_barrier(jnp.take(y, pos.T.reshape(-1), axis=0, mode='clip'))


def _ffn_residual_kernel(x_ref, y0_ref, y1_ref, gate_ref, o_ref):
    o_ref[0] = x_ref[0] + gate_ref[0] * (y0_ref[0, 0] + y1_ref[0, 0])


def _ffn_residual_norm_kernel(x_ref, y0_ref, y1_ref, gate_ref, g_ref, o_ref):
    x = x_ref[0] + gate_ref[0] * (y0_ref[0, 0] + y1_ref[0, 0])
    o_ref[0] = x * lax.rsqrt(jnp.mean(x * x, axis=-1, keepdims=True) + RMS_EPS) * g_ref[...]


def _ffn_residual(x, yg, gate, final_g=None):
    G, R, D = x.shape
    tm = 512
    tok = pl.BlockSpec((1, tm, D), lambda g, i: (g, i, 0))
    in_specs = [tok,
                pl.BlockSpec((1, 1, tm, D), lambda g, i: (0, g, i, 0)),
                pl.BlockSpec((1, 1, tm, D), lambda g, i: (1, g, i, 0)),
                pl.BlockSpec((1, 1, D), lambda g, i: (g, 0, 0))]
    args = [x, yg, yg, gate.reshape(G, 1, D)]
    if final_g is not None:
        in_specs.append(pl.BlockSpec((1, D), lambda g, i: (0, 0)))
        args.append(final_g.reshape(1, D))
    return pl.pallas_call(
        _ffn_residual_kernel if final_g is None else _ffn_residual_norm_kernel,
        grid=(G, R // tm),
        in_specs=in_specs,
        out_specs=tok,
        out_shape=jax.ShapeDtypeStruct((G, R, D), F32),
        compiler_params=_params("arbitrary", "arbitrary"),
    )(*args)


def _split_cols(u, sizes):
    out, start = [], 0
    for s in sizes:
        out.append(u[..., start:start + s])
        start += s
    return out


def _regroup_w_in(w):
    src, start = {}, 0
    for name, size in zip(('z', 'xbc', 'dt', 'q', 'k', 'v', 'g', 'glr', 'rw'), IN_SPLITS):
        src[name] = (start, start + size)
        start += size
    runs = (('rw', 'rw', 0), ('dt', 'dt', COL_DT[0] * COL_DT[1]),
            ('glr', 'glr', COL_DT[0] * COL_DT[1] + 2 * SSD_HEADS),
            ('z', 'xbc', COL_Z[0] * COL_Z[1]), ('q', 'g', COL_QK[0] * COL_QK[1]))
    out = jnp.zeros((w.shape[0], P_IN_PAD), BF16)
    for first, last, dst in runs:
        out = lax.dynamic_update_slice(out, w[:, src[first][0]:src[last][1]].astype(BF16), (0, dst))
    return out


def _gla_pick_mid():
    out = []
    for d in range(2):
        t = jnp.arange(CHUNK) if d == 0 else CHUNK - 1 - jnp.arange(CHUNK)
        tq, tj = t[:, None], t[None, :]
        halves = [CHUNK >> (i + 1) for i in range(CHUNK.bit_length() - 1)]
        out.append(jnp.concatenate([tj == (tq // (2 * m)) * (2 * m) + m - 1 for m in halves], axis=0))
    return jnp.stack(out).astype(BF16)


def _same_head_matrix():
    head = jnp.arange(D_RWKV) // RWKV_N
    return (head[:, None] == head[None, :]).astype(BF16)


def _lane_row(v, offset=0):
    v = v.reshape(-1)
    return jnp.zeros((1, LANE), F32).at[0, offset:offset + v.shape[0]].set(v)


def _rows_at(m, offset):
    return jnp.zeros((LANE, m.shape[1]), F32).at[offset:offset + m.shape[0]].set(m)


def _mixer_params(p, l):
    two = range(2)
    rw = dict(
        mu=jnp.pad(p['rwkv_mu'][l], ((0, 0), (0, COL_RW[0] - P_RWKV))),
        w0=p['rwkv_w0'][l].reshape(2, 1, D_RWKV),
        w2=jnp.stack([_rows_at(p['rwkv_w2'][l][d], d * RWKV_LW) for d in two]),
        a0=p['rwkv_a0'][l].reshape(2, 1, D_RWKV),
        a2=jnp.stack([_rows_at(p['rwkv_a2'][l][d], d * RWKV_LA) for d in two]),
        g2=p['rwkv_g2'][l],
        k_k=p['rwkv_k_k'][l].reshape(1, D_RWKV),
        k_a=p['rwkv_k_a'][l].reshape(1, D_RWKV),
        r_k=p['rwkv_r_k'][l].reshape(1, D_RWKV))
    return dict(
        conv_w=p['ssd_conv_w'][l].reshape(9, SSD_CONV_CH),
        conv_b=p['ssd_conv_b'][l].reshape(1, SSD_CONV_CH),
        dt_bias=_lane_row(p['ssd_dt_bias'][l]),
        ssd_a=jnp.repeat(-jnp.exp(p['ssd_a_log'][l]), SSD_HEAD_DIM, axis=1).reshape(2, 1, D_SSD),
        gla_up=jnp.stack([_rows_at(p['gla_gate_up'][l][d], 2 * SSD_HEADS + d * GLA_LR) for d in two]),
        gla_b=p['gla_gate_b'][l].reshape(2, 1, GLA_QK),
        gla_pick_mid=_gla_pick_mid(),
        rwkv=rw,
        epi=dict(dx=jnp.repeat(p['ssd_d'][l], SSD_HEAD_DIM).reshape(1, D_SSD),
                 ssd_norm_g=p['ssd_norm_g'][l].reshape(1, D_SSD),
                 gla_norm_g=p['gla_norm_g'][l].reshape(1, GLA_DV),
                 ln_g=p['rwkv_ln_g'][l].reshape(1, D_RWKV),
                 ln_b=p['rwkv_ln_b'][l].reshape(1, D_RWKV),
                 same_head=_same_head_matrix()))


def _token_mixers(u, b0, B, T, s0, mp):
    s_ssd, s_gla, s_rwkv = s0
    st_ssd = s_ssd.transpose(0, 1, 4, 2, 3).reshape(B, 2, SSD_STATE, D_SSD)
    st_gla = s_gla.transpose(0, 1, 4, 2, 3).reshape(B, 2, GLA_DV, GLA_QK)
    yf, yb, f_ssd = _ssd_scan(u, b0, B, T, mp['dt_bias'], mp['ssd_a'], st_ssd)
    gf, gb, f_gla = _gla_scan(u, b0, B, T, mp['gla_up'], mp['gla_b'], mp['gla_pick_mid'], st_gla)
    rf, rb, gate, bonus, f_rwkv = _rwkv_scan(u, b0, B, T, mp['rwkv'], s_rwkv)
    mix = _epilogue(u, b0, B, T, (yf, yb), (gf, gb), (rf, rb, gate, bonus), mp['epi'])
    f_ssd = f_ssd.reshape(B, 2, SSD_STATE, SSD_HEADS, SSD_HEAD_DIM).transpose(0, 1, 3, 4, 2)
    f_gla = f_gla.reshape(B, 2, GLA_DV, GLA_HEADS, GLA_DK).transpose(0, 1, 3, 4, 2)
    return mix, (f_ssd, f_gla, f_rwkv)


def kernel(x_prompt, x_sample, state_ssd, state_gla, state_rwkv, c, c_ctx, ada_w, ada_b, norm_mix_g, norm_ffn_g,
           w_in, ssd_conv_w, ssd_conv_b, ssd_dt_bias, ssd_a_log, ssd_d, ssd_norm_g, gla_gate_up, gla_gate_b,
           gla_norm_g, rwkv_mu, rwkv_w0, rwkv_w2, rwkv_a0, rwkv_a2, rwkv_g2, rwkv_k_k, rwkv_k_a, rwkv_r_k,
           rwkv_ln_g, rwkv_ln_b, w_out, router_w, router_b, moe_w_gate, moe_w_up, moe_w_down, final_norm_g):
    p = dict(ssd_conv_w=ssd_conv_w, ssd_conv_b=ssd_conv_b, ssd_dt_bias=ssd_dt_bias, ssd_a_log=ssd_a_log,
             ssd_d=ssd_d, ssd_norm_g=ssd_norm_g, gla_gate_up=gla_gate_up, gla_gate_b=gla_gate_b,
             gla_norm_g=gla_norm_g, rwkv_mu=rwkv_mu, rwkv_w0=rwkv_w0, rwkv_w2=rwkv_w2, rwkv_a0=rwkv_a0,
             rwkv_a2=rwkv_a2, rwkv_g2=rwkv_g2, rwkv_k_k=rwkv_k_k, rwkv_k_a=rwkv_k_a, rwkv_r_k=rwkv_r_k,
             rwkv_ln_g=rwkv_ln_g, rwkv_ln_b=rwkv_ln_b)
    D = D_MODEL
    nb, seq = x_prompt.shape[:2]
    db, dseq = x_sample.shape[:2]
    grp = nb * seq
    assert dseq == grp and seq % CHUNK == 0 and dseq == GRID_W * GRID_W
    G = 1 + db
    x = jnp.concatenate([x_prompt.reshape(1, grp, D), x_sample], axis=0)
    cvec = jnp.concatenate([c_ctx[None, :], c, jnp.zeros((8 - G, D), F32)], axis=0)
    mods = _ada(cvec, ada_w, ada_b)[:, :G]
    router_w_pad = jnp.pad(router_w, ((0, 0), (0, LANE - N_EXPERTS)))
    zero_states = (jnp.zeros((nb, 2, SSD_HEADS, SSD_HEAD_DIM, SSD_STATE), F32),
                   jnp.zeros((nb, 2, GLA_HEADS, GLA_DK, GLA_DV), F32),
                   jnp.zeros((nb, 2, RWKV_HEADS, RWKV_N, RWKV_N), F32))
    def regrid_latent(t):
        return t[1:].reshape(db, GRID_W, GRID_W, D).swapaxes(1, 2).reshape(db, dseq, D)

    ctx_states = []
    col_major = False
    for l in range(DEPTH):
        sh1, sc1, g1, sh2, sc2, g2 = jnp.split(mods[l], 6, axis=-1)
        mp = _mixer_params(p, l)
        if col_major != (l % 2 == 1):
            x, col_major = x.at[1:].set(regrid_latent(x)), not col_major
        u = _inproj(x, norm_mix_g[l], sh1, sc1, _regroup_w_in(w_in[l]))
        u = _conv_inplace(u, mp['conv_w'], mp['conv_b'], 0, 1, seq, 1)
        u = _conv_inplace(u, mp['conv_w'], mp['conv_b'], 1, db, GRID_W, dseq // GRID_W, col_major)
        mix_p, new = _token_mixers(u.reshape(G * nb, seq, P_IN_PAD), 0, nb, seq, zero_states, mp)
        s0 = (state_ssd[:, l], state_gla[:, l], state_rwkv[:, l])
        mix_s, _ = _token_mixers(u, 1, db, dseq, s0, mp)
        ctx_states.append(new)
        mix = jnp.concatenate([mix_p.reshape(1, grp, D), mix_s], axis=0)
        x = _outproj(mix, w_out[l].astype(BF16), x, g1)
        h, logits = _ffn_in(x, norm_ffn_g[l], sh2, sc2, router_w_pad)
        moe = _moe(h.reshape(G * grp, D), logits.reshape(G * grp, LANE)[:, :N_EXPERTS], router_b,
                   moe_w_gate, moe_w_up, moe_w_down, l)
        x = _ffn_residual(x, moe.reshape(MOE_TOP_K, G, grp, D), g2, final_norm_g if l == DEPTH - 1 else None)
    y_prompt = x[0].reshape(nb, seq, D)
    y_sample = regrid_latent(x) if col_major else x[1:]
    new_ssd = jnp.stack([s[0] for s in ctx_states], axis=1)
    new_gla = jnp.stack([s[1] for s in ctx_states], axis=1)
    new_rwkv = jnp.stack([s[2] for s in ctx_states], axis=1)
    return (y_prompt, y_sample, new_ssd, new_gla, new_rwkv)
```

```python
import functools

import jax
import jax.numpy as jnp
from jax import lax
from jax.experimental import pallas as pl
from jax.experimental.pallas import tpu as pltpu

F32 = jnp.float32
BF16 = jnp.bfloat16

D_MODEL = 2048
DEPTH = 2
GRID_W = 64
SSD_HEAD_DIM = 64
D_SSD = 1024
SSD_HEADS = 16
SSD_GROUPS = 2
SSD_STATE = 128
SSD_BC = SSD_GROUPS * SSD_STATE
SSD_CONV_CH = D_SSD + 2 * SSD_BC
GLA_HEADS = 4
D_GLA = 512
GLA_DV = 128
GLA_DK = 64
GLA_QK = GLA_HEADS * GLA_DK
GLA_LR = 16
GLA_GATE_NORM = 16.0
RWKV_N = 64
D_RWKV = 512
RWKV_HEADS = 8
RWKV_LW = 64
RWKV_LA = 64
RWKV_LG = 128
P_RWKV = 3 * D_RWKV + 2 * RWKV_LW + 2 * RWKV_LA + RWKV_LG
IN_SPLITS = (D_SSD, SSD_CONV_CH, 2 * SSD_HEADS, GLA_QK, GLA_QK, D_GLA, D_GLA, 2 * GLA_LR, P_RWKV)
N_EXPERTS = 16
MOE_GROUPS = 4
EXPERTS_PER_GROUP = 4
MOE_TOP_K = 2
D_EXPERT = 1024
RMS_EPS = 1e-6
RWKV_LN_EPS = 64e-5

LANE = 128
CHUNK = 64
RWKV_INV_BASE = 8
MOE_TM = 256
VMEM_LIMIT = 56 * 1024 * 1024

P_IN_PAD = 6144
COL_RW = (2048, 0)
COL_DT = (LANE, 15)
COL_Z = (1024, 2)
COL_X = (1024, 3)
COL_BC = (512, 8)
COL_QK = (512, 9)
COL_V = (512, 10)
COL_G = (512, 11)
CONV_TILE = 256
CONV_FIRST_TILE = 12


def _dot(a, b, contract, passes):
    dn = (contract, ((), ()))
    if passes == 6:
        return lax.dot_general(a, b, dn, precision=lax.Precision.HIGHEST, preferred_element_type=F32)
    ah = a.astype(BF16)
    bh = b.astype(BF16)
    out = lax.dot_general(ah, bh, dn, preferred_element_type=F32)
    if passes == 3:
        al = (a - ah.astype(F32)).astype(BF16)
        bl = (b - bh.astype(F32)).astype(BF16)
        out = out + lax.dot_general(ah, bl, dn, preferred_element_type=F32)
        out = out + lax.dot_general(al, bh, dn, preferred_element_type=F32)
    return out


_NN = ((1,), (0,))
_NT = ((1,), (1,))
_TN = ((0,), (0,))


def _params(*sem):
    return pltpu.CompilerParams(dimension_semantics=sem, vmem_limit_bytes=VMEM_LIMIT)


def _split3(x):
    hi = x.astype(BF16)
    r1 = x - hi.astype(F32)
    mid = r1.astype(BF16)
    lo = (r1 - mid.astype(F32)).astype(BF16)
    return hi, mid, lo


def _sel_dot(sel, x, contract=_NN):
    dn = (contract, ((), ()))
    s = sel.astype(BF16)
    hi, mid, lo = _split3(x)
    out = lax.dot_general(s, hi, dn, preferred_element_type=F32)
    out = out + lax.dot_general(s, mid, dn, preferred_element_type=F32)
    return out + lax.dot_general(s, lo, dn, preferred_element_type=F32)


def _dot_sel(x, sel):
    dn = (_NN, ((), ()))
    s = sel.astype(BF16)
    hi, mid, lo = _split3(x)
    out = lax.dot_general(hi, s, dn, preferred_element_type=F32)
    out = out + lax.dot_general(mid, s, dn, preferred_element_type=F32)
    return out + lax.dot_general(lo, s, dn, preferred_element_type=F32)


def _bdot(a, b, contract=_NN):
    return lax.dot_general(a.astype(BF16), b.astype(BF16), (contract, ((), ())), preferred_element_type=F32)


def _silu(x):
    return x * jax.nn.sigmoid(x)


def _softplus(x):
    return jnp.maximum(x, 0.0) + jnp.log(1.0 + jnp.exp(-jnp.abs(x)))


def _log_sigmoid(x):
    return jnp.minimum(x, 0.0) - jnp.log(1.0 + jnp.exp(-jnp.abs(x)))


def _time_masks(d):
    row = lax.broadcasted_iota(jnp.int32, (CHUNK, CHUNK), 0)
    col = lax.broadcasted_iota(jnp.int32, (CHUNK, CHUNK), 1)
    if d == 0:
        return col <= row, col < row
    return col >= row, col > row


def _seq_specs(width_idx, b0, nc):
    w, j = width_idx
    return [pl.BlockSpec((1, CHUNK, w), lambda b, c: (b + b0, c, j)),
            pl.BlockSpec((1, CHUNK, w), lambda b, c: (b + b0, nc - 1 - c, j))]


def _conv_kernel(x_ref, w_ref, b_ref, o_ref, *, width, rows, col_major):
    x = x_ref[0]
    T = x.shape[0]
    t = lax.broadcasted_iota(jnp.int32, (T, 1), 0)
    if col_major:
        col, row = (t // rows) % width, t % rows
    else:
        col, row = t % width, (t // width) % rows
    acc = jnp.zeros_like(x) + b_ref[...]
    for dr in (-1, 0, 1):
        if rows == 1 and dr != 0:
            continue
        for dc in (-1, 0, 1):
            off = dc * rows + dr if col_major else dr * width + dc
            xs = x if off == 0 else pltpu.roll(x, (-off) % T, axis=0)
            valid = (col + dc >= 0) & (col + dc < width) & (row + dr >= 0) & (row + dr < rows)
            tap = (dr + 1) * 3 + (dc + 1)
            acc = acc + jnp.where(valid, xs, 0.0) * w_ref[tap:tap + 1, :]
    o_ref[0] = _silu(acc)


def _conv_inplace(u, w9, bias, g0, ng, width, rows, col_major=False):
    G, T, P = u.shape
    nt = SSD_CONV_CH // CONV_TILE
    blk = pl.BlockSpec((1, T, CONV_TILE), lambda g, j: (g + g0, 0, CONV_FIRST_TILE + j))
    return pl.pallas_call(
        functools.partial(_conv_kernel, width=width, rows=rows, col_major=col_major),
        grid=(ng, nt),
        in_specs=[blk,
                  pl.BlockSpec((9, CONV_TILE), lambda g, j: (0, j)),
                  pl.BlockSpec((1, CONV_TILE), lambda g, j: (0, j))],
        out_specs=blk,
        out_shape=jax.ShapeDtypeStruct(u.shape, u.dtype),
        input_output_aliases={0: 0},
        compiler_params=_params("arbitrary", "arbitrary"),
    )(u, w9, bias)


def _ssd_kernel(xf_ref, bcf_ref, dtf_ref, xb_ref, bcb_ref, dtb_ref, bias_ref, a_ref, s0_ref,
                yf_ref, yb_ref, sfin_ref, st_scr):
    c = pl.program_id(1)
    L = CHUNK
    HP = D_SSD

    @pl.when(c == 0)
    def _():
        st_scr[...] = s0_ref[0]

    rowx = lax.broadcasted_iota(jnp.int32, (L, HP), 0)
    s_of_lane = lax.broadcasted_iota(jnp.int32, (L, HP), 1) % SSD_HEAD_DIM
    e_row = lax.broadcasted_iota(jnp.int32, (LANE, HP), 0)
    e_head = lax.broadcasted_iota(jnp.int32, (LANE, HP), 1) // SSD_HEAD_DIM
    lane2 = lax.broadcasted_iota(jnp.int32, (L, LANE), 1)
    for d, (x_ref, bc_ref, dt_ref, y_ref) in enumerate(((xf_ref, bcf_ref, dtf_ref, yf_ref),
                                                        (xb_ref, bcb_ref, dtb_ref, yb_ref))):
        incl, _ = _time_masks(d)
        strict_t = (s_of_lane < rowx) if d == 0 else (s_of_lane > rowx)
        incl_t = (s_of_lane <= rowx) if d == 0 else (s_of_lane >= rowx)
        expand = e_row == d * SSD_HEADS + e_head
        dtv = _softplus(dt_ref[0] + bias_ref[...])
        dtx = _dot_sel(dtv, expand)
        dax = dtx * a_ref[d]
        cumx = _sel_dot(incl, dax)
        totx = jnp.sum(dax, axis=0, keepdims=True)
        segx = _sel_dot(incl, jnp.where(strict_t, dax, 0.0))
        decx = jnp.where(incl_t, jnp.exp(jnp.minimum(segx, 0.0)), 0.0)
        x = x_ref[0]
        xdt = x * dtx
        xend = xdt * jnp.exp(totx - cumx)
        ecum = jnp.exp(cumx)
        etot = jnp.exp(totx)
        bc = bc_ref[0]
        for g in range(SSD_GROUPS):
            bm = bc[:, g * SSD_STATE:(g + 1) * SSD_STATE]
            cm = bc[:, SSD_BC + g * SSD_STATE:SSD_BC + (g + 1) * SSD_STATE]
            sc = _bdot(cm, bm, _NT)
            sc2 = jnp.concatenate([sc, sc], axis=1)
            for i in range(SSD_HEADS // SSD_GROUPS // 2):
                ls = slice((g * 4 + i) * LANE, (g * 4 + i + 1) * LANE)
                xp = xdt[:, ls]
                xbd = jnp.concatenate([jnp.where(lane2 < SSD_HEAD_DIM, xp, 0.0),
                                       jnp.where(lane2 >= SSD_HEAD_DIM, xp, 0.0)], axis=0)
                st = st_scr[d, :, ls]
                y_ref[0, :, ls] = _bdot(decx[:, ls] * sc2, xbd) + _bdot(cm, st) * ecum[:, ls]
                st_scr[d, :, ls] = st * etot[:, ls] + _bdot(bm, xend[:, ls], _TN)

    @pl.when(c == pl.num_programs(1) - 1)
    def _():
        sfin_ref[0] = st_scr[...]


def _ssd_scan(u, b0, B, T, bias128, a_heads, s0):
    nc = T // CHUNK
    st = pl.BlockSpec((1, 2, SSD_STATE, D_SSD), lambda b, c: (b, 0, 0, 0))
    vec = pl.BlockSpec((1, LANE), lambda b, c: (0, 0))
    avec = pl.BlockSpec((2, 1, D_SSD), lambda b, c: (0, 0, 0))
    xs, bcs, dts = (_seq_specs(col, b0, nc) for col in (COL_X, COL_BC, COL_DT))
    ys = _seq_specs((D_SSD, 0), 0, nc)
    return pl.pallas_call(
        _ssd_kernel,
        grid=(B, nc),
        in_specs=[xs[0], bcs[0], dts[0], xs[1], bcs[1], dts[1], vec, avec, st],
        out_specs=[ys[0], ys[1], st],
        out_shape=[jax.ShapeDtypeStruct((B, T, D_SSD), F32), jax.ShapeDtypeStruct((B, T, D_SSD), F32),
                   jax.ShapeDtypeStruct((B, 2, SSD_STATE, D_SSD), F32)],
        scratch_shapes=[pltpu.VMEM((2, SSD_STATE, D_SSD), F32)],
        compiler_params=_params("arbitrary", "arbitrary"),
    )(u, u, u, u, u, u, bias128, a_heads, s0)


def _stack_heads(x, head_of_lane):
    return jnp.concatenate([jnp.where(head_of_lane == h, x, 0.0) for h in range(GLA_HEADS)], axis=0)


def _gla_kernel(qkf_ref, vf_ref, dtf_ref, qkb_ref, vb_ref, dtb_ref, wup_ref, gb_ref, pm_ref, s0_ref,
                of_ref, ob_ref, sfin_ref, st_scr):
    c = pl.program_id(1)
    L = CHUNK

    @pl.when(c == 0)
    def _():
        st_scr[...] = s0_ref[0]

    row = lax.broadcasted_iota(jnp.int32, (L, L), 0)
    col = lax.broadcasted_iota(jnp.int32, (L, L), 1)
    row1 = lax.broadcasted_iota(jnp.int32, (L, 1), 0)
    head_of_lane = lax.broadcasted_iota(jnp.int32, (L, GLA_QK), 1) // GLA_DK
    row4 = lax.broadcasted_iota(jnp.int32, (GLA_HEADS * L, L), 0) % L
    col4 = lax.broadcasted_iota(jnp.int32, (GLA_HEADS * L, L), 1)
    for d, (qk_ref, v_ref, dt_ref, o_ref) in enumerate(((qkf_ref, vf_ref, dtf_ref, of_ref),
                                                        (qkb_ref, vb_ref, dtb_ref, ob_ref))):
        incl, _ = _time_masks(d)
        flip = (lambda i: i) if d == 0 else (lambda i: L - 1 - i)
        tq, tj, tq1, tq4, tj4 = flip(row), flip(col), flip(row1), flip(row4), flip(col4)
        qk = qk_ref[0]
        q = qk[:, :GLA_QK] * GLA_DK ** -0.5
        k = qk[:, GLA_QK:]
        v = v_ref[0]
        lg = _log_sigmoid(_dot(dt_ref[0], wup_ref[d], _NN, 3) + gb_ref[d]) / GLA_GATE_NORM
        cum = _sel_dot(incl, lg)
        tot = jnp.sum(lg, axis=0, keepdims=True)
        acc = jnp.where(row4 == col4, _bdot(_stack_heads(q, head_of_lane), k, _NT), 0.0)
        halves = [L >> (i + 1) for i in range(L.bit_length() - 1)]
        cmid_all = _sel_dot(pm_ref[d], cum)
        for i, m in enumerate(halves):
            blk = 2 * m
            cmid = cmid_all[i * L:(i + 1) * L]
            later = (tq1 % blk) >= m
            qs = jnp.where(later, q * jnp.exp(jnp.minimum(cum - cmid, 0.0)), 0.0)
            ks = jnp.where(later, 0.0, k * jnp.exp(jnp.minimum(cmid - cum, 0.0)))
            a_lvl = _bdot(_stack_heads(qs, head_of_lane), ks, _NT)
            acc = acc + jnp.where((tq4 // blk) == (tj4 // blk), a_lvl, 0.0)
        st = st_scr[d]
        o_inter = _bdot(_stack_heads(q * jnp.exp(cum), head_of_lane), st, _NT)
        for h in range(GLA_HEADS):
            rs = slice(h * L, (h + 1) * L)
            ls = slice(h * GLA_DV, (h + 1) * GLA_DV)
            o_ref[0, :, ls] = _bdot(acc[rs], v[:, ls]) + o_inter[rs]
        vst = jnp.concatenate([v[:, h * GLA_DV:(h + 1) * GLA_DV] for h in range(GLA_HEADS)], axis=0)
        kst = _stack_heads(k * jnp.exp(tot - cum), head_of_lane)
        st_scr[d] = st * jnp.exp(tot) + _bdot(vst, kst, _TN)

    @pl.when(c == pl.num_programs(1) - 1)
    def _():
        sfin_ref[0] = st_scr[...]


def _gla_scan(u, b0, B, T, wup, gb, pick_mid, s0):
    nc = T // CHUNK
    st = pl.BlockSpec((1, 2, GLA_DV, GLA_QK), lambda b, c: (b, 0, 0, 0))
    qks, vs, dts = (_seq_specs(col, b0, nc) for col in (COL_QK, COL_V, COL_DT))
    os_ = _seq_specs((D_GLA, 0), 0, nc)
    return pl.pallas_call(
        _gla_kernel,
        grid=(B, nc),
        in_specs=[qks[0], vs[0], dts[0], qks[1], vs[1], dts[1],
                  pl.BlockSpec((2, LANE, GLA_QK), lambda b, c: (0, 0, 0)),
                  pl.BlockSpec((2, 1, GLA_QK), lambda b, c: (0, 0, 0)),
                  pl.BlockSpec(pick_mid.shape, lambda b, c: (0, 0, 0)), st],
        out_specs=[os_[0], os_[1], st],
        out_shape=[jax.ShapeDtypeStruct((B, T, D_GLA), F32), jax.ShapeDtypeStruct((B, T, D_GLA), F32),
                   jax.ShapeDtypeStruct((B, 2, GLA_DV, GLA_QK), F32)],
        scratch_shapes=[pltpu.VMEM((2, GLA_DV, GLA_QK), F32)],
        compiler_params=_params("arbitrary", "arbitrary"),
    )(u, u, u, u, u, u, wup, gb, pick_mid, s0)


def _rwkv_kernel(xf_ref, hpf_ref, hnf_ref, xb_ref, hpb_ref, hnb_ref, mu_ref, w0_ref, w2_ref, a0_ref, a2_ref,
                 g2_ref, kkw_ref, ka_ref, rk_ref, s0_ref, of_ref, ob_ref, gate_ref, bonus_ref, sfin_ref, s_scr):
    c = pl.program_id(1)
    nc = pl.num_programs(1)
    C = CHUNK
    N = RWKV_N
    D = D_RWKV

    @pl.when(c == 0)
    def _():
        s_scr[...] = s0_ref[0]

    row = lax.broadcasted_iota(jnp.int32, (C, C), 0)
    col = lax.broadcasted_iota(jnp.int32, (C, C), 1)
    row1 = lax.broadcasted_iota(jnp.int32, (C, 1), 0)
    eye = (col == row).astype(F32)
    same_head = (lax.broadcasted_iota(jnp.int32, (D, D), 0) // N) == (lax.broadcasted_iota(jnp.int32, (D, D), 1) // N)

    def a_icl(ad, d):
        return jax.nn.sigmoid(a0_ref[d] + _dot(ad, a2_ref[d], _NN, 3))

    def k_eff(kr, a):
        return kr * (1.0 + (a - 1.0) * ka_ref[...])

    chains = []
    for d, (x_ref, hp_ref, hn_ref, o_ref) in enumerate(((xf_ref, hpf_ref, hnf_ref, of_ref),
                                                        (xb_ref, hpb_ref, hnb_ref, ob_ref))):
        cc = c if d == 0 else nc - 1 - c
        incl, strict = _time_masks(d)
        x = x_ref[0]
        prev_row = jnp.where(cc > 0, hp_ref[0][7:8, :], 0.0)
        next_row = jnp.where(cc < nc - 1, hn_ref[0][0:1, :], 0.0)
        prev = jnp.where(row1 == 0, prev_row, pltpu.roll(x, 1, axis=0))
        nxt = jnp.where(row1 == C - 1, next_row, pltpu.roll(x, C - 1, axis=0))
        xm = x + mu_ref[0:1, :] * (prev - x) + mu_ref[1:2, :] * (nxt - x)
        r = xm[:, 0:D]
        kr = xm[:, D:2 * D]
        v = xm[:, 2 * D:3 * D]
        wd = xm[:, 3 * D:3 * D + LANE]
        ad = xm[:, 3 * D + LANE:3 * D + 2 * LANE]
        w_log = _log_sigmoid(w0_ref[d] + _dot(jnp.tanh(wd), w2_ref[d], _NN, 3)) - 0.5
        lw = -jnp.exp(w_log)
        a = a_icl(ad, d)
        kk = kr * kkw_ref[...]
        kk = kk * lax.rsqrt(_dot_sel(kk * kk, same_head) + 1e-12)
        k = k_eff(kr, a)
        if d == 0:
            gd = xm[:, 3 * D + 2 * LANE:3 * D + 3 * LANE]
            gate_ref[0] = _dot(jax.nn.sigmoid(gd), g2_ref[...], _NN, 3)
            k_both = k + k_eff(kr, a_icl(ad, 1))
            bonus_ref[0] = _dot_sel(r * k_both * rk_ref[...], same_head) * v

        cum = _sel_dot(incl, lw)
        tot = jnp.sum(lw, axis=0, keepdims=True)
        b = a * kk
        kp = kk * jnp.exp(cum - lw)
        rp = r * jnp.exp(cum)
        pinv = jnp.exp(-cum)
        kinv = k * pinv
        binv = b * pinv
        pend = jnp.exp(tot - cum)
        kd = k * pend
        bd = b * pend
        ptot = jnp.exp(tot)
        for h in range(RWKV_HEADS):
            sl = slice(h * N, (h + 1) * N)
            chains.append(dict(
                d=d, h=h, sl=sl, incl=incl, strict=strict, o_ref=o_ref, v=v[:, sl], kd=kd[:, sl], bd=bd[:, sl],
                ptot=ptot[:, sl],
                x=jnp.concatenate([kp[:, sl], rp[:, sl]], axis=0).astype(BF16),
                y=jnp.concatenate([kinv[:, sl], binv[:, sl]], axis=0).astype(BF16)))

    for ch in chains:
        aa = _bdot(ch['x'], ch['y'], _NT)
        ch['a_kk'] = jnp.where(ch['strict'], aa[:C, :C], 0.0)
        ch['L'] = jnp.where(ch['strict'], aa[:C, C:], 0.0)
        ch['a_rk'] = jnp.where(ch['incl'], aa[C:, :C], 0.0)
        ch['a_rb'] = jnp.where(ch['incl'], aa[C:, C:], 0.0)
    def same_block(m):
        return (row // m) == (col // m)

    for ch in chains:
        ch['pw'] = jnp.where(same_block(RWKV_INV_BASE), ch['L'], 0.0)
        ch['t'] = eye - ch['pw']
    n = 2
    while n < RWKV_INV_BASE:
        for ch in chains:
            ch['pw'] = _bdot(ch['pw'], ch['pw'])
        for ch in chains:
            ch['t'] = ch['t'] + _bdot(ch['t'], ch['pw'])
        n *= 2
    m = RWKV_INV_BASE
    while m < C:
        off = same_block(2 * m) & jnp.logical_not(same_block(m))
        for ch in chains:
            ch['pw'] = _bdot(ch['t'], jnp.where(off, ch['L'], 0.0))
        for ch in chains:
            ch['t'] = ch['t'] - _bdot(ch['pw'], ch['t'])
        m *= 2
    for ch in chains:
        ch['s0'] = s_scr[ch['d'], ch['h']]
        ch['xs'] = _bdot(ch['x'], ch['s0'], _NT)
        ch['g'] = ch['xs'][:C] + _bdot(ch['a_kk'], ch['v'])
    for ch in chains:
        ch['u'] = _bdot(ch['t'], ch['g'])
    for ch in chains:
        o = ch['xs'][C:] + _bdot(ch['a_rk'], ch['v']) - _bdot(ch['a_rb'], ch['u'])
        ch['o_ref'][0, :, ch['sl']] = o
        s_scr[ch['d'], ch['h']] = (ch['s0'] * ch['ptot'] + _bdot(ch['v'], ch['kd'], _TN)
                                   - _bdot(ch['u'], ch['bd'], _TN))

    @pl.when(c == nc - 1)
    def _():
        sfin_ref[0] = s_scr[...]


def _rwkv_scan(u, b0, B, T, prm, s0):
    nc = T // CHUNK
    w, j = COL_RW
    xs = _seq_specs(COL_RW, b0, nc)
    sub = CHUNK // 8

    def halo(chunk_of, step):
        edge = sub - 1 if step < 0 else 0
        return pl.BlockSpec((1, 8, w),
                            lambda b, c: (b + b0, jnp.clip(chunk_of(c) + step, 0, nc - 1) * sub + edge, j))

    fwd, bwd = (lambda c: c), (lambda c: nc - 1 - c)
    halo_prev = [halo(fwd, -1), halo(bwd, -1)]
    halo_next = [halo(fwd, 1), halo(bwd, 1)]
    st = pl.BlockSpec((1, 2, RWKV_HEADS, RWKV_N, RWKV_N), lambda b, c: (b, 0, 0, 0, 0))
    os_ = _seq_specs((D_RWKV, 0), 0, nc)

    def full(a):
        return pl.BlockSpec(a.shape, lambda b, c, _n=a.ndim: (0,) * _n)

    names = ('mu', 'w0', 'w2', 'a0', 'a2', 'g2', 'k_k', 'k_a', 'r_k')
    tok = jax.ShapeDtypeStruct((B, T, D_RWKV), F32)
    return pl.pallas_call(
        _rwkv_kernel,
        grid=(B, nc),
        in_specs=[xs[0], halo_prev[0], halo_next[0], xs[1], halo_prev[1], halo_next[1]]
                 + [full(prm[n]) for n in names] + [st],
        out_specs=[os_[0], os_[1], os_[0], os_[0], st],
        out_shape=[tok, tok, tok, tok, jax.ShapeDtypeStruct((B, 2, RWKV_HEADS, RWKV_N, RWKV_N), F32)],
        scratch_shapes=[pltpu.VMEM((2, RWKV_HEADS, RWKV_N, RWKV_N), F32)],
        compiler_params=_params("arbitrary", "arbitrary"),
    )(u, u, u, u, u, u, *[prm[n] for n in names], s0)


def _epilogue_kernel(yf_ref, yb_ref, z_ref, x_ref, gf_ref, gb_ref, g_ref, rf_ref, rb_ref, gate_ref, bonus_ref,
                     dx_ref, sng_ref, gng_ref, lng_ref, lnb_ref, sh_ref, o_ref):
    y = (yf_ref[0] + yb_ref[0] + x_ref[0] * dx_ref[...]) * _silu(z_ref[0])
    half = D_SSD // SSD_GROUPS
    for g in range(SSD_GROUPS):
        ls = slice(g * half, (g + 1) * half)
        yy = y[:, ls]
        yy = yy * lax.rsqrt(jnp.mean(yy * yy, axis=-1, keepdims=True) + RMS_EPS)
        o_ref[0, :, ls] = (yy * sng_ref[:, ls]).astype(BF16)
    go = gf_ref[0] + gb_ref[0]
    gg = g_ref[0]
    for h in range(GLA_HEADS):
        ls = slice(h * GLA_DV, (h + 1) * GLA_DV)
        oo = go[:, ls]
        oo = oo * lax.rsqrt(jnp.mean(oo * oo, axis=-1, keepdims=True) + RMS_EPS)
        o_ref[0, :, D_SSD + h * GLA_DV:D_SSD + (h + 1) * GLA_DV] = (oo * gng_ref[...] * _silu(gg[:, ls])).astype(BF16)
    ro = rf_ref[0] + rb_ref[0]
    same_head = sh_ref[...]
    mean = _dot_sel(ro, same_head) * (1.0 / RWKV_N)
    cen = ro - mean
    var = _dot_sel(cen * cen, same_head) * (1.0 / RWKV_N)
    o = cen * lax.rsqrt(var + RWKV_LN_EPS) * lng_ref[...] + lnb_ref[...]
    o_ref[0, :, D_SSD + D_GLA:] = ((o + bonus_ref[0]) * gate_ref[0]).astype(BF16)


def _epilogue(u, b0, B, T, ssd_y, gla_o, rwkv_o, prm):
    tm = min(T, 512)

    def tok(width_idx, off):
        w, j = width_idx
        return pl.BlockSpec((1, tm, w), lambda b, i: (b + off, i, j))

    def vec(a):
        return pl.BlockSpec(a.shape, lambda b, i: (0, 0))

    y1, g1, r1 = tok((D_SSD, 0), 0), tok((D_GLA, 0), 0), tok((D_RWKV, 0), 0)
    names = ('dx', 'ssd_norm_g', 'gla_norm_g', 'ln_g', 'ln_b', 'same_head')
    return pl.pallas_call(
        _epilogue_kernel,
        grid=(B, T // tm),
        in_specs=[y1, y1, tok(COL_Z, b0), tok(COL_X, b0), g1, g1, tok(COL_G, b0), r1, r1, r1, r1]
                 + [vec(prm[n]) for n in names],
        out_specs=tok((D_MODEL, 0), 0),
        out_shape=jax.ShapeDtypeStruct((B, T, D_MODEL), BF16),
        compiler_params=_params("arbitrary", "arbitrary"),
    )(ssd_y[0], ssd_y[1], u, u, gla_o[0], gla_o[1], u, rwkv_o[0], rwkv_o[1], rwkv_o[2], rwkv_o[3],
      *[prm[n] for n in names])


def _ada_kernel(c_ref, w_ref, b_ref, o_ref):
    o_ref[0] = _dot(_silu(c_ref[...]), w_ref[0], _NN, 1) + b_ref[0]


def _ada(cvec8, ada_w, ada_b):
    tn = 1536
    n = ada_w.shape[-1]
    return pl.pallas_call(
        _ada_kernel,
        grid=(DEPTH, n // tn),
        in_specs=[pl.BlockSpec((8, D_MODEL), lambda l, j: (0, 0)),
                  pl.BlockSpec((1, D_MODEL, tn), lambda l, j: (l, 0, j)),
                  pl.BlockSpec((1, 1, tn), lambda l, j: (l, 0, j))],
        out_specs=pl.BlockSpec((1, 8, tn), lambda l, j: (l, 0, j)),
        out_shape=jax.ShapeDtypeStruct((DEPTH, 8, n), F32),
        compiler_params=_params("arbitrary", "arbitrary"),
    )(cvec8, ada_w, ada_b.reshape(DEPTH, 1, n))


def _norm_mod(x, g, shift, scale):
    y = x * lax.rsqrt(jnp.mean(x * x, axis=-1, keepdims=True) + RMS_EPS)
    return (y * g) * (1.0 + scale) + shift


def _inproj_kernel(x_ref, g_ref, sh_ref, sc_ref, w_ref, o_ref, h_scr):
    @pl.when(pl.program_id(2) == 0)
    def _():
        h_scr[...] = _norm_mod(x_ref[0], g_ref[...], sh_ref[0], sc_ref[0]).astype(BF16)

    o_ref[0] = jnp.dot(h_scr[...], w_ref[...], preferred_element_type=F32)


def _inproj(x, g, shift, scale, w):
    G, R, D = x.shape
    N = w.shape[1]
    tm, tn = 1024, 1024
    vec = pl.BlockSpec((1, 1, D), lambda gi, i, j: (gi, 0, 0))
    return pl.pallas_call(
        _inproj_kernel,
        grid=(G, R // tm, N // tn),
        in_specs=[pl.BlockSpec((1, tm, D), lambda gi, i, j: (gi, i, 0)),
                  pl.BlockSpec((1, D), lambda gi, i, j: (0, 0)),
                  vec, vec,
                  pl.BlockSpec((D, tn), lambda gi, i, j: (0, j))],
        out_specs=pl.BlockSpec((1, tm, tn), lambda gi, i, j: (gi, i, j)),
        out_shape=jax.ShapeDtypeStruct((G, R, N), F32),
        scratch_shapes=[pltpu.VMEM((tm, D), BF16)],
        compiler_params=_params("arbitrary", "arbitrary", "arbitrary"),
    )(x, g.reshape(1, D), shift.reshape(G, 1, D), scale.reshape(G, 1, D), w)


def _outproj_kernel(a_ref, w_ref, x_ref, gate_ref, o_ref):
    o_ref[0] = x_ref[0] + gate_ref[0] * jnp.dot(a_ref[0], w_ref[...], preferred_element_type=F32)


def _outproj(a, w, xres, gate):
    G, R, Kd = a.shape
    N = w.shape[1]
    tm, tn = 1024, 1024
    return pl.pallas_call(
        _outproj_kernel,
        grid=(G, R // tm, N // tn),
        in_specs=[pl.BlockSpec((1, tm, Kd), lambda gi, i, j: (gi, i, 0)),
                  pl.BlockSpec((Kd, tn), lambda gi, i, j: (0, j)),
                  pl.BlockSpec((1, tm, tn), lambda gi, i, j: (gi, i, j)),
                  pl.BlockSpec((1, 1, tn), lambda gi, i, j: (gi, 0, j))],
        out_specs=pl.BlockSpec((1, tm, tn), lambda gi, i, j: (gi, i, j)),
        out_shape=jax.ShapeDtypeStruct((G, R, N), F32),
        compiler_params=_params("arbitrary", "arbitrary", "arbitrary"),
    )(a, w, xres, gate.reshape(G, 1, N))


def _ffn_in_kernel(x_ref, g_ref, sh_ref, sc_ref, rw_ref, h_ref, lg_ref):
    h = _norm_mod(x_ref[0], g_ref[...], sh_ref[0], sc_ref[0])
    h_ref[0] = h
    lg_ref[0] = _dot(h, rw_ref[...], _NN, 6)


def _ffn_in(x, g, shift, scale, router_w_pad):
    G, R, D = x.shape
    tm = 512
    vec = pl.BlockSpec((1, 1, D), lambda gi, i: (gi, 0, 0))
    return pl.pallas_call(
        _ffn_in_kernel,
        grid=(G, R // tm),
        in_specs=[pl.BlockSpec((1, tm, D), lambda gi, i: (gi, i, 0)),
                  pl.BlockSpec((1, D), lambda gi, i: (0, 0)),
                  vec, vec,
                  pl.BlockSpec((D, LANE), lambda gi, i: (0, 0))],
        out_specs=[pl.BlockSpec((1, tm, D), lambda gi, i: (gi, i, 0)),
                   pl.BlockSpec((1, tm, LANE), lambda gi, i: (gi, i, 0))],
        out_shape=[jax.ShapeDtypeStruct((G, R, D), F32), jax.ShapeDtypeStruct((G, R, LANE), F32)],
        compiler_params=_params("arbitrary", "arbitrary"),
    )(x, g.reshape(1, D), shift.reshape(G, 1, D), scale.reshape(G, 1, D), router_w_pad)


def _moe_kernel(te_ref, na_ref, x_ref, gw_ref, wg_ref, wu_ref, wd_ref, o_ref, wg_s, wu_s, wd_s):
    i = pl.program_id(0)
    active = i < na_ref[0]
    new_expert = (i == 0) | (te_ref[i] != te_ref[jnp.maximum(i - 1, 0)])

    @pl.when(active & new_expert)
    def _():
        wg_s[...] = wg_ref[0, 0].astype(BF16)
        wu_s[...] = wu_ref[0, 0].astype(BF16)
        wd_s[...] = wd_ref[0, 0].astype(BF16)

    @pl.when(active)
    def _():
        x = x_ref[...].astype(BF16)
        hg = jnp.dot(x, wg_s[...], preferred_element_type=F32)
        hu = jnp.dot(x, wu_s[...], preferred_element_type=F32)
        he = (_silu(hg) * hu).astype(BF16)
        o_ref[...] = gw_ref[...] * jnp.dot(he, wd_s[...], preferred_element_type=F32)

    @pl.when(jnp.logical_not(active))
    def _():
        o_ref[...] = jnp.zeros_like(o_ref)


def _moe_experts(tile_expert, n_active, xs, gate_w, w_gate, w_up, w_down, layer):
    P, D = xs.shape
    tm = MOE_TM

    def expert_w(shape):
        return pl.BlockSpec((1, 1) + shape, lambda i, te, na: (layer, te[i], 0, 0), pipeline_mode=pl.Buffered(1))

    grid_spec = pltpu.PrefetchScalarGridSpec(
        num_scalar_prefetch=2,
        grid=(P // tm,),
        in_specs=[pl.BlockSpec((tm, D), lambda i, te, na: (i, 0)),
                  pl.BlockSpec((tm, 1), lambda i, te, na: (i, 0)),
                  expert_w((D, D_EXPERT)), expert_w((D, D_EXPERT)), expert_w((D_EXPERT, D))],
        out_specs=pl.BlockSpec((tm, D), lambda i, te, na: (i, 0)),
        scratch_shapes=[pltpu.VMEM((D, D_EXPERT), BF16), pltpu.VMEM((D, D_EXPERT), BF16),
                        pltpu.VMEM((D_EXPERT, D), BF16)],
    )
    return pl.pallas_call(
        _moe_kernel,
        grid_spec=grid_spec,
        out_shape=jax.ShapeDtypeStruct((P, D), F32),
        compiler_params=_params("arbitrary"),
    )(tile_expert, n_active, xs, gate_w, w_gate, w_up, w_down)


RANK_TM = 512


def _rank_kernel(e_ref, rank_ref, cnt_ref, run_scr):
    @pl.when(pl.program_id(0) == 0)
    def _():
        run_scr[...] = jnp.zeros_like(run_scr)

    e = e_ref[...]
    own = lax.broadcasted_iota(jnp.int32, (RANK_TM, LANE), 1) == e
    row = lax.broadcasted_iota(jnp.int32, (RANK_TM, RANK_TM), 0)
    col = lax.broadcasted_iota(jnp.int32, (RANK_TM, RANK_TM), 1)
    cum = _bdot(col <= row, own) + run_scr[...]
    rank_ref[...] = jnp.sum(jnp.where(own, cum, 0.0), axis=1, keepdims=True).astype(jnp.int32)
    run_scr[...] = cum[RANK_TM - 1:RANK_TM, :]
    cnt_ref[...] = cum[RANK_TM - 1:RANK_TM, :].astype(jnp.int32)


def _expert_ranks(e_flat):
    n = e_flat.shape[0]
    rank, cnt = pl.pallas_call(
        _rank_kernel,
        grid=(n // RANK_TM,),
        in_specs=[pl.BlockSpec((RANK_TM, 1), lambda i: (i, 0))],
        out_specs=[pl.BlockSpec((RANK_TM, 1), lambda i: (i, 0)), pl.BlockSpec((1, LANE), lambda i: (0, 0))],
        out_shape=[jax.ShapeDtypeStruct((n, 1), jnp.int32), jax.ShapeDtypeStruct((1, LANE), jnp.int32)],
        scratch_shapes=[pltpu.VMEM((1, LANE), F32)],
        compiler_params=_params("arbitrary"),
    )(e_flat)
    return rank, cnt[0, :N_EXPERTS]


def _route(logits, router_b):
    aff = jax.nn.sigmoid(logits)
    sel = (aff + router_b.astype(F32)).reshape(-1, MOE_GROUPS, EXPERTS_PER_GROUP)
    lane = lax.broadcasted_iota(jnp.int32, sel.shape, 2)

    def top2(x, ids):
        i1 = jnp.argmax(x, axis=-1)
        x2 = jnp.where(ids == i1[..., None], -jnp.inf, x)
        i2 = jnp.argmax(x2, axis=-1)
        return jnp.max(x, axis=-1), jnp.max(x2, axis=-1), i1, i2

    m1, m2, _, _ = top2(sel, lane)
    grp = jnp.argmax(m1 + m2, axis=-1)
    in_grp = jnp.take_along_axis(sel, grp[:, None, None], axis=1)[:, 0]
    _, _, l1, l2 = top2(in_grp, lane[:, 0])
    idx = grp[:, None] * EXPERTS_PER_GROUP + jnp.stack([l1, l2], axis=-1)
    wts = jnp.take_along_axis(aff, idx, axis=1)
    wts = wts / jnp.sum(wts, axis=-1, keepdims=True)
    return idx.astype(jnp.int32), wts


def _moe(h, logits, router_b, w_gate, w_up, w_down, layer):
    M, D = h.shape
    tm = MOE_TM
    idx, wts = _route(logits, router_b)
    n_asg = M * MOE_TOP_K
    P = n_asg + N_EXPERTS * tm
    e_flat = idx.reshape(n_asg, 1)
    rank, counts = _expert_ranks(e_flat)
    pcounts = ((counts + tm - 1) // tm) * tm
    pends = jnp.cumsum(pcounts)
    pstarts = pends - pcounts
    onehot = e_flat == jnp.arange(N_EXPERTS, dtype=jnp.int32)[None, :]
    dest = jnp.sum(jnp.where(onehot, pstarts[None, :], 0), axis=1) + rank[:, 0] - 1
    fields = jnp.stack([jnp.arange(n_asg, dtype=jnp.int32) // MOE_TOP_K,
                        lax.bitcast_convert_type(wts.reshape(-1), jnp.int32)], axis=1)
    slots = jnp.zeros((P, 2), jnp.int32).at[dest].set(fields)
    rows_tok = slots[:, 0]
    gate_w = lax.bitcast_convert_type(slots[:, 1], F32)
    pos = dest.reshape(M, MOE_TOP_K)
    tile_start = jnp.arange(P // tm, dtype=jnp.int32) * tm
    tile_expert = jnp.minimum(jnp.searchsorted(pends, tile_start, side='right'), N_EXPERTS - 1).astype(jnp.int32)
    n_active = (pends[-1] // tm).astype(jnp.int32).reshape(1)
    xs = jnp.take(h, rows_tok, axis=0, mode='clip')
    y = _moe_experts(tile_expert, n_active, xs, gate_w.reshape(P, 1), w_gate, w_up, w_down, layer)
    return lax.optimization_barrier(jnp.take(y, pos.T.reshape(-1), axis=0, mode='clip'))


def _ffn_residual_kernel(x_ref, y0_ref, y1_ref, gate_ref, o_ref):
    o_ref[0] = x_ref[0] + gate_ref[0] * (y0_ref[0, 0] + y1_ref[0, 0])


def _ffn_residual_norm_kernel(x_ref, y0_ref, y1_ref, gate_ref, g_ref, o_ref):
    x = x_ref[0] + gate_ref[0] * (y0_ref[0, 0] + y1_ref[0, 0])
    o_ref[0] = x * lax.rsqrt(jnp.mean(x * x, axis=-1, keepdims=True) + RMS_EPS) * g_ref[...]


def _ffn_residual(x, yg, gate, final_g=None):
    G, R, D = x.shape
    tm = 512
    tok = pl.BlockSpec((1, tm, D), lambda g, i: (g, i, 0))
    in_specs = [tok,
                pl.BlockSpec((1, 1, tm, D), lambda g, i: (0, g, i, 0)),
                pl.BlockSpec((1, 1, tm, D), lambda g, i: (1, g, i, 0)),
                pl.BlockSpec((1, 1, D), lambda g, i: (g, 0, 0))]
    args = [x, yg, yg, gate.reshape(G, 1, D)]
    if final_g is not None:
        in_specs.append(pl.BlockSpec((1, D), lambda g, i: (0, 0)))
        args.append(final_g.reshape(1, D))
    return pl.pallas_call(
        _ffn_residual_kernel if final_g is None else _ffn_residual_norm_kernel,
        grid=(G, R // tm),
        in_specs=in_specs,
        out_specs=tok,
        out_shape=jax.ShapeDtypeStruct((G, R, D), F32),
        compiler_params=_params("arbitrary", "arbitrary"),
    )(*args)


def _split_cols(u, sizes):
    out, start = [], 0
    for s in sizes:
        out.append(u[..., start:start + s])
        start += s
    return out


def _regroup_w_in(w):
    src, start = {}, 0
    for name, size in zip(('z', 'xbc', 'dt', 'q', 'k', 'v', 'g', 'glr', 'rw'), IN_SPLITS):
        src[name] = (start, start + size)
        start += size
    runs = (('rw', 'rw', 0), ('dt', 'dt', COL_DT[0] * COL_DT[1]),
            ('glr', 'glr', COL_DT[0] * COL_DT[1] + 2 * SSD_HEADS),
            ('z', 'xbc', COL_Z[0] * COL_Z[1]), ('q', 'g', COL_QK[0] * COL_QK[1]))
    out = jnp.zeros((w.shape[0], P_IN_PAD), BF16)
    for first, last, dst in runs:
        out = lax.dynamic_update_slice(out, w[:, src[first][0]:src[last][1]].astype(BF16), (0, dst))
    return out


def _gla_pick_mid():
    out = []
    for d in range(2):
        t = jnp.arange(CHUNK) if d == 0 else CHUNK - 1 - jnp.arange(CHUNK)
        tq, tj = t[:, None], t[None, :]
        halves = [CHUNK >> (i + 1) for i in range(CHUNK.bit_length() - 1)]
        out.append(jnp.concatenate([tj == (tq // (2 * m)) * (2 * m) + m - 1 for m in halves], axis=0))
    return jnp.stack(out).astype(BF16)


def _same_head_matrix():
    head = jnp.arange(D_RWKV) // RWKV_N
    return (head[:, None] == head[None, :]).astype(BF16)


def _lane_row(v, offset=0):
    v = v.reshape(-1)
    return jnp.zeros((1, LANE), F32).at[0, offset:offset + v.shape[0]].set(v)


def _rows_at(m, offset):
    return jnp.zeros((LANE, m.shape[1]), F32).at[offset:offset + m.shape[0]].set(m)


def _mixer_params(p, l):
    two = range(2)
    rw = dict(
        mu=jnp.pad(p['rwkv_mu'][l], ((0, 0), (0, COL_RW[0] - P_RWKV))),
        w0=p['rwkv_w0'][l].reshape(2, 1, D_RWKV),
        w2=jnp.stack([_rows_at(p['rwkv_w2'][l][d], d * RWKV_LW) for d in two]),
        a0=p['rwkv_a0'][l].reshape(2, 1, D_RWKV),
        a2=jnp.stack([_rows_at(p['rwkv_a2'][l][d], d * RWKV_LA) for d in two]),
        g2=p['rwkv_g2'][l],
        k_k=p['rwkv_k_k'][l].reshape(1, D_RWKV),
        k_a=p['rwkv_k_a'][l].reshape(1, D_RWKV),
        r_k=p['rwkv_r_k'][l].reshape(1, D_RWKV))
    return dict(
        conv_w=p['ssd_conv_w'][l].reshape(9, SSD_CONV_CH),
        conv_b=p['ssd_conv_b'][l].reshape(1, SSD_CONV_CH),
        dt_bias=_lane_row(p['ssd_dt_bias'][l]),
        ssd_a=jnp.repeat(-jnp.exp(p['ssd_a_log'][l]), SSD_HEAD_DIM, axis=1).reshape(2, 1, D_SSD),
        gla_up=jnp.stack([_rows_at(p['gla_gate_up'][l][d], 2 * SSD_HEADS + d * GLA_LR) for d in two]),
        gla_b=p['gla_gate_b'][l].reshape(2, 1, GLA_QK),
        gla_pick_mid=_gla_pick_mid(),
        rwkv=rw,
        epi=dict(dx=jnp.repeat(p['ssd_d'][l], SSD_HEAD_DIM).reshape(1, D_SSD),
                 ssd_norm_g=p['ssd_norm_g'][l].reshape(1, D_SSD),
                 gla_norm_g=p['gla_norm_g'][l].reshape(1, GLA_DV),
                 ln_g=p['rwkv_ln_g'][l].reshape(1, D_RWKV),
                 ln_b=p['rwkv_ln_b'][l].reshape(1, D_RWKV),
                 same_head=_same_head_matrix()))


def _token_mixers(u, b0, B, T, s0, mp):
    s_ssd, s_gla, s_rwkv = s0
    st_ssd = s_ssd.transpose(0, 1, 4, 2, 3).reshape(B, 2, SSD_STATE, D_SSD)
    st_gla = s_gla.transpose(0, 1, 4, 2, 3).reshape(B, 2, GLA_DV, GLA_QK)
    yf, yb, f_ssd = _ssd_scan(u, b0, B, T, mp['dt_bias'], mp['ssd_a'], st_ssd)
    gf, gb, f_gla = _gla_scan(u, b0, B, T, mp['gla_up'], mp['gla_b'], mp['gla_pick_mid'], st_gla)
    rf, rb, gate, bonus, f_rwkv = _rwkv_scan(u, b0, B, T, mp['rwkv'], s_rwkv)
    mix = _epilogue(u, b0, B, T, (yf, yb), (gf, gb), (rf, rb, gate, bonus), mp['epi'])
    f_ssd = f_ssd.reshape(B, 2, SSD_STATE, SSD_HEADS, SSD_HEAD_DIM).transpose(0, 1, 3, 4, 2)
    f_gla = f_gla.reshape(B, 2, GLA_DV, GLA_HEADS, GLA_DK).transpose(0, 1, 3, 4, 2)
    return mix, (f_ssd, f_gla, f_rwkv)


def kernel(x_prompt, x_sample, state_ssd, state_gla, state_rwkv, c, c_ctx, ada_w, ada_b, norm_mix_g, norm_ffn_g,
           w_in, ssd_conv_w, ssd_conv_b, ssd_dt_bias, ssd_a_log, ssd_d, ssd_norm_g, gla_gate_up, gla_gate_b,
           gla_norm_g, rwkv_mu, rwkv_w0, rwkv_w2, rwkv_a0, rwkv_a2, rwkv_g2, rwkv_k_k, rwkv_k_a, rwkv_r_k,
           rwkv_ln_g, rwkv_ln_b, w_out, router_w, router_b, moe_w_gate, moe_w_up, moe_w_down, final_norm_g):
    p = dict(ssd_conv_w=ssd_conv_w, ssd_conv_b=ssd_conv_b, ssd_dt_bias=ssd_dt_bias, ssd_a_log=ssd_a_log,
             ssd_d=ssd_d, ssd_norm_g=ssd_norm_g, gla_gate_up=gla_gate_up, gla_gate_b=gla_gate_b,
             gla_norm_g=gla_norm_g, rwkv_mu=rwkv_mu, rwkv_w0=rwkv_w0, rwkv_w2=rwkv_w2, rwkv_a0=rwkv_a0,
             rwkv_a2=rwkv_a2, rwkv_g2=rwkv_g2, rwkv_k_k=rwkv_k_k, rwkv_k_a=rwkv_k_a, rwkv_r_k=rwkv_r_k,
             rwkv_ln_g=rwkv_ln_g, rwkv_ln_b=rwkv_ln_b)
    D = D_MODEL
    nb, seq = x_prompt.shape[:2]
    db, dseq = x_sample.shape[:2]
    grp = nb * seq
    assert dseq == grp and seq % CHUNK == 0 and dseq == GRID_W * GRID_W
    G = 1 + db
    x = jnp.concatenate([x_prompt.reshape(1, grp, D), x_sample], axis=0)
    cvec = jnp.concatenate([c_ctx[None, :], c, jnp.zeros((8 - G, D), F32)], axis=0)
    mods = _ada(cvec, ada_w, ada_b)[:, :G]
    router_w_pad = jnp.pad(router_w, ((0, 0), (0, LANE - N_EXPERTS)))
    zero_states = (jnp.zeros((nb, 2, SSD_HEADS, SSD_HEAD_DIM, SSD_STATE), F32),
                   jnp.zeros((nb, 2, GLA_HEADS, GLA_DK, GLA_DV), F32),
                   jnp.zeros((nb, 2, RWKV_HEADS, RWKV_N, RWKV_N), F32))
    def regrid_latent(t):
        return t[1:].reshape(db, GRID_W, GRID_W, D).swapaxes(1, 2).reshape(db, dseq, D)

    ctx_states = []
    col_major = False
    for l in range(DEPTH):
        sh1, sc1, g1, sh2, sc2, g2 = jnp.split(mods[l], 6, axis=-1)
        mp = _mixer_params(p, l)
        if col_major != (l % 2 == 1):
            x, col_major = x.at[1:].set(regrid_latent(x)), not col_major
        u = _inproj(x, norm_mix_g[l], sh1, sc1, _regroup_w_in(w_in[l]))
        u = _conv_inplace(u, mp['conv_w'], mp['conv_b'], 0, 1, seq, 1)
        u = _conv_inplace(u, mp['conv_w'], mp['conv_b'], 1, db, GRID_W, dseq // GRID_W, col_major)
        mix_p, new = _token_mixers(u.reshape(G * nb, seq, P_IN_PAD), 0, nb, seq, zero_states, mp)
        s0 = (state_ssd[:, l], state_gla[:, l], state_rwkv[:, l])
        mix_s, _ = _token_mixers(u, 1, db, dseq, s0, mp)
        ctx_states.append(new)
        mix = jnp.concatenate([mix_p.reshape(1, grp, D), mix_s], axis=0)
        x = _outproj(mix, w_out[l].astype(BF16), x, g1)
        h, logits = _ffn_in(x, norm_ffn_g[l], sh2, sc2, router_w_pad)
        moe = _moe(h.reshape(G * grp, D), logits.reshape(G * grp, LANE)[:, :N_EXPERTS], router_b,
                   moe_w_gate, moe_w_up, moe_w_down, l)
        x = _ffn_residual(x, moe.reshape(MOE_TOP_K, G, grp, D), g2, final_norm_g if l == DEPTH - 1 else None)
    y_prompt = x[0].reshape(nb, seq, D)
    y_sample = regrid_latent(x) if col_major else x[1:]
    new_ssd = jnp.stack([s[0] for s in ctx_states], axis=1)
    new_gla = jnp.stack([s[1] for s in ctx_states], axis=1)
    new_rwkv = jnp.stack([s[2] for s in ctx_states], axis=1)
    return (y_prompt, y_sample, new_ssd, new_gla, new_rwkv)
```
